```python
import math, functools
import jax, jax.numpy as jnp
from jax import lax
import numpy as np

D_MODEL = 1024
BATCH = 4
SEQ = 8192
DEPTH = 2

GRID_W = 64
CTX_LEN = 256
EPS = 1e-6

DN_HEADS = 4
DN_DK = 128
DN_DV = 128
DN_CHUNK = 64
DN_CONV_W = 3
QK_W = DN_HEADS * DN_DK
V_W = DN_HEADS * DN_DV
QKV_W = 2 * QK_W + V_W
POOL_WINDOWS = (2, 4, 8, 16)
POOL_GROUP = D_MODEL // 8
POOL_WIDTH = len(POOL_WINDOWS) * POOL_GROUP
MIX_WIDTH = V_W + POOL_WIDTH
EVEN_IN = QKV_W + V_W + 4 * DN_HEADS + POOL_WIDTH
SC_CONV_W = 3
N_EXPERTS = 16
D_EXPERT = D_MODEL // 2
EC_CAPACITY_FACTOR = 2

N_EVEN = (DEPTH + 1) // 2
N_ODD = DEPTH // 2

kernel_name = "hybrid_dit_deltanet_pool_shortconv_ecmoe"


def rms_norm(x, g):
    xf = x.astype(jnp.float32)
    y = xf * lax.rsqrt(jnp.mean(xf * xf, axis=-1, keepdims=True) + EPS)
    return (y * g.astype(jnp.float32)).astype(x.dtype)


def modulate(h, shift, scale):
    return h * (1 + scale[:, None]) + shift[:, None]


def dw_conv(x, w):
    k = w.shape[0]
    pad_l = (k - 1) // 2
    return lax.conv_general_dilated(
        x, w[:, None, :].astype(x.dtype), window_strides=(1,), padding=[(pad_l, k - 1 - pad_l)],
        dimension_numbers=('NWC', 'WIO', 'NWC'), feature_group_count=x.shape[-1])


def l2norm(t):
    return t * lax.rsqrt(jnp.sum(t * t, axis=-1, keepdims=True) + EPS)


def to_heads(t, d):
    b, L, _ = t.shape
    return t.reshape(b, L, DN_HEADS, d).transpose(0, 2, 1, 3)


def gated_delta_chunked(q, k, v, beta, log_decay, s0):
    b, h, L, dk = k.shape
    dv = v.shape[-1]
    c = DN_CHUNK
    n = L // c
    k = k.reshape(b, h, n, c, dk)
    v = v.reshape(b, h, n, c, dv)
    beta = beta.reshape(b, h, n, c)
    gam = jnp.cumsum(log_decay.reshape(b, h, n, c), axis=-1)
    incl = jnp.tril(jnp.ones((c, c), dtype=bool))
    strict = jnp.tril(jnp.ones((c, c), dtype=bool), k=-1)
    diff = gam[..., :, None] - gam[..., None, :]
    decay = jnp.where(incl, jnp.exp(jnp.where(incl, diff, 0.0)), 0.0)
    kk = jnp.einsum('bhnid,bhnjd->bhnij', k, k)
    t_mat = jnp.eye(c, dtype=jnp.float32) + jnp.where(strict, kk * decay, 0.0) * beta[..., :, None]
    solve = functools.partial(lax.linalg.triangular_solve, left_side=True, lower=True, unit_diagonal=True)
    u = solve(t_mat, v * beta[..., None])
    wk = solve(t_mat, k * (beta * jnp.exp(gam))[..., None])
    k_dec = k * jnp.exp(gam[..., -1:] - gam)[..., None]
    chunk_decay = jnp.exp(gam[..., -1])[..., None, None]
    to_scan = lambda t: jnp.moveaxis(t, 2, 0)

    def advance(s, u_c, wk_c, kd_c, cd_c):
        w = u_c - jnp.einsum('bhck,bhkv->bhcv', wk_c, s)
        return w, cd_c * s + jnp.einsum('bhck,bhcv->bhkv', kd_c, w)

    if q is None:
        def step_state(s, inp):
            _, s_new = advance(s, *inp)
            return s_new, None
        s_fin, _ = lax.scan(step_state, s0, tuple(map(to_scan, (u, wk, k_dec, chunk_decay))))
        return None, s_fin

    q = q.reshape(b, h, n, c, dk)
    qk = jnp.einsum('bhnid,bhnjd->bhnij', q, k) * decay
    q_dec = q * jnp.exp(gam)[..., None]

    def step(s, inp):
        u_c, wk_c, kd_c, cd_c, qk_c, qd_c = inp
        w, s_new = advance(s, u_c, wk_c, kd_c, cd_c)
        o = jnp.einsum('bhck,bhkv->bhcv', qd_c, s) + jnp.einsum('bhij,bhjv->bhiv', qk_c, w)
        return s_new, o

    s_fin, o = lax.scan(step, s0, tuple(map(to_scan, (u, wk, k_dec, chunk_decay, qk, q_dec))))
    return jnp.moveaxis(o, 0, 2).reshape(b, h, L, dv), s_fin


def even_project(h, w_in, conv_w, a_log, dt_bias):
    proj = h @ w_in
    qkv, z, gates, pool_in = jnp.split(proj, [QKV_W, QKV_W + V_W, QKV_W + V_W + 4 * DN_HEADS], axis=-1)
    qkv = jax.nn.silu(dw_conv(qkv, conv_w)).astype(jnp.float32)
    q, k, v = jnp.split(qkv, [QK_W, 2 * QK_W], axis=-1)
    q = l2norm(to_heads(q, DN_DK)) * DN_DK ** -0.5
    k = l2norm(to_heads(k, DN_DK))
    v = to_heads(v, DN_DV)
    b, L, _ = gates.shape
    gates = gates.astype(jnp.float32).reshape(b, L, 4, DN_HEADS).transpose(2, 0, 3, 1)
    beta = jax.nn.sigmoid(gates[:2])
    a_log = a_log.astype(jnp.float32)[:, None, :, None]
    dt_bias = dt_bias.astype(jnp.float32)[:, None, :, None]
    log_decay = -jnp.exp(a_log) * jax.nn.softplus(gates[2:] + dt_bias)
    return q, k, v, z, beta, log_decay, pool_in


def centred_mean(u, n_seg, seg_len, w):
    b, L, ch = u.shape
    us = u.astype(jnp.float32).reshape(b, n_seg, seg_len, ch)
    cs = jnp.concatenate([jnp.zeros_like(us[:, :, :1]), jnp.cumsum(us, axis=2)], axis=2)
    t = jnp.arange(seg_len)
    lo = jnp.clip(t - w // 2, 0, seg_len)
    hi = jnp.clip(t + w - w // 2, 0, seg_len)
    cnt = (hi - lo).astype(jnp.float32)[:, None]
    return ((cs[:, :, hi] - cs[:, :, lo]) / cnt).reshape(b, L, ch).astype(u.dtype)


def even_output(o, z, pool_in, o_norm, pool_w, pool_scale, w_out, n_seg, seg_len):
    b, _, L, _ = o.shape
    o = o.transpose(0, 2, 1, 3)
    zh = z.reshape(b, L, DN_HEADS, DN_DV).astype(jnp.float32)
    o = (rms_norm(o, o_norm) * jax.nn.silu(zh)).reshape(b, L, V_W).astype(z.dtype)
    groups = jnp.stack([centred_mean(pg, n_seg, seg_len, w) - pg
                        for pg, w in zip(jnp.split(pool_in, len(POOL_WINDOWS), axis=-1), POOL_WINDOWS)], axis=2)
    pooled = jnp.einsum('blgc,gcd->blgd', groups, pool_w).reshape(b, L, POOL_WIDTH) * pool_scale
    return jnp.concatenate([o, pooled], axis=-1) @ w_out


def even_mixer(hl, hc, w_in, conv_w, a_log, dt_bias, o_norm, pool_w, pool_scale, w_out, rows, ctx_out):
    ql, kl, vl, zl, betal, gl, pl = even_project(hl, w_in, conv_w, a_log, dt_bias)
    qc, kc, vc, zc, betac, gc, pc = even_project(hc, w_in, conv_w, a_log, dt_bias)
    flip = lambda t: jnp.flip(t, axis=2)
    s0 = jnp.zeros((hl.shape[0], DN_HEADS, DN_DK, DN_DV), jnp.float32)
    oc_f, sc_f = gated_delta_chunked(qc if ctx_out else None, kc, vc, betac[0], gc[0], s0)
    oc_b, sc_b = gated_delta_chunked(flip(qc) if ctx_out else None, flip(kc), flip(vc), flip(betac[1]), flip(gc[1]), s0)
    ol_f, _ = gated_delta_chunked(ql, kl, vl, betal[0], gl[0], sc_f)
    ol_b, _ = gated_delta_chunked(flip(ql), flip(kl), flip(vl), flip(betal[1]), flip(gl[1]), sc_b)
    y_lat = even_output(ol_f + flip(ol_b), zl, pl, o_norm, pool_w, pool_scale, w_out, rows, GRID_W)
    if not ctx_out:
        return y_lat, None
    y_ctx = even_output(oc_f + flip(oc_b), zc, pc, o_norm, pool_w, pool_scale, w_out, 1, hc.shape[1])
    return y_lat, y_ctx


def short_conv_mixer(h, w_in, conv_w, w_out):
    gb, gc, xin = jnp.split(h @ w_in, 3, axis=-1)
    return (gb * dw_conv(gc * xin, conv_w)) @ w_out


def expert_choice_ffn(h, router, w_gate, w_up, w_down):
    b, n, _ = h.shape
    cap = EC_CAPACITY_FACTOR * n // N_EXPERTS
    aff = jax.nn.softmax(jnp.einsum('bnd,de->bne', h, router).astype(jnp.float32), axis=-1)
    gate, idx = lax.top_k(aff.transpose(0, 2, 1), cap)
    bidx = jnp.arange(b)[:, None, None]
    xg = h[bidx, idx]
    hid = jax.nn.silu(jnp.einsum('becd,edf->becf', xg, w_gate)) * jnp.einsum('becd,edf->becf', xg, w_up)
    y = jnp.einsum('becf,efd->becd', hid, w_down) * gate[..., None].astype(h.dtype)
    return jnp.zeros_like(h).at[bidx, idx].add(y)


def setup_inputs(seed: int = 0) -> dict:
    key = jax.random.key(seed)
    ks = jax.random.split(key, 24)
    f32 = jnp.float32
    D = D_MODEL
    nrm = lambda k, shape, s: jax.random.normal(k, shape, f32) * s
    dt = jnp.exp(jax.random.uniform(ks[10], (N_EVEN, 2, DN_HEADS), f32, math.log(1e-3), math.log(1e-1)))
    return {
        "x": nrm(ks[0], (BATCH, SEQ, D), 1.0),
        "c": nrm(ks[1], (BATCH, D), 1.0),
        "ctx": nrm(ks[2], (BATCH, CTX_LEN, D), 1.0),
        "c_ctx": nrm(ks[3], (D,), 1.0),
        "ada_w": nrm(ks[4], (DEPTH, D, 6 * D), 0.5 * D ** -0.5),
        "ada_b": nrm(ks[5], (DEPTH, 6 * D), 0.02),
        "norm_mix": 1.0 + nrm(ks[6], (DEPTH, D), 0.02),
        "norm_ffn": 1.0 + nrm(ks[7], (DEPTH, D), 0.02),
        "norm_final": 1.0 + nrm(ks[8], (D,), 0.02),
        "ev_w_in": nrm(ks[9], (N_EVEN, D, EVEN_IN), D ** -0.5),
        "dn_conv": nrm(ks[11], (N_EVEN, DN_CONV_W, QKV_W), DN_CONV_W ** -0.5),
        "dn_a_log": jnp.log(jax.random.uniform(ks[12], (N_EVEN, 2, DN_HEADS), f32, 1.0, 16.0)),
        "dn_dt_bias": dt + jnp.log(-jnp.expm1(-dt)),
        "dn_norm": 1.0 + nrm(ks[13], (N_EVEN, DN_DV), 0.02),
        "pool_w": nrm(ks[14], (N_EVEN, len(POOL_WINDOWS), POOL_GROUP, POOL_GROUP), POOL_GROUP ** -0.5),
        "pool_scale": 1.0 + nrm(ks[15], (N_EVEN, POOL_WIDTH), 0.1),
        "ev_w_out": nrm(ks[16], (N_EVEN, MIX_WIDTH, D), MIX_WIDTH ** -0.5),
        "sc_w_in": nrm(ks[17], (N_ODD, D, 3 * D), D ** -0.5),
        "sc_conv": nrm(ks[18], (N_ODD, SC_CONV_W, D), SC_CONV_W ** -0.5),
        "sc_w_out": nrm(ks[19], (N_ODD, D, D), D ** -0.5),
        "router": nrm(ks[20], (DEPTH, D, N_EXPERTS), D ** -0.5),
        "w_gate": nrm(ks[21], (DEPTH, N_EXPERTS, D, D_EXPERT), D ** -0.5),
        "w_up": nrm(ks[22], (DEPTH, N_EXPERTS, D, D_EXPERT), D ** -0.5),
        "w_down": nrm(ks[23], (DEPTH, N_EXPERTS, D_EXPERT, D), D_EXPERT ** -0.5),
    }


def reference(x, c, ctx, c_ctx, ada_w, ada_b, norm_mix, norm_ffn, norm_final,
              ev_w_in, dn_conv, dn_a_log, dn_dt_bias, dn_norm, pool_w, pool_scale, ev_w_out,
              sc_w_in, sc_conv, sc_w_out, router, w_gate, w_up, w_down):
    rows = x.shape[1] // GRID_W
    xl, xc = x, ctx
    for i in range(DEPTH):
        j = i // 2
        ctx_read = i % 2 == 0
        ctx_live = any(l % 2 == 0 for l in range(i + 1, DEPTH))
        sh1_l, sc1_l, g1_l, sh2_l, sc2_l, g2_l = jnp.split(jax.nn.silu(c) @ ada_w[i] + ada_b[i], 6, axis=-1)
        hl = modulate(rms_norm(xl, norm_mix[i]), sh1_l, sc1_l)
        yc = None
        if ctx_read or ctx_live:
            sh1_c, sc1_c, g1_c, sh2_c, sc2_c, g2_c = jnp.split((jax.nn.silu(c_ctx) @ ada_w[i] + ada_b[i])[None], 6, axis=-1)
            hc = modulate(rms_norm(xc, norm_mix[i]), sh1_c, sc1_c)
        if ctx_read:
            yl, yc = even_mixer(hl, hc, ev_w_in[j], dn_conv[j], dn_a_log[j], dn_dt_bias[j], dn_norm[j],
                                pool_w[j], pool_scale[j], ev_w_out[j], rows, ctx_live)
        else:
            yl = short_conv_mixer(hl, sc_w_in[j], sc_conv[j], sc_w_out[j])
            if ctx_live:
                yc = short_conv_mixer(hc, sc_w_in[j], sc_conv[j], sc_w_out[j])
        xl = xl + g1_l[:, None] * yl
        hl2 = modulate(rms_norm(xl, norm_ffn[i]), sh2_l, sc2_l)
        xl = xl + g2_l[:, None] * expert_choice_ffn(hl2, router[i], w_gate[i], w_up[i], w_down[i])
        if ctx_live:
            xc = xc + g1_c[:, None] * yc
            hc2 = modulate(rms_norm(xc, norm_ffn[i]), sh2_c, sc2_c)
            xc = xc + g2_c[:, None] * expert_choice_ffn(hc2, router[i], w_gate[i], w_up[i], w_down[i])
    return rms_norm(xl, norm_final)
```

```python
import functools

import jax
import jax.numpy as jnp
from jax import lax
from jax.experimental import pallas as pl
from jax.experimental.pallas import tpu as pltpu

F32 = jnp.float32
BF16 = jnp.bfloat16
I32 = jnp.int32

EPS = 1e-6
GRID_W = 64
HEADS = 4
HEAD_DIM = 128
CHUNK = 64
POOL_WINDOWS = (2, 4, 8, 16)
POOL_GROUP = 128
LANES = 128
SUBLANES = 8
VMEM_LIMIT = 56 * 1024 * 1024
VALID_BIT = 1 << 30


def _silu(x):
    return x * jax.nn.sigmoid(x)


def _norm_mod(x, g, shift, scale):
    ms = jnp.mean(x * x, axis=-1, keepdims=True)
    return (x * lax.rsqrt(ms + EPS) * g) * (1.0 + scale) + shift


def _dot(a, b):
    return jnp.dot(a, b, preferred_element_type=F32)


def _dot_nt(a, b):
    return lax.dot_general(a, b, (((1,), (1,)), ((), ())), preferred_element_type=F32)


def _dot_tn(a, b):
    return lax.dot_general(a, b, (((0,), (0,)), ((), ())), preferred_element_type=F32)


def _split3(x):
    hi = x.astype(BF16)
    r = x - hi.astype(F32)
    mid = r.astype(BF16)
    lo = (r - mid.astype(F32)).astype(BF16)
    return hi, mid, lo


def _const_spec(shape):
    nd = len(shape)
    return pl.BlockSpec(shape, lambda *_: (0,) * nd, pipeline_mode=pl.Buffered(1))


def _params(*sem):
    return pltpu.CompilerParams(dimension_semantics=sem, vmem_limit_bytes=VMEM_LIMIT)


def _ada_kernel(c_ref, w_ref, b_ref, o_ref):
    s = _silu(c_ref[...])
    o_ref[0] = _dot(s.astype(BF16), w_ref[0].astype(BF16)) + b_ref[0]


def _ada_call(cc, ada_w, ada_b):
    depth, d, n6 = ada_w.shape
    tn = n6 // 4
    return pl.pallas_call(
        _ada_kernel,
        out_shape=jax.ShapeDtypeStruct((depth, SUBLANES, n6), F32),
        grid=(depth, n6 // tn),
        in_specs=[
            pl.BlockSpec((SUBLANES, d), lambda i, j: (0, 0)),
            pl.BlockSpec((1, d, tn), lambda i, j: (i, 0, j)),
            pl.BlockSpec((1, 1, tn), lambda i, j: (i, 0, j)),
        ],
        out_specs=pl.BlockSpec((1, SUBLANES, tn), lambda i, j: (i, 0, j)),
        compiler_params=_params("arbitrary", "arbitrary"),
        name="adaln",
    )(cc, ada_w, ada_b.reshape(depth, 1, n6))


def _proj0_kernel(xp_ref, x_ref, xn_ref, sh_ref, sc_ref, g_ref, wqkv_ref, wrest_ref, conv_ref, gpar_ref,
                  q_ref, k_ref, v_ref, z_ref, p_ref, gt_ref, *, tm):
    i = pl.program_id(1)
    last = pl.num_programs(1) - 1
    x = jnp.concatenate([xp_ref[0], x_ref[0], xn_ref[0]], axis=0)
    h = _norm_mod(x, g_ref[...], sh_ref[0], sc_ref[0])
    row = lax.broadcasted_iota(I32, (tm + 2 * SUBLANES, 1), 0)
    dead = ((row < SUBLANES) & (i == 0)) | ((row >= tm + SUBLANES) & (i == last))
    hb = jnp.where(dead, 0.0, h).astype(BF16)
    proj = _dot(hb, wqkv_ref[...])
    cw = conv_ref[...]
    lo = SUBLANES
    a = (proj[lo - 1:lo - 1 + tm] * cw[0:1] + proj[lo:lo + tm] * cw[1:2] + proj[lo + 1:lo + 1 + tm] * cw[2:3])
    a = _silu(a)
    w = HEADS * HEAD_DIM
    for hd in range(HEADS):
        sl = slice(hd * HEAD_DIM, (hd + 1) * HEAD_DIM)
        qh = a[:, sl]
        kh = a[:, w + hd * HEAD_DIM: w + (hd + 1) * HEAD_DIM]
        q_ref[0, :, sl] = qh * lax.rsqrt(jnp.sum(qh * qh, axis=-1, keepdims=True) + EPS) * (HEAD_DIM ** -0.5)
        k_ref[0, :, sl] = kh * lax.rsqrt(jnp.sum(kh * kh, axis=-1, keepdims=True) + EPS)
    v_ref[0] = a[:, 2 * w:]
    rest = _dot(hb[lo:lo + tm], wrest_ref[...])
    z_ref[0] = rest[:, :w]
    p_ref[0] = rest[:, w:2 * w]
    gates = rest[:, 2 * w:]
    col = lax.broadcasted_iota(I32, (1, LANES), 1)
    xb = gates + gpar_ref[1:2]
    softplus = jnp.maximum(xb, 0.0) + jnp.log1p(jnp.exp(-jnp.abs(xb)))
    log_decay = -jnp.exp(gpar_ref[0:1]) * softplus
    out = jnp.where(col < 2 * HEADS, jax.nn.sigmoid(gates), log_decay)
    gt_ref[0] = out[:, :4 * HEADS]


def _proj0_call(x, shift, scale, gain, wqkv, wrest, conv_w, gpar, tm):
    b, l, d = x.shape
    tm = min(tm, l)
    nt = l // tm
    nb8 = l // SUBLANES
    r = tm // SUBLANES
    w = HEADS * HEAD_DIM
    tok = lambda width: pl.BlockSpec((1, tm, width), lambda bi, i: (bi, i, 0))
    vec = pl.BlockSpec((1, 1, d), lambda bi, i: (bi, 0, 0))
    return pl.pallas_call(
        functools.partial(_proj0_kernel, tm=tm),
        out_shape=[jax.ShapeDtypeStruct((b, l, w), F32)] * 5 + [jax.ShapeDtypeStruct((b, l, 4 * HEADS), F32)],
        grid=(b, nt),
        in_specs=[
            pl.BlockSpec((1, SUBLANES, d), lambda bi, i: (bi, jnp.maximum(i * r - 1, 0), 0)),
            tok(d),
            pl.BlockSpec((1, SUBLANES, d), lambda bi, i: (bi, jnp.minimum((i + 1) * r, nb8 - 1), 0)),
            vec, vec,
            _const_spec((1, d)),
            _const_spec(wqkv.shape),
            _const_spec(wrest.shape),
            _const_spec(conv_w.shape),
            _const_spec(gpar.shape),
        ],
        out_specs=[tok(w)] * 5 + [tok(4 * HEADS)],
        compiler_params=_params("arbitrary", "arbitrary"),
        name="proj0",
    )(x, x, x, shift, scale, gain, wqkv, wrest, conv_w, gpar)


def _delta_prep_kernel(q_ref, k_ref, v_ref, g_ref, gt_ref, u_ref, wk_ref, qd_ref, kdt_ref, qk_ref, cd_ref):
    c = CHUNK
    ri = lax.broadcasted_iota(I32, (c, c), 0)
    ci = lax.broadcasted_iota(I32, (c, c), 1)
    tri_l = jnp.where(ri >= ci, 1.0, 0.0).astype(BF16)
    tri_u = jnp.where(ri <= ci, 1.0, 0.0).astype(BF16)
    eye = jnp.where(ri == ci, 1.0, 0.0)
    g = g_ref[0]
    g3 = _split3(g)
    gt3 = _split3(gt_ref[0, 0])
    cols = (sum(_dot(tri_l, p) for p in g3), sum(_dot(tri_u, p) for p in g3))
    rows = (sum(_dot(p, tri_u) for p in gt3), sum(_dot(p, tri_l) for p in gt3))
    incl = (ri >= ci, ri <= ci)
    strict = (ri > ci, ri < ci)
    tot = (c - 1, 0)

    heads = range(HEADS)
    sls = [slice(hd * HEAD_DIM, (hd + 1) * HEAD_DIM) for hd in heads]
    ks = [k_ref[0, :, sl] for sl in sls]
    kbs = [k.astype(BF16) for k in ks]
    qs = [q_ref[0, :, sl] for sl in sls]
    kk = [_dot_nt(kb, kb) for kb in kbs]
    qk0 = [_dot_nt(q.astype(BF16), kb) for q, kb in zip(qs, kbs)]

    combos = [(d, hd) for d in range(2) for hd in heads]
    beta, gam, gtot, decay, a = {}, {}, {}, {}, {}
    for d, hd in combos:
        bcol = d * HEADS + hd
        dcol = 2 * HEADS + bcol
        beta[d, hd] = g[:, bcol:bcol + 1]
        gam[d, hd] = cols[d][:, dcol:dcol + 1]
        gtot[d, hd] = gam[d, hd][tot[d]:tot[d] + 1, :]
        diff = gam[d, hd] - rows[d][dcol:dcol + 1, :]
        decay[d, hd] = jnp.where(incl[d], jnp.exp(jnp.where(incl[d], diff, 0.0)), 0.0)
        a[d, hd] = jnp.where(strict[d], kk[hd] * decay[d, hd], 0.0) * beta[d, hd]
    p = dict(a)
    tinv = {key: eye - a[key] for key in combos}
    for _ in range(5):
        pb = {key: p[key].astype(BF16) for key in combos}
        p = {key: _dot(pb[key], pb[key]) for key in combos}
        tinv = {key: tinv[key] + _dot(tinv[key].astype(BF16), p[key].astype(BF16)) for key in combos}
    eg = {key: jnp.exp(gam[key]) for key in combos}
    uw = {}
    for d, hd in combos:
        rhs = jnp.concatenate([v_ref[0, :, sls[hd]] * beta[d, hd], ks[hd] * (beta[d, hd] * eg[d, hd])], axis=1)
        uw[d, hd] = _dot(tinv[d, hd].astype(BF16), rhs.astype(BF16))
    for d, hd in combos:
        sl = sls[hd]
        u_ref[0, d, :, sl] = uw[d, hd][:, :HEAD_DIM]
        wk_ref[0, d, :, sl] = uw[d, hd][:, HEAD_DIM:].astype(BF16)
        qd_ref[0, d, :, sl] = (qs[hd] * eg[d, hd]).astype(BF16)
        kd = ks[hd] * jnp.exp(gtot[d, hd] - gam[d, hd])
        kdt_ref[0, d, 0, :, hd * c:(hd + 1) * c] = kd.T.astype(BF16)
        qk_ref[0, d, :, hd * c:(hd + 1) * c] = (qk0[hd] * decay[d, hd]).astype(BF16)
    for d in range(2):
        cd = [jnp.broadcast_to(jnp.exp(gtot[d, hd]), (1, LANES)) for hd in heads]
        cd_ref[0, d, 0] = jnp.concatenate(cd + [jnp.zeros((SUBLANES - HEADS, LANES), F32)], axis=0)


def _delta_prep_call(q, k, v, g, gt):
    b, l, w = k.shape
    nc = l // CHUNK
    tok = lambda width: pl.BlockSpec((1, CHUNK, width), lambda bi, n: (bi, n, 0))
    dtok = lambda width: pl.BlockSpec((1, 2, CHUNK, width), lambda bi, n: (bi, 0, n, 0))
    return pl.pallas_call(
        _delta_prep_kernel,
        out_shape=[
            jax.ShapeDtypeStruct((b, 2, l, w), F32),
            jax.ShapeDtypeStruct((b, 2, l, w), BF16),
            jax.ShapeDtypeStruct((b, 2, l, w), BF16),
            jax.ShapeDtypeStruct((b, 2, nc, HEAD_DIM, HEADS * CHUNK), BF16),
            jax.ShapeDtypeStruct((b, 2, l, HEADS * CHUNK), BF16),
            jax.ShapeDtypeStruct((b, 2, nc, SUBLANES, LANES), F32),
        ],
        grid=(b, nc),
        in_specs=[tok(w), tok(w), tok(w), tok(4 * HEADS),
                  pl.BlockSpec((1, 1, 4 * HEADS, CHUNK), lambda bi, n: (bi, n, 0, 0))],
        out_specs=[dtok(w), dtok(w), dtok(w),
                   pl.BlockSpec((1, 2, 1, HEAD_DIM, HEADS * CHUNK), lambda bi, n: (bi, 0, n, 0, 0)),
                   dtok(HEADS * CHUNK),
                   pl.BlockSpec((1, 2, 1, SUBLANES, LANES), lambda bi, n: (bi, 0, n, 0, 0))],
        compiler_params=_params("arbitrary", "arbitrary"),
        name="delta_prep",
    )(q, k, v, g, gt)


def _delta_scan_kernel(*refs):
    ins = (refs[0:6], refs[6:12])
    s0_ref, of_ref, ob_ref, sfin_ref, s_ref = refs[12:]
    outs = (of_ref, ob_ref)
    n = pl.program_id(1)
    c = CHUNK

    @pl.when(n == 0)
    def _():
        s_ref[...] = s0_ref[0]

    combos = [(d, hd) for d in range(2) for hd in range(HEADS)]
    sl = lambda hd: slice(hd * HEAD_DIM, (hd + 1) * HEAD_DIM)
    cs = lambda hd: slice(hd * c, (hd + 1) * c)
    s = {(d, hd): s_ref[d, hd] for d, hd in combos}
    m1, w, m2 = {}, {}, {}
    for d, hd in combos:
        u_ref, wk_ref, qd_ref, kdt_ref, qk_ref, cd_ref = ins[d]
        lhs = jnp.concatenate([wk_ref[0, 0, :, sl(hd)], qd_ref[0, 0, :, sl(hd)]], axis=0)
        m1[d, hd] = _dot(lhs, s[d, hd].astype(BF16))
    for d, hd in combos:
        w[d, hd] = (ins[d][0][0, 0, :, sl(hd)] - m1[d, hd][:c]).astype(BF16)
    for d, hd in combos:
        u_ref, wk_ref, qd_ref, kdt_ref, qk_ref, cd_ref = ins[d]
        lhs = jnp.concatenate([qk_ref[0, 0, :, cs(hd)], kdt_ref[0, 0, 0, :, cs(hd)]], axis=0)
        m2[d, hd] = _dot(lhs, w[d, hd])
    for d, hd in combos:
        outs[d][0, :, sl(hd)] = m1[d, hd][c:] + m2[d, hd][:c]
        s_ref[d, hd] = ins[d][5][0, 0, 0, hd:hd + 1, :] * s[d, hd] + m2[d, hd][c:]

    @pl.when(n == pl.num_programs(1) - 1)
    def _():
        sfin_ref[0] = s_ref[...]


def _delta_scan_call(prep, s0):
    u, wk, qd, kdt, qk, cd = prep
    b, _, l, w = u.shape
    nc = l // CHUNK

    def specs(d, chunk):
        tok = lambda width: pl.BlockSpec((1, 1, CHUNK, width), lambda bi, n: (bi, d, chunk(n), 0))
        return [tok(w), tok(w), tok(w),
                pl.BlockSpec((1, 1, 1, HEAD_DIM, HEADS * CHUNK), lambda bi, n: (bi, d, chunk(n), 0, 0)),
                tok(HEADS * CHUNK),
                pl.BlockSpec((1, 1, 1, SUBLANES, LANES), lambda bi, n: (bi, d, chunk(n), 0, 0))]

    st = pl.BlockSpec((1, 2, HEADS, HEAD_DIM, HEAD_DIM), lambda bi, n: (bi, 0, 0, 0, 0))
    return pl.pallas_call(
        _delta_scan_kernel,
        out_shape=[jax.ShapeDtypeStruct((b, l, w), F32)] * 2
        + [jax.ShapeDtypeStruct((b, 2, HEADS, HEAD_DIM, HEAD_DIM), F32)],
        grid=(b, nc),
        in_specs=specs(0, lambda n: n) + specs(1, lambda n: nc - 1 - n) + [st],
        out_specs=[pl.BlockSpec((1, CHUNK, w), lambda bi, n: (bi, n, 0)),
                   pl.BlockSpec((1, CHUNK, w), lambda bi, n: (bi, nc - 1 - n, 0)), st],
        scratch_shapes=[pltpu.VMEM((2, HEADS, HEAD_DIM, HEAD_DIM), F32)],
        compiler_params=_params("arbitrary", "arbitrary"),
        name="delta_scan",
    )(*prep, *prep, s0)


def _ffn_prenorm(x, nffn_ref, sh2_ref, sc2_ref, rt_ref, h2_ref, lg_ref):
    h2 = _norm_mod(x, nffn_ref[...], sh2_ref[0], sc2_ref[0])
    tm, d = x.shape
    pieces = d // LANES
    for j in range(pieces):
        h2_ref[0, pl.ds(j, tm, stride=pieces), :] = h2[:, j * LANES:(j + 1) * LANES]
    lg_ref[0] = _dot_nt(rt_ref[...], h2.astype(BF16))


def _out0_kernel(of_ref, ob_ref, z_ref, p_ref, x_ref, band_ref, cnt_ref, onorm_ref, pw_ref, ps_ref, wout_ref,
                 g1_ref, sh2_ref, sc2_ref, nffn_ref, rt_ref, x1_ref, h2_ref, lg_ref):
    o = of_ref[0] + ob_ref[0]
    z = z_ref[0]
    pin = p_ref[0]
    parts = []
    for hd in range(HEADS):
        sl = slice(hd * HEAD_DIM, (hd + 1) * HEAD_DIM)
        oh = o[:, sl]
        ms = jnp.mean(oh * oh, axis=-1, keepdims=True)
        parts.append(oh * lax.rsqrt(ms + EPS) * onorm_ref[...] * _silu(z[:, sl]))
    for gi in range(len(POOL_WINDOWS)):
        sl = slice(gi * POOL_GROUP, (gi + 1) * POOL_GROUP)
        u = pin[:, sl]
        uh = u.astype(BF16)
        ul = (u - uh.astype(F32)).astype(BF16)
        band = band_ref[gi]
        wsum = _dot(band, uh) + _dot(band, ul)
        grp = wsum / cnt_ref[gi] - u
        parts.append(_dot(grp.astype(BF16), pw_ref[gi]) * ps_ref[:, sl])
    cat = jnp.concatenate(parts, axis=1).astype(BF16)
    x1 = x_ref[0] + g1_ref[0] * _dot(cat, wout_ref[...])
    x1_ref[0] = x1
    _ffn_prenorm(x1, nffn_ref, sh2_ref, sc2_ref, rt_ref, h2_ref, lg_ref)


def _pool_tables(tm):
    t = jnp.arange(tm)
    seg = t // GRID_W
    loc = t % GRID_W
    bands, cnts = [], []
    for w in POOL_WINDOWS:
        lo = jnp.clip(loc - w // 2, 0, GRID_W)
        hi = jnp.clip(loc + w - w // 2, 0, GRID_W)
        inside = (seg[:, None] == seg[None, :]) & (loc[None, :] >= lo[:, None]) & (loc[None, :] < hi[:, None])
        bands.append(inside.astype(BF16))
        cnts.append((hi - lo).astype(F32)[:, None])
    return jnp.stack(bands), jnp.stack(cnts)


def _out0_call(o_f, o_b, z, pin, x, onorm, pool_w, pool_scale, w_out, g1, sh2, sc2, nffn, router_t, tm):
    b, l, d = x.shape
    tm = min(tm, l)
    w = HEADS * HEAD_DIM
    e = router_t.shape[0]
    band, cnt = _pool_tables(tm)
    tok = lambda width: pl.BlockSpec((1, tm, width), lambda bi, i: (bi, i, 0))
    vec = pl.BlockSpec((1, 1, d), lambda bi, i: (bi, 0, 0))
    return pl.pallas_call(
        _out0_kernel,
        out_shape=[jax.ShapeDtypeStruct((b, l, d), F32), jax.ShapeDtypeStruct((b, l * d // LANES, LANES), F32),
                   jax.ShapeDtypeStruct((b, e, l), F32)],
        grid=(b, l // tm),
        in_specs=[tok(w), tok(w), tok(w), tok(w), tok(d),
                  _const_spec(band.shape), _const_spec(cnt.shape), _const_spec(onorm.shape),
                  _const_spec(pool_w.shape), _const_spec(pool_scale.shape), _const_spec(w_out.shape),
                  vec, vec, vec, _const_spec(nffn.shape), _const_spec(router_t.shape)],
        out_specs=[tok(d), pl.BlockSpec((1, tm * d // LANES, LANES), lambda bi, i: (bi, i, 0)),
                   pl.BlockSpec((1, e, tm), lambda bi, i: (bi, 0, i))],
        compiler_params=_params("arbitrary", "arbitrary"),
        name="out0",
    )(o_f, o_b, z, pin, x, band, cnt, onorm, pool_w, pool_scale, w_out, g1, sh2, sc2, nffn, router_t)


def _excl_cumsum_lanes(m):
    rows, n = m.shape
    ri = lax.broadcasted_iota(I32, (LANES, LANES), 0)
    ci = lax.broadcasted_iota(I32, (LANES, LANES), 1)
    upper = jnp.where(ri <= ci, 1.0, 0.0).astype(BF16)
    carry = jnp.zeros((rows, 1), F32)
    outs = []
    for blk in range(n // LANES):
        x = m[:, blk * LANES:(blk + 1) * LANES]
        inc = _dot(x.astype(BF16), upper)
        outs.append(inc - x + carry)
        carry = carry + inc[:, LANES - 1:LANES]
    return jnp.concatenate(outs, axis=1)


def _route_kernel(lg_ref, idx_ref, gate_ref, *, cap):
    x = lg_ref[0]
    e, n = x.shape
    ex = jnp.exp(x - jnp.max(x, axis=0, keepdims=True))
    aff = ex / jnp.sum(ex, axis=0, keepdims=True)

    def count_ge(bits):
        return jnp.sum(jnp.where(aff >= pltpu.bitcast(bits, F32), 1.0, 0.0), axis=1, keepdims=True)

    def bisect(_, lohi):
        lo, hi = lohi
        mid = lo + ((hi - lo + 1) >> 1)
        ok = count_ge(mid) >= cap
        return jnp.where(ok, mid, lo), jnp.where(ok, hi, mid - 1)

    lo0 = jnp.zeros((e, 1), I32)
    hi0 = jnp.full((e, 1), 0x7F800000, I32)
    thr, _ = lax.fori_loop(0, 31, bisect, (lo0, hi0))
    above = jnp.where(aff >= pltpu.bitcast(thr + 1, F32), 1.0, 0.0)
    tied = jnp.where(aff >= pltpu.bitcast(thr, F32), 1.0, 0.0) - above
    need = cap - jnp.sum(above, axis=1, keepdims=True)
    sel = above + tied * jnp.where(_excl_cumsum_lanes(tied) < need, 1.0, 0.0)
    pos = _excl_cumsum_lanes(sel)
    tok = lax.broadcasted_iota(I32, (e, n), 1)
    key = jnp.where(sel > 0.0, (tok - pos.astype(I32)) | VALID_BIT, 0)
    shift = 1
    while shift < n:
        mk = pltpu.roll(key, n - shift, axis=1)
        mt = pltpu.roll(tok, n - shift, axis=1)
        ma = pltpu.roll(aff, n - shift, axis=1)
        take = (mk & shift) != 0
        key = jnp.where(take, mk, jnp.where((key & shift) == 0, key, 0))
        tok = jnp.where(take, mt, tok)
        aff = jnp.where(take, ma, aff)
        shift *= 2
    idx_ref[0] = tok[:, :cap]
    gate_ref[0] = aff[:, :cap]


def _route_call(logits_t, cap):
    b, e, n = logits_t.shape
    return pl.pallas_call(
        functools.partial(_route_kernel, cap=cap),
        out_shape=[jax.ShapeDtypeStruct((b, e, cap), I32), jax.ShapeDtypeStruct((b, e, cap), F32)],
        grid=(b,),
        in_specs=[pl.BlockSpec((1, e, n), lambda bi: (bi, 0, 0))],
        out_specs=[pl.BlockSpec((1, e, cap), lambda bi: (bi, 0, 0))] * 2,
        compiler_params=_params("arbitrary"),
        name="route",
    )(logits_t)


GATHER_UNROLL = 8


def _moe_up_kernel(idx_ref, src_ref, wgu_ref, hid_ref, rows_ref, *, cap, f):
    def gather(cidx, carry):
        for u in range(GATHER_UNROLL):
            s = cidx * GATHER_UNROLL + u
            t = idx_ref[0, 0, s]
            rows_ref[pl.ds(pl.multiple_of(s * SUBLANES, SUBLANES), SUBLANES), :] = (
                src_ref[0, pl.ds(pl.multiple_of(t * SUBLANES, SUBLANES), SUBLANES), :])
        return carry

    lax.fori_loop(0, cap // GATHER_UNROLL, gather, 0)
    x = jnp.concatenate([rows_ref[pl.ds(j, cap, stride=SUBLANES), :].astype(BF16) for j in range(SUBLANES)], axis=1)
    gu = _dot(x, wgu_ref[0])
    hid_ref[0, 0] = (_silu(gu[:, :f]) * gu[:, f:]).astype(BF16)


def _moe_up_call(idx, src, wgu, cap):
    b, nrows, _ = src.shape
    e, d = wgu.shape[:2]
    f = wgu.shape[2] // 2
    return pl.pallas_call(
        functools.partial(_moe_up_kernel, cap=cap, f=f),
        out_shape=jax.ShapeDtypeStruct((b, e, cap, f), BF16),
        grid=(b, e),
        in_specs=[
            pl.BlockSpec((1, 1, cap), lambda bi, ei: (bi * e + ei, 0, 0), memory_space=pltpu.SMEM),
            pl.BlockSpec((1, nrows, LANES), lambda bi, ei: (bi, 0, 0), pipeline_mode=pl.Buffered(1)),
            pl.BlockSpec((1, d, 2 * f), lambda bi, ei: (ei, 0, 0)),
        ],
        out_specs=pl.BlockSpec((1, 1, cap, f), lambda bi, ei: (bi, ei, 0, 0)),
        scratch_shapes=[pltpu.VMEM((cap * d // LANES, LANES), F32)],
        compiler_params=_params("arbitrary", "arbitrary"),
        name="moe_up",
    )(idx.reshape(b * e, 1, cap), src, wgu)


SCATTER_UNROLL = 8
DOWN_SPLIT = 2


def _moe_down_kernel(idx_ref, gate_ref, hid_ref, wd_ref, acc_ref, y_ref, *, cap, rows):
    @pl.when(pl.program_id(1) == 0)
    def _():
        acc_ref[...] = jnp.zeros_like(acc_ref)

    y = _dot(hid_ref[0, 0], wd_ref[0])
    for j in range(rows):
        y_ref[pl.ds(j, cap, stride=rows), :] = y[:, j * LANES:(j + 1) * LANES]

    def scatter(cidx, carry):
        toks, vals = [], []
        for u in range(SCATTER_UNROLL):
            s = cidx * SCATTER_UNROLL + u
            t = idx_ref[0, 0, s]
            piece = y_ref[pl.ds(pl.multiple_of(s * rows, rows), rows), :]
            dst = pl.ds(pl.multiple_of(t * rows, rows), rows)
            toks.append(dst)
            vals.append(acc_ref[0, dst, :] + gate_ref[0, 0, s] * piece)
        for dst, val in zip(toks, vals):
            acc_ref[0, dst, :] = val
        return carry

    lax.fori_loop(0, cap // SCATTER_UNROLL, scatter, 0)


def _moe_down_call(idx, gate, hid, wd, n):
    b, e, cap, f = hid.shape
    d = wd.shape[2]
    dw = d // DOWN_SPLIT
    rows = dw // LANES
    out = pl.pallas_call(
        functools.partial(_moe_down_kernel, cap=cap, rows=rows),
        out_shape=jax.ShapeDtypeStruct((b * DOWN_SPLIT, n * rows, LANES), F32),
        grid=(b * DOWN_SPLIT, e),
        in_specs=[
            pl.BlockSpec((1, 1, cap), lambda bh, ei: ((bh // DOWN_SPLIT) * e + ei, 0, 0), memory_space=pltpu.SMEM),
            pl.BlockSpec((1, 1, cap), lambda bh, ei: ((bh // DOWN_SPLIT) * e + ei, 0, 0), memory_space=pltpu.SMEM),
            pl.BlockSpec((1, 1, cap, f), lambda bh, ei: (bh // DOWN_SPLIT, ei, 0, 0)),
            pl.BlockSpec((1, f, dw), lambda bh, ei: (ei, 0, bh % DOWN_SPLIT)),
        ],
        out_specs=pl.BlockSpec((1, n * rows, LANES), lambda bh, ei: (bh, 0, 0)),
        scratch_shapes=[pltpu.VMEM((cap * rows, LANES), F32)],
        compiler_params=_params("arbitrary", "arbitrary"),
        name="moe_down",
    )(idx.reshape(b * e, 1, cap), gate.reshape(b * e, 1, cap), hid, wd)
    return out.reshape(b, DOWN_SPLIT, n * rows, LANES)


def _moe(h2, logits_t, wgu, wd):
    b, e, n = logits_t.shape
    cap = 2 * n // e
    idx, gate = _route_call(logits_t, cap)
    hid = _moe_up_call(idx, h2, wgu, cap)
    return _moe_down_call(idx, gate, hid, wd, n)


def _join(m_ref, tm):
    rows = m_ref.shape[2] // tm
    return jnp.concatenate([m_ref[0, h, pl.ds(j, tm, stride=rows), :]
                            for h in range(DOWN_SPLIT) for j in range(rows)], axis=1)


def _mix1_kernel(xp_ref, x_ref, xn_ref, mp_ref, m_ref, mn_ref, g2p_ref, sh1_ref, sc1_ref, nmix_ref, win_ref,
                 conv_ref, wout_ref, g1_ref, sh2_ref, sc2_ref, nffn_ref, rt_ref, x3_ref, h2_ref, lg_ref, *, tm):
    i = pl.program_id(1)
    last = pl.num_programs(1) - 1
    d = x_ref.shape[2]
    xs = jnp.concatenate([xp_ref[0], x_ref[0], xn_ref[0]], axis=0)
    ms = jnp.concatenate([_join(mp_ref, SUBLANES), _join(m_ref, tm), _join(mn_ref, SUBLANES)], axis=0)
    x2 = xs + g2p_ref[0] * ms
    h = _norm_mod(x2, nmix_ref[...], sh1_ref[0], sc1_ref[0])
    row = lax.broadcasted_iota(I32, (tm + 2 * SUBLANES, 1), 0)
    dead = ((row < SUBLANES) & (i == 0)) | ((row >= tm + SUBLANES) & (i == last))
    hb = jnp.where(dead, 0.0, h).astype(BF16)
    proj = _dot(hb, win_ref[...])
    lo = SUBLANES
    u = proj[:, d:2 * d] * proj[:, 2 * d:]
    cw = conv_ref[...]
    cv = u[lo - 1:lo - 1 + tm] * cw[0:1] + u[lo:lo + tm] * cw[1:2] + u[lo + 1:lo + 1 + tm] * cw[2:3]
    y = _dot((proj[lo:lo + tm, :d] * cv).astype(BF16), wout_ref[...])
    x3 = x2[lo:lo + tm] + g1_ref[0] * y
    x3_ref[0] = x3
    _ffn_prenorm(x3, nffn_ref, sh2_ref, sc2_ref, rt_ref, h2_ref, lg_ref)


def _mix1_call(x1, moe, g2p, sh1, sc1, nmix, w_in, conv_w, w_out, g1, sh2, sc2, nffn, router_t, tm):
    b, l, d = x1.shape
    tm = min(tm, l)
    nb8 = l // SUBLANES
    r = tm // SUBLANES
    e = router_t.shape[0]
    mr = d // DOWN_SPLIT // LANES
    prev = lambda bi, i: (bi, jnp.maximum(i * r - 1, 0), 0)
    nxt = lambda bi, i: (bi, jnp.minimum((i + 1) * r, nb8 - 1), 0)
    tok = pl.BlockSpec((1, tm, d), lambda bi, i: (bi, i, 0))
    vec = pl.BlockSpec((1, 1, d), lambda bi, i: (bi, 0, 0))
    return pl.pallas_call(
        functools.partial(_mix1_kernel, tm=tm),
        out_shape=[jax.ShapeDtypeStruct((b, l, d), F32), jax.ShapeDtypeStruct((b, l * d // LANES, LANES), F32),
                   jax.ShapeDtypeStruct((b, e, l), F32)],
        grid=(b, l // tm),
        in_specs=[
            pl.BlockSpec((1, SUBLANES, d), prev), tok, pl.BlockSpec((1, SUBLANES, d), nxt),
            pl.BlockSpec((1, DOWN_SPLIT, SUBLANES * mr, LANES), lambda bi, i: (bi, 0, jnp.maximum(i * r - 1, 0), 0)),
            pl.BlockSpec((1, DOWN_SPLIT, tm * mr, LANES), lambda bi, i: (bi, 0, i, 0)),
            pl.BlockSpec((1, DOWN_SPLIT, SUBLANES * mr, LANES),
                         lambda bi, i: (bi, 0, jnp.minimum((i + 1) * r, nb8 - 1), 0)),
            vec, vec, vec, _const_spec(nmix.shape), _const_spec(w_in.shape), _const_spec(conv_w.shape),
            _const_spec(w_out.shape), vec, vec, vec, _const_spec(nffn.shape), _const_spec(router_t.shape),
        ],
        out_specs=[tok, pl.BlockSpec((1, tm * d // LANES, LANES), lambda bi, i: (bi, i, 0)),
                   pl.BlockSpec((1, e, tm), lambda bi, i: (bi, 0, i))],
        compiler_params=_params("arbitrary", "arbitrary"),
        name="mix1",
    )(x1, x1, x1, moe, moe, moe, g2p, sh1, sc1, nmix, w_in, conv_w, w_out, g1, sh2, sc2, nffn, router_t)


def _final_kernel(x_ref, m_ref, g2_ref, nf_ref, o_ref):
    x = x_ref[0] + g2_ref[0] * _join(m_ref, x_ref.shape[1])
    ms = jnp.mean(x * x, axis=-1, keepdims=True)
    o_ref[0] = x * lax.rsqrt(ms + EPS) * nf_ref[...]


def _final_call(x3, moe, g2, nf, tm):
    b, l, d = x3.shape
    tm = min(tm, l)
    mr = d // DOWN_SPLIT // LANES
    tok = pl.BlockSpec((1, tm, d), lambda bi, i: (bi, i, 0))
    return pl.pallas_call(
        _final_kernel,
        out_shape=jax.ShapeDtypeStruct((b, l, d), F32),
        grid=(b, l // tm),
        in_specs=[tok, pl.BlockSpec((1, DOWN_SPLIT, tm * mr, LANES), lambda bi, i: (bi, 0, i, 0)),
                  pl.BlockSpec((1, 1, d), lambda bi, i: (bi, 0, 0)), _const_spec(nf.shape)],
        out_specs=tok,
        compiler_params=_params("arbitrary", "arbitrary"),
        name="final",
    )(x3, moe, g2, nf)


def _layer0_mixer_inputs(ev_w_in, dn_a_log, dn_dt_bias):
    w = HEADS * HEAD_DIM
    qkv_w = 3 * w
    wqkv = ev_w_in[:, :qkv_w].astype(BF16)
    wz = ev_w_in[:, qkv_w:qkv_w + w]
    wg = ev_w_in[:, qkv_w + w:qkv_w + w + 4 * HEADS]
    wp = ev_w_in[:, qkv_w + w + 4 * HEADS:]
    wrest = jnp.concatenate([wz, wp, jnp.pad(wg, ((0, 0), (0, LANES - 4 * HEADS)))], axis=1).astype(BF16)
    pad = jnp.zeros((2 * HEADS,), F32)
    tail = jnp.zeros((LANES - 4 * HEADS,), F32)
    gpar = jnp.stack([jnp.concatenate([pad, dn_a_log.reshape(-1), tail]),
                      jnp.concatenate([pad, dn_dt_bias.reshape(-1), tail])])
    return wqkv, wrest, gpar


def _chunk_rows(g):
    b, l, c = g.shape
    return g.reshape(b, l // CHUNK, CHUNK, c).transpose(0, 1, 3, 2)


def kernel(x, c, ctx, c_ctx, ada_w, ada_b, norm_mix, norm_ffn, norm_final, ev_w_in, dn_conv, dn_a_log, dn_dt_bias,
           dn_norm, pool_w, pool_scale, ev_w_out, sc_w_in, sc_conv, sc_w_out, router, w_gate, w_up, w_down):
    b, n, d = x.shape
    depth = ada_w.shape[0]
    assert depth == 2 and b + 1 <= SUBLANES and n % GRID_W == 0

    cc = jnp.concatenate([c, c_ctx[None], jnp.zeros((SUBLANES - b - 1, d), F32)], axis=0)
    mods = _ada_call(cc, ada_w, ada_b)

    def mod(layer, k, rows=slice(0, b)):
        return mods[layer, rows, None, k * d:(k + 1) * d]

    ctx_rows = lambda layer, k: jnp.broadcast_to(mods[layer, b:b + 1, None, k * d:(k + 1) * d], (b, 1, d))
    row = lambda v: v.reshape(1, -1)
    router_t = jnp.swapaxes(router, 1, 2).astype(BF16)
    wgu = jnp.concatenate([w_gate, w_up], axis=-1).astype(BF16)
    wdn = w_down.astype(BF16)

    wqkv, wrest, gpar = _layer0_mixer_inputs(ev_w_in[0], dn_a_log[0], dn_dt_bias[0])
    nmix0 = row(norm_mix[0])
    qc, kc, vc, _, _, gc = _proj0_call(ctx, ctx_rows(0, 0), ctx_rows(0, 1), nmix0, wqkv, wrest, dn_conv[0], gpar, 256)
    ql, kl, vl, zl, pl_in, gl = _proj0_call(x, mod(0, 0), mod(0, 1), nmix0, wqkv, wrest, dn_conv[0], gpar, 512)
    s0 = jnp.zeros((b, 2, HEADS, HEAD_DIM, HEAD_DIM), F32)
    _, _, s_ctx = _delta_scan_call(_delta_prep_call(qc, kc, vc, gc, _chunk_rows(gc)), s0)
    o_f, o_b, _ = _delta_scan_call(_delta_prep_call(ql, kl, vl, gl, _chunk_rows(gl)), s_ctx)
    x1, h2, lg = _out0_call(o_f, o_b, zl, pl_in, x, row(dn_norm[0]), pool_w[0].astype(BF16), row(pool_scale[0]),
                            ev_w_out[0].astype(BF16), mod(0, 2), mod(0, 3), mod(0, 4), row(norm_ffn[0]),
                            router_t[0], 256)
    moe0 = _moe(h2, lg, wgu[0], wdn[0])

    x3, h2, lg = _mix1_call(x1, moe0, mod(0, 5), mod(1, 0), mod(1, 1), row(norm_mix[1]), sc_w_in[0].astype(BF16),
                            sc_conv[0], sc_w_out[0].astype(BF16), mod(1, 2), mod(1, 3), mod(1, 4),
                            row(norm_ffn[1]), router_t[1], 256)
    moe1 = _moe(h2, lg, wgu[1], wdn[1])
    return _final_call(x3, moe1, mod(1, 5), row(norm_final), 512)
```

```python
import functools

import jax
import jax.numpy as jnp
from jax import lax
from jax.experimental import pallas as pl
from jax.experimental.pallas import tpu as pltpu

F32 = jnp.float32
BF16 = jnp.bfloat16
I32 = jnp.int32

EPS = 1e-6
GRID_W = 64
HEADS = 4
HEAD_DIM = 128
CHUNK = 64
POOL_WINDOWS = (2, 4, 8, 16)
POOL_GROUP = 128
LANES = 128
SUBLANES = 8
VMEM_LIMIT = 56 * 1024 * 1024
VALID_BIT = 1 << 30


def _silu(x):
    return x * jax.nn.sigmoid(x)


def _norm_mod(x, g, shift, scale):
    ms = jnp.mean(x * x, axis=-1, keepdims=True)
    return (x * lax.rsqrt(ms + EPS) * g) * (1.0 + scale) + shift


def _dot(a, b):
    return jnp.dot(a, b, preferred_element_type=F32)


def _dot_nt(a, b):
    return lax.dot_general(a, b, (((1,), (1,)), ((), ())), preferred_element_type=F32)


def _dot_tn(a, b):
    return lax.dot_general(a, b, (((0,), (0,)), ((), ())), preferred_element_type=F32)


def _split3(x):
    hi = x.astype(BF16)
    r = x - hi.astype(F32)
    mid = r.astype(BF16)
    lo = (r - mid.astype(F32)).astype(BF16)
    return hi, mid, lo


def _const_spec(shape):
    nd = len(shape)
    return pl.BlockSpec(shape, lambda *_: (0,) * nd, pipeline_mode=pl.Buffered(1))


def _params(*sem):
    return pltpu.CompilerParams(dimension_semantics=sem, vmem_limit_bytes=VMEM_LIMIT)


def _ada_kernel(c_ref, w_ref, b_ref, o_ref):
    s = _silu(c_ref[...])
    o_ref[0] = _dot(s.astype(BF16), w_ref[0].astype(BF16)) + b_ref[0]


def _ada_call(cc, ada_w, ada_b):
    depth, d, n6 = ada_w.shape
    tn = n6 // 4
    return pl.pallas_call(
        _ada_kernel,
        out_shape=jax.ShapeDtypeStruct((depth, SUBLANES, n6), F32),
        grid=(depth, n6 // tn),
        in_specs=[
            pl.BlockSpec((SUBLANES, d), lambda i, j: (0, 0)),
            pl.BlockSpec((1, d, tn), lambda i, j: (i, 0, j)),
            pl.BlockSpec((1, 1, tn), lambda i, j: (i, 0, j)),
        ],
        out_specs=pl.BlockSpec((1, SUBLANES, tn), lambda i, j: (i, 0, j)),
        compiler_params=_params("arbitrary", "arbitrary"),
        name="adaln",
    )(cc, ada_w, ada_b.reshape(depth, 1, n6))


def _proj0_kernel(xp_ref, x_ref, xn_ref, sh_ref, sc_ref, g_ref, wqkv_ref, wrest_ref, conv_ref, gpar_ref,
                  q_ref, k_ref, v_ref, z_ref, p_ref, gt_ref, *, tm):
    i = pl.program_id(1)
    last = pl.num_programs(1) - 1
    x = jnp.concatenate([xp_ref[0], x_ref[0], xn_ref[0]], axis=0)
    h = _norm_mod(x, g_ref[...], sh_ref[0], sc_ref[0])
    row = lax.broadcasted_iota(I32, (tm + 2 * SUBLANES, 1), 0)
    dead = ((row < SUBLANES) & (i == 0)) | ((row >= tm + SUBLANES) & (i == last))
    hb = jnp.where(dead, 0.0, h).astype(BF16)
    proj = _dot(hb, wqkv_ref[...])
    cw = conv_ref[...]
    lo = SUBLANES
    a = (proj[lo - 1:lo - 1 + tm] * cw[0:1] + proj[lo:lo + tm] * cw[1:2] + proj[lo + 1:lo + 1 + tm] * cw[2:3])
    a = _silu(a)
    w = HEADS * HEAD_DIM
    for hd in range(HEADS):
        sl = slice(hd * HEAD_DIM, (hd + 1) * HEAD_DIM)
        qh = a[:, sl]
        kh = a[:, w + hd * HEAD_DIM: w + (hd + 1) * HEAD_DIM]
        qn = qh * lax.rsqrt(jnp.sum(qh * qh, axis=-1, keepdims=True) + EPS) * (HEAD_DIM ** -0.5)
        q_ref[0, :, sl] = qn.astype(BF16)
        k_ref[0, :, sl] = (kh * lax.rsqrt(jnp.sum(kh * kh, axis=-1, keepdims=True) + EPS)).astype(BF16)
    v_ref[0] = a[:, 2 * w:].astype(BF16)
    rest = _dot(hb[lo:lo + tm], wrest_ref[...])
    z_ref[0] = rest[:, :w].astype(BF16)
    p_ref[0] = rest[:, w:2 * w].astype(BF16)
    gates = rest[:, 2 * w:]
    col = lax.broadcasted_iota(I32, (1, LANES), 1)
    xb = gates + gpar_ref[1:2]
    softplus = jnp.maximum(xb, 0.0) + jnp.log1p(jnp.exp(-jnp.abs(xb)))
    log_decay = -jnp.exp(gpar_ref[0:1]) * softplus
    out = jnp.where(col < 2 * HEADS, jax.nn.sigmoid(gates), log_decay)
    gt_ref[0] = out[:, :4 * HEADS]


def _proj0_call(x, shift, scale, gain, wqkv, wrest, conv_w, gpar, tm):
    b, l, d = x.shape
    tm = min(tm, l)
    nt = l // tm
    nb8 = l // SUBLANES
    r = tm // SUBLANES
    w = HEADS * HEAD_DIM
    tok = lambda width: pl.BlockSpec((1, tm, width), lambda bi, i: (bi, i, 0))
    vec = pl.BlockSpec((1, 1, d), lambda bi, i: (bi, 0, 0))
    return pl.pallas_call(
        functools.partial(_proj0_kernel, tm=tm),
        out_shape=[jax.ShapeDtypeStruct((b, l, w), BF16)] * 5 + [jax.ShapeDtypeStruct((b, l, 4 * HEADS), F32)],
        grid=(b, nt),
        in_specs=[
            pl.BlockSpec((1, SUBLANES, d), lambda bi, i: (bi, jnp.maximum(i * r - 1, 0), 0)),
            tok(d),
            pl.BlockSpec((1, SUBLANES, d), lambda bi, i: (bi, jnp.minimum((i + 1) * r, nb8 - 1), 0)),
            vec, vec,
            _const_spec((1, d)),
            _const_spec(wqkv.shape),
            _const_spec(wrest.shape),
            _const_spec(conv_w.shape),
            _const_spec(gpar.shape),
        ],
        out_specs=[tok(w)] * 5 + [tok(4 * HEADS)],
        compiler_params=_params("arbitrary", "arbitrary"),
        name="proj0",
    )(x, x, x, shift, scale, gain, wqkv, wrest, conv_w, gpar)


PREP_CHUNKS = 2
SCAN_CHUNKS = 2


def _delta_prep_kernel(q_ref, k_ref, v_ref, g_ref, gt_ref, u_ref, wk_ref, qd_ref, kdt_ref, qk_ref, cd_ref):
    c = CHUNK
    ri = lax.broadcasted_iota(I32, (c, c), 0)
    ci = lax.broadcasted_iota(I32, (c, c), 1)
    tri_l = jnp.where(ri >= ci, 1.0, 0.0).astype(BF16)
    tri_u = jnp.where(ri <= ci, 1.0, 0.0).astype(BF16)
    eye = jnp.where(ri == ci, 1.0, 0.0)
    incl = (ri >= ci, ri <= ci)
    strict = (ri > ci, ri < ci)
    tot = (c - 1, 0)
    chunks = range(PREP_CHUNKS)
    heads = range(HEADS)
    tok = [slice(j * c, (j + 1) * c) for j in chunks]
    sls = [slice(hd * HEAD_DIM, (hd + 1) * HEAD_DIM) for hd in heads]

    g = [g_ref[0, tok[j], :] for j in chunks]
    g3 = [_split3(g[j]) for j in chunks]
    gt3 = [_split3(gt_ref[0, j]) for j in chunks]
    cols = [(sum(_dot(tri_l, p) for p in g3[j]), sum(_dot(tri_u, p) for p in g3[j])) for j in chunks]
    rows = [(sum(_dot(p, tri_u) for p in gt3[j]), sum(_dot(p, tri_l) for p in gt3[j])) for j in chunks]

    ks = {(j, hd): k_ref[0, tok[j], sls[hd]].astype(F32) for j in chunks for hd in heads}
    qs = {(j, hd): q_ref[0, tok[j], sls[hd]].astype(F32) for j in chunks for hd in heads}
    kbs = {key: ks[key].astype(BF16) for key in ks}
    kk = {key: _dot_nt(kbs[key], kbs[key]) for key in ks}
    qk0 = {key: _dot_nt(qs[key].astype(BF16), kbs[key]) for key in ks}

    combos = [(j, d, hd) for j in chunks for d in range(2) for hd in heads]
    beta, gam, gtot, decay, a = {}, {}, {}, {}, {}
    for j, d, hd in combos:
        key = (j, d, hd)
        bcol = d * HEADS + hd
        dcol = 2 * HEADS + bcol
        beta[key] = g[j][:, bcol:bcol + 1]
        gam[key] = cols[j][d][:, dcol:dcol + 1]
        gtot[key] = gam[key][tot[d]:tot[d] + 1, :]
        diff = gam[key] - rows[j][d][dcol:dcol + 1, :]
        decay[key] = jnp.where(incl[d], jnp.exp(jnp.where(incl[d], diff, 0.0)), 0.0)
        a[key] = jnp.where(strict[d], kk[j, hd] * decay[key], 0.0) * beta[key]
    p = dict(a)
    tinv = {key: eye - a[key] for key in combos}
    for _ in range(c.bit_length() - 2):
        pb = {key: p[key].astype(BF16) for key in combos}
        p = {key: _dot(pb[key], pb[key]) for key in combos}
        tinv = {key: tinv[key] + _dot(tinv[key].astype(BF16), p[key].astype(BF16)) for key in combos}
    eg = {key: jnp.exp(gam[key]) for key in combos}
    uw = {}
    for j, d, hd in combos:
        key = (j, d, hd)
        v = v_ref[0, tok[j], sls[hd]].astype(F32)
        rhs = jnp.concatenate([v * beta[key], ks[j, hd] * (beta[key] * eg[key])], axis=1)
        uw[key] = _dot(tinv[key].astype(BF16), rhs.astype(BF16))
    for j, d, hd in combos:
        key = (j, d, hd)
        u_ref[0, d, tok[j], sls[hd]] = uw[key][:, :HEAD_DIM]
        wk_ref[0, d, tok[j], sls[hd]] = uw[key][:, HEAD_DIM:].astype(BF16)
        qd_ref[0, d, tok[j], sls[hd]] = (qs[j, hd] * eg[key]).astype(BF16)
        kd = ks[j, hd] * jnp.exp(gtot[key] - gam[key])
        kdt_ref[0, d, j, :, hd * c:(hd + 1) * c] = kd.T.astype(BF16)
        qk_ref[0, d, tok[j], hd * c:(hd + 1) * c] = (qk0[j, hd] * decay[key]).astype(BF16)
    for j in chunks:
        for d in range(2):
            cd = [jnp.broadcast_to(jnp.exp(gtot[j, d, hd]), (1, LANES)) for hd in heads]
            cd_ref[0, d, j] = jnp.concatenate(cd + [jnp.zeros((SUBLANES - HEADS, LANES), F32)], axis=0)


def _delta_prep_call(q, k, v, g, gt):
    b, l, w = k.shape
    nc = l // CHUNK
    cps = PREP_CHUNKS
    t = cps * CHUNK
    tok = lambda width: pl.BlockSpec((1, t, width), lambda bi, n: (bi, n, 0))
    dtok = lambda width: pl.BlockSpec((1, 2, t, width), lambda bi, n: (bi, 0, n, 0))
    return pl.pallas_call(
        _delta_prep_kernel,
        out_shape=[
            jax.ShapeDtypeStruct((b, 2, l, w), F32),
            jax.ShapeDtypeStruct((b, 2, l, w), BF16),
            jax.ShapeDtypeStruct((b, 2, l, w), BF16),
            jax.ShapeDtypeStruct((b, 2, nc, HEAD_DIM, HEADS * CHUNK), BF16),
            jax.ShapeDtypeStruct((b, 2, l, HEADS * CHUNK), BF16),
            jax.ShapeDtypeStruct((b, 2, nc, SUBLANES, LANES), F32),
        ],
        grid=(b, nc // cps),
        in_specs=[tok(w), tok(w), tok(w), tok(4 * HEADS),
                  pl.BlockSpec((1, cps, 4 * HEADS, CHUNK), lambda bi, n: (bi, n, 0, 0))],
        out_specs=[dtok(w), dtok(w), dtok(w),
                   pl.BlockSpec((1, 2, cps, HEAD_DIM, HEADS * CHUNK), lambda bi, n: (bi, 0, n, 0, 0)),
                   dtok(HEADS * CHUNK),
                   pl.BlockSpec((1, 2, cps, SUBLANES, LANES), lambda bi, n: (bi, 0, n, 0, 0))],
        compiler_params=_params("arbitrary", "arbitrary"),
        name="delta_prep",
    )(q, k, v, g, gt)


def _delta_scan_kernel(*refs):
    ins = (refs[0:6], refs[6:12])
    s0_ref, of_ref, ob_ref, sfin_ref, s_ref = refs[12:]
    outs = (of_ref, ob_ref)
    n = pl.program_id(1)
    c = CHUNK
    cps = SCAN_CHUNKS

    @pl.when(n == 0)
    def _():
        s_ref[...] = s0_ref[0]

    combos = [(d, hd) for d in range(2) for hd in range(HEADS)]
    sl = lambda hd: slice(hd * HEAD_DIM, (hd + 1) * HEAD_DIM)
    cs = lambda hd: slice(hd * c, (hd + 1) * c)
    s = {(d, hd): s_ref[d, hd] for d, hd in combos}
    for step in range(cps):
        pos = (step, cps - 1 - step)
        tok = [slice(pos[d] * c, (pos[d] + 1) * c) for d in range(2)]
        m1, w, m2 = {}, {}, {}
        for d, hd in combos:
            u_ref, wk_ref, qd_ref, kdt_ref, qk_ref, cd_ref = ins[d]
            lhs = jnp.concatenate([wk_ref[0, 0, tok[d], sl(hd)], qd_ref[0, 0, tok[d], sl(hd)]], axis=0)
            m1[d, hd] = _dot(lhs, s[d, hd].astype(BF16))
        for d, hd in combos:
            w[d, hd] = (ins[d][0][0, 0, tok[d], sl(hd)] - m1[d, hd][:c]).astype(BF16)
        for d, hd in combos:
            u_ref, wk_ref, qd_ref, kdt_ref, qk_ref, cd_ref = ins[d]
            lhs = jnp.concatenate([qk_ref[0, 0, tok[d], cs(hd)], kdt_ref[0, 0, pos[d], :, cs(hd)]], axis=0)
            m2[d, hd] = _dot(lhs, w[d, hd])
        for d, hd in combos:
            outs[d][0, tok[d], sl(hd)] = m1[d, hd][c:] + m2[d, hd][:c]
            s[d, hd] = ins[d][5][0, 0, pos[d], hd:hd + 1, :] * s[d, hd] + m2[d, hd][c:]
    for d, hd in combos:
        s_ref[d, hd] = s[d, hd]

    @pl.when(n == pl.num_programs(1) - 1)
    def _():
        sfin_ref[0] = s_ref[...]


def _delta_scan_call(prep, s0):
    u, wk, qd, kdt, qk, cd = prep
    b, _, l, w = u.shape
    cps = SCAN_CHUNKS
    t = cps * CHUNK
    ns = l // t

    def specs(d, blk):
        tok = lambda width: pl.BlockSpec((1, 1, t, width), lambda bi, n: (bi, d, blk(n), 0))
        return [tok(w), tok(w), tok(w),
                pl.BlockSpec((1, 1, cps, HEAD_DIM, HEADS * CHUNK), lambda bi, n: (bi, d, blk(n), 0, 0)),
                tok(HEADS * CHUNK),
                pl.BlockSpec((1, 1, cps, SUBLANES, LANES), lambda bi, n: (bi, d, blk(n), 0, 0))]

    st = pl.BlockSpec((1, 2, HEADS, HEAD_DIM, HEAD_DIM), lambda bi, n: (bi, 0, 0, 0, 0))
    return pl.pallas_call(
        _delta_scan_kernel,
        out_shape=[jax.ShapeDtypeStruct((b, l, w), F32)] * 2
        + [jax.ShapeDtypeStruct((b, 2, HEADS, HEAD_DIM, HEAD_DIM), F32)],
        grid=(b, ns),
        in_specs=specs(0, lambda n: n) + specs(1, lambda n: ns - 1 - n) + [st],
        out_specs=[pl.BlockSpec((1, t, w), lambda bi, n: (bi, n, 0)),
                   pl.BlockSpec((1, t, w), lambda bi, n: (bi, ns - 1 - n, 0)), st],
        scratch_shapes=[pltpu.VMEM((2, HEADS, HEAD_DIM, HEAD_DIM), F32)],
        compiler_params=_params("arbitrary", "arbitrary"),
        name="delta_scan",
    )(*prep, *prep, s0)


def _ffn_prenorm(x, nffn_ref, sh2_ref, sc2_ref, rt_ref, h2_ref, lg_ref):
    h2 = _norm_mod(x, nffn_ref[...], sh2_ref[0], sc2_ref[0])
    tm, d = x.shape
    pieces = d // LANES
    for j in range(pieces):
        h2_ref[0, pl.ds(j, tm, stride=pieces), :] = h2[:, j * LANES:(j + 1) * LANES]
    lg_ref[0] = _dot_nt(rt_ref[...], h2.astype(BF16))


def _out0_kernel(of_ref, ob_ref, z_ref, p_ref, x_ref, band_ref, cnt_ref, onorm_ref, pw_ref, ps_ref, wout_ref,
                 g1_ref, sh2_ref, sc2_ref, nffn_ref, rt_ref, x1_ref, h2_ref, lg_ref):
    o = of_ref[0] + ob_ref[0]
    z = z_ref[0].astype(F32)
    pin = p_ref[0].astype(F32)
    parts = []
    for hd in range(HEADS):
        sl = slice(hd * HEAD_DIM, (hd + 1) * HEAD_DIM)
        oh = o[:, sl]
        ms = jnp.mean(oh * oh, axis=-1, keepdims=True)
        parts.append(oh * lax.rsqrt(ms + EPS) * onorm_ref[...] * _silu(z[:, sl]))
    for gi in range(len(POOL_WINDOWS)):
        sl = slice(gi * POOL_GROUP, (gi + 1) * POOL_GROUP)
        u = pin[:, sl]
        uh = u.astype(BF16)
        ul = (u - uh.astype(F32)).astype(BF16)
        band = band_ref[gi]
        wsum = _dot(band, uh) + _dot(band, ul)
        grp = wsum / cnt_ref[gi] - u
        parts.append(_dot(grp.astype(BF16), pw_ref[gi]) * ps_ref[:, sl])
    cat = jnp.concatenate(parts, axis=1).astype(BF16)
    x1 = x_ref[0] + g1_ref[0] * _dot(cat, wout_ref[...])
    x1_ref[0] = x1
    _ffn_prenorm(x1, nffn_ref, sh2_ref, sc2_ref, rt_ref, h2_ref, lg_ref)


def _pool_tables(tm):
    t = jnp.arange(tm)
    seg = t // GRID_W
    loc = t % GRID_W
    bands, cnts = [], []
    for w in POOL_WINDOWS:
        lo = jnp.clip(loc - w // 2, 0, GRID_W)
        hi = jnp.clip(loc + w - w // 2, 0, GRID_W)
        inside = (seg[:, None] == seg[None, :]) & (loc[None, :] >= lo[:, None]) & (loc[None, :] < hi[:, None])
        bands.append(inside.astype(BF16))
        cnts.append((hi - lo).astype(F32)[:, None])
    return jnp.stack(bands), jnp.stack(cnts)


def _out0_call(o_f, o_b, z, pin, x, onorm, pool_w, pool_scale, w_out, g1, sh2, sc2, nffn, router_t, tm):
    b, l, d = x.shape
    tm = min(tm, l)
    w = HEADS * HEAD_DIM
    e = router_t.shape[0]
    band, cnt = _pool_tables(tm)
    tok = lambda width: pl.BlockSpec((1, tm, width), lambda bi, i: (bi, i, 0))
    vec = pl.BlockSpec((1, 1, d), lambda bi, i: (bi, 0, 0))
    return pl.pallas_call(
        _out0_kernel,
        out_shape=[jax.ShapeDtypeStruct((b, l, d), F32), jax.ShapeDtypeStruct((b, l * d // LANES, LANES), F32),
                   jax.ShapeDtypeStruct((b, e, l), F32)],
        grid=(b, l // tm),
        in_specs=[tok(w), tok(w), tok(w), tok(w), tok(d),
                  _const_spec(band.shape), _const_spec(cnt.shape), _const_spec(onorm.shape),
                  _const_spec(pool_w.shape), _const_spec(pool_scale.shape), _const_spec(w_out.shape),
                  vec, vec, vec, _const_spec(nffn.shape), _const_spec(router_t.shape)],
        out_specs=[tok(d), pl.BlockSpec((1, tm * d // LANES, LANES), lambda bi, i: (bi, i, 0)),
                   pl.BlockSpec((1, e, tm), lambda bi, i: (bi, 0, i))],
        compiler_params=_params("arbitrary", "arbitrary"),
        name="out0",
    )(o_f, o_b, z, pin, x, band, cnt, onorm, pool_w, pool_scale, w_out, g1, sh2, sc2, nffn, router_t)


def _excl_cumsum_lanes(m):
    rows, n = m.shape
    ri = lax.broadcasted_iota(I32, (LANES, LANES), 0)
    ci = lax.broadcasted_iota(I32, (LANES, LANES), 1)
    upper = jnp.where(ri <= ci, 1.0, 0.0).astype(BF16)
    carry = jnp.zeros((rows, 1), F32)
    outs = []
    for blk in range(n // LANES):
        x = m[:, blk * LANES:(blk + 1) * LANES]
        inc = _dot(x.astype(BF16), upper)
        outs.append(inc - x + carry)
        carry = carry + inc[:, LANES - 1:LANES]
    return jnp.concatenate(outs, axis=1)


def _route_kernel(lg_ref, idx_ref, gate_ref, *, cap):
    x = lg_ref[0]
    e, n = x.shape
    ex = jnp.exp(x - jnp.max(x, axis=0, keepdims=True))
    aff = ex / jnp.sum(ex, axis=0, keepdims=True)

    def count_ge(bits):
        return jnp.sum(jnp.where(aff >= pltpu.bitcast(bits, F32), 1.0, 0.0), axis=1, keepdims=True)

    def bisect(_, lohi):
        lo, hi = lohi
        mid = lo + ((hi - lo + 1) >> 1)
        ok = count_ge(mid) >= cap
        return jnp.where(ok, mid, lo), jnp.where(ok, hi, mid - 1)

    lo0 = jnp.zeros((e, 1), I32)
    hi0 = jnp.full((e, 1), 0x7F800000, I32)
    thr, _ = lax.fori_loop(0, 31, bisect, (lo0, hi0))
    above = jnp.where(aff >= pltpu.bitcast(thr + 1, F32), 1.0, 0.0)
    tied = jnp.where(aff >= pltpu.bitcast(thr, F32), 1.0, 0.0) - above
    need = cap - jnp.sum(above, axis=1, keepdims=True)
    sel = above + tied * jnp.where(_excl_cumsum_lanes(tied) < need, 1.0, 0.0)
    pos = _excl_cumsum_lanes(sel)
    tok = lax.broadcasted_iota(I32, (e, n), 1)
    key = jnp.where(sel > 0.0, (tok - pos.astype(I32)) | VALID_BIT, 0)
    shift = 1
    while shift < n:
        mk = pltpu.roll(key, n - shift, axis=1)
        mt = pltpu.roll(tok, n - shift, axis=1)
        ma = pltpu.roll(aff, n - shift, axis=1)
        take = (mk & shift) != 0
        key = jnp.where(take, mk, jnp.where((key & shift) == 0, key, 0))
        tok = jnp.where(take, mt, tok)
        aff = jnp.where(take, ma, aff)
        shift *= 2
    idx_ref[0] = tok[:, :cap]
    gate_ref[0] = aff[:, :cap]


def _route_call(logits_t, cap):
    b, e, n = logits_t.shape
    return pl.pallas_call(
        functools.partial(_route_kernel, cap=cap),
        out_shape=[jax.ShapeDtypeStruct((b, e, cap), I32), jax.ShapeDtypeStruct((b, e, cap), F32)],
        grid=(b,),
        in_specs=[pl.BlockSpec((1, e, n), lambda bi: (bi, 0, 0))],
        out_specs=[pl.BlockSpec((1, e, cap), lambda bi: (bi, 0, 0))] * 2,
        compiler_params=_params("arbitrary"),
        name="route",
    )(logits_t)


GATHER_UNROLL = 8


def _moe_up_kernel(idx_ref, src_ref, wgu_ref, hid_ref, rows_ref, *, cap, f):
    def gather(cidx, carry):
        for u in range(GATHER_UNROLL):
            s = cidx * GATHER_UNROLL + u
            t = idx_ref[0, 0, s]
            rows_ref[pl.ds(pl.multiple_of(s * SUBLANES, SUBLANES), SUBLANES), :] = (
                src_ref[0, pl.ds(pl.multiple_of(t * SUBLANES, SUBLANES), SUBLANES), :])
        return carry

    lax.fori_loop(0, cap // GATHER_UNROLL, gather, 0)
    x = jnp.concatenate([rows_ref[pl.ds(j, cap, stride=SUBLANES), :].astype(BF16) for j in range(SUBLANES)], axis=1)
    gu = _dot(x, wgu_ref[0])
    hid_ref[0, 0] = (_silu(gu[:, :f]) * gu[:, f:]).astype(BF16)


def _moe_up_call(idx, src, wgu, cap):
    b, nrows, _ = src.shape
    e, d = wgu.shape[:2]
    f = wgu.shape[2] // 2
    return pl.pallas_call(
        functools.partial(_moe_up_kernel, cap=cap, f=f),
        out_shape=jax.ShapeDtypeStruct((b, e, cap, f), BF16),
        grid=(b, e),
        in_specs=[
            pl.BlockSpec((1, 1, cap), lambda bi, ei: (bi * e + ei, 0, 0), memory_space=pltpu.SMEM),
            pl.BlockSpec((1, nrows, LANES), lambda bi, ei: (bi, 0, 0), pipeline_mode=pl.Buffered(1)),
            pl.BlockSpec((1, d, 2 * f), lambda bi, ei: (ei, 0, 0)),
        ],
        out_specs=pl.BlockSpec((1, 1, cap, f), lambda bi, ei: (bi, ei, 0, 0)),
        scratch_shapes=[pltpu.VMEM((cap * d // LANES, LANES), F32)],
        compiler_params=_params("arbitrary", "arbitrary"),
        name="moe_up",
    )(idx.reshape(b * e, 1, cap), src, wgu)


SCATTER_UNROLL = 8
DOWN_SPLIT = 1


def _moe_down_kernel(idx_ref, gate_ref, hid_ref, wd_ref, acc_ref, y_ref, *, cap, rows):
    @pl.when(pl.program_id(1) == 0)
    def _():
        acc_ref[...] = jnp.zeros_like(acc_ref)

    y = _dot(hid_ref[0, 0], wd_ref[0])
    for j in range(rows):
        y_ref[pl.ds(j, cap, stride=rows), :] = y[:, j * LANES:(j + 1) * LANES]

    def scatter(cidx, carry):
        toks, vals = [], []
        for u in range(SCATTER_UNROLL):
            s = cidx * SCATTER_UNROLL + u
            t = idx_ref[0, 0, s]
            piece = y_ref[pl.ds(pl.multiple_of(s * rows, rows), rows), :]
            dst = pl.ds(pl.multiple_of(t * rows, rows), rows)
            toks.append(dst)
            vals.append(acc_ref[0, dst, :] + gate_ref[0, 0, s] * piece)
        for dst, val in zip(toks, vals):
            acc_ref[0, dst, :] = val
        return carry

    lax.fori_loop(0, cap // SCATTER_UNROLL, scatter, 0)


def _moe_down_call(idx, gate, hid, wd, n):
    b, e, cap, f = hid.shape
    d = wd.shape[2]
    dw = d // DOWN_SPLIT
    rows = dw // LANES
    out = pl.pallas_call(
        functools.partial(_moe_down_kernel, cap=cap, rows=rows),
        out_shape=jax.ShapeDtypeStruct((b * DOWN_SPLIT, n * rows, LANES), F32),
        grid=(b * DOWN_SPLIT, e),
        in_specs=[
            pl.BlockSpec((1, 1, cap), lambda bh, ei: ((bh // DOWN_SPLIT) * e + ei, 0, 0), memory_space=pltpu.SMEM),
            pl.BlockSpec((1, 1, cap), lambda bh, ei: ((bh // DOWN_SPLIT) * e + ei, 0, 0), memory_space=pltpu.SMEM),
            pl.BlockSpec((1, 1, cap, f), lambda bh, ei: (bh // DOWN_SPLIT, ei, 0, 0)),
            pl.BlockSpec((1, f, dw), lambda bh, ei: (ei, 0, bh % DOWN_SPLIT)),
        ],
        out_specs=pl.BlockSpec((1, n * rows, LANES), lambda bh, ei: (bh, 0, 0), pipeline_mode=pl.Buffered(1)),
        scratch_shapes=[pltpu.VMEM((cap * rows, LANES), F32)],
        compiler_params=_params("arbitrary", "arbitrary"),
        name="moe_down",
    )(idx.reshape(b * e, 1, cap), gate.reshape(b * e, 1, cap), hid, wd)
    return out.reshape(b, DOWN_SPLIT, n * rows, LANES)


def _moe(h2, logits_t, wgu, wd):
    b, e, n = logits_t.shape
    cap = 2 * n // e
    idx, gate = _route_call(logits_t, cap)
    hid = _moe_up_call(idx, h2, wgu, cap)
    return _moe_down_call(idx, gate, hid, wd, n)


def _join(m_ref, tm):
    rows = m_ref.shape[2] // tm
    return jnp.concatenate([m_ref[0, h, pl.ds(j, tm, stride=rows), :]
                            for h in range(DOWN_SPLIT) for j in range(rows)], axis=1)


def _mix1_kernel(xp_ref, x_ref, xn_ref, mp_ref, m_ref, mn_ref, g2p_ref, sh1_ref, sc1_ref, nmix_ref, win_ref,
                 conv_ref, wout_ref, g1_ref, sh2_ref, sc2_ref, nffn_ref, rt_ref, x3_ref, h2_ref, lg_ref, *, tm):
    i = pl.program_id(1)
    last = pl.num_programs(1) - 1
    d = x_ref.shape[2]
    xs = jnp.concatenate([xp_ref[0], x_ref[0], xn_ref[0]], axis=0)
    ms = jnp.concatenate([_join(mp_ref, SUBLANES), _join(m_ref, tm), _join(mn_ref, SUBLANES)], axis=0)
    x2 = xs + g2p_ref[0] * ms
    h = _norm_mod(x2, nmix_ref[...], sh1_ref[0], sc1_ref[0])
    row = lax.broadcasted_iota(I32, (tm + 2 * SUBLANES, 1), 0)
    dead = ((row < SUBLANES) & (i == 0)) | ((row >= tm + SUBLANES) & (i == last))
    hb = jnp.where(dead, 0.0, h).astype(BF16)
    proj = _dot(hb, win_ref[...])
    lo = SUBLANES
    u = proj[:, d:2 * d] * proj[:, 2 * d:]
    cw = conv_ref[...]
    cv = u[lo - 1:lo - 1 + tm] * cw[0:1] + u[lo:lo + tm] * cw[1:2] + u[lo + 1:lo + 1 + tm] * cw[2:3]
    y = _dot((proj[lo:lo + tm, :d] * cv).astype(BF16), wout_ref[...])
    x3 = x2[lo:lo + tm] + g1_ref[0] * y
    x3_ref[0] = x3
    _ffn_prenorm(x3, nffn_ref, sh2_ref, sc2_ref, rt_ref, h2_ref, lg_ref)


def _mix1_call(x1, moe, g2p, sh1, sc1, nmix, w_in, conv_w, w_out, g1, sh2, sc2, nffn, router_t, tm):
    b, l, d = x1.shape
    tm = min(tm, l)
    nb8 = l // SUBLANES
    r = tm // SUBLANES
    e = router_t.shape[0]
    mr = d // DOWN_SPLIT // LANES
    prev = lambda bi, i: (bi, jnp.maximum(i * r - 1, 0), 0)
    nxt = lambda bi, i: (bi, jnp.minimum((i + 1) * r, nb8 - 1), 0)
    tok = pl.BlockSpec((1, tm, d), lambda bi, i: (bi, i, 0))
    vec = pl.BlockSpec((1, 1, d), lambda bi, i: (bi, 0, 0))
    return pl.pallas_call(
        functools.partial(_mix1_kernel, tm=tm),
        out_shape=[jax.ShapeDtypeStruct((b, l, d), F32), jax.ShapeDtypeStruct((b, l * d // LANES, LANES), F32),
                   jax.ShapeDtypeStruct((b, e, l), F32)],
        grid=(b, l // tm),
        in_specs=[
            pl.BlockSpec((1, SUBLANES, d), prev), tok, pl.BlockSpec((1, SUBLANES, d), nxt),
            pl.BlockSpec((1, DOWN_SPLIT, SUBLANES * mr, LANES), lambda bi, i: (bi, 0, jnp.maximum(i * r - 1, 0), 0)),
            pl.BlockSpec((1, DOWN_SPLIT, tm * mr, LANES), lambda bi, i: (bi, 0, i, 0)),
            pl.BlockSpec((1, DOWN_SPLIT, SUBLANES * mr, LANES),
                         lambda bi, i: (bi, 0, jnp.minimum((i + 1) * r, nb8 - 1), 0)),
            vec, vec, vec, _const_spec(nmix.shape), _const_spec(w_in.shape), _const_spec(conv_w.shape),
            _const_spec(w_out.shape), vec, vec, vec, _const_spec(nffn.shape), _const_spec(router_t.shape),
        ],
        out_specs=[tok, pl.BlockSpec((1, tm * d // LANES, LANES), lambda bi, i: (bi, i, 0)),
                   pl.BlockSpec((1, e, tm), lambda bi, i: (bi, 0, i))],
        compiler_params=_params("arbitrary", "arbitrary"),
        name="mix1",
    )(x1, x1, x1, moe, moe, moe, g2p, sh1, sc1, nmix, w_in, conv_w, w_out, g1, sh2, sc2, nffn, router_t)


def _final_kernel(x_ref, m_ref, g2_ref, nf_ref, o_ref):
    x = x_ref[0] + g2_ref[0] * _join(m_ref, x_ref.shape[1])
    ms = jnp.mean(x * x, axis=-1, keepdims=True)
    o_ref[0] = x * lax.rsqrt(ms + EPS) * nf_ref[...]


def _final_call(x3, moe, g2, nf, tm):
    b, l, d = x3.shape
    tm = min(tm, l)
    mr = d // DOWN_SPLIT // LANES
    tok = pl.BlockSpec((1, tm, d), lambda bi, i: (bi, i, 0))
    return pl.pallas_call(
        _final_kernel,
        out_shape=jax.ShapeDtypeStruct((b, l, d), F32),
        grid=(b, l // tm),
        in_specs=[tok, pl.BlockSpec((1, DOWN_SPLIT, tm * mr, LANES), lambda bi, i: (bi, 0, i, 0)),
                  pl.BlockSpec((1, 1, d), lambda bi, i: (bi, 0, 0)), _const_spec(nf.shape)],
        out_specs=tok,
        compiler_params=_params("arbitrary", "arbitrary"),
        name="final",
    )(x3, moe, g2, nf)


def _layer0_mixer_inputs(ev_w_in, dn_a_log, dn_dt_bias):
    w = HEADS * HEAD_DIM
    qkv_w = 3 * w
    wqkv = ev_w_in[:, :qkv_w].astype(BF16)
    wz = ev_w_in[:, qkv_w:qkv_w + w]
    wg = ev_w_in[:, qkv_w + w:qkv_w + w + 4 * HEADS]
    wp = ev_w_in[:, qkv_w + w + 4 * HEADS:]
    wrest = jnp.concatenate([wz, wp, jnp.pad(wg, ((0, 0), (0, LANES - 4 * HEADS)))], axis=1).astype(BF16)
    pad = jnp.zeros((2 * HEADS,), F32)
    tail = jnp.zeros((LANES - 4 * HEADS,), F32)
    gpar = jnp.stack([jnp.concatenate([pad, dn_a_log.reshape(-1), tail]),
                      jnp.concatenate([pad, dn_dt_bias.reshape(-1), tail])])
    return wqkv, wrest, gpar


def _chunk_rows(g):
    b, l, c = g.shape
    return g.reshape(b, l // CHUNK, CHUNK, c).transpose(0, 1, 3, 2)


def kernel(x, c, ctx, c_ctx, ada_w, ada_b, norm_mix, norm_ffn, norm_final, ev_w_in, dn_conv, dn_a_log, dn_dt_bias,
           dn_norm, pool_w, pool_scale, ev_w_out, sc_w_in, sc_conv, sc_w_out, router, w_gate, w_up, w_down):
    b, n, d = x.shape
    depth = ada_w.shape[0]
    span = max(PREP_CHUNKS, SCAN_CHUNKS) * CHUNK
    assert depth == 2 and b + 1 <= SUBLANES and n % GRID_W == 0 and n % span == 0 and ctx.shape[1] % span == 0

    cc = jnp.concatenate([c, c_ctx[None], jnp.zeros((SUBLANES - b - 1, d), F32)], axis=0)
    mods = _ada_call(cc, ada_w, ada_b)

    def mod(layer, k, rows=slice(0, b)):
        return mods[layer, rows, None, k * d:(k + 1) * d]

    ctx_rows = lambda layer, k: jnp.broadcast_to(mods[layer, b:b + 1, None, k * d:(k + 1) * d], (b, 1, d))
    row = lambda v: v.reshape(1, -1)
    router_t = jnp.swapaxes(router, 1, 2).astype(BF16)
    wgu = jnp.concatenate([w_gate, w_up], axis=-1).astype(BF16)
    wdn = w_down.astype(BF16)

    wqkv, wrest, gpar = _layer0_mixer_inputs(ev_w_in[0], dn_a_log[0], dn_dt_bias[0])
    nmix0 = row(norm_mix[0])
    qc, kc, vc, _, _, gc = _proj0_call(ctx, ctx_rows(0, 0), ctx_rows(0, 1), nmix0, wqkv, wrest, dn_conv[0], gpar, 256)
    ql, kl, vl, zl, pl_in, gl = _proj0_call(x, mod(0, 0), mod(0, 1), nmix0, wqkv, wrest, dn_conv[0], gpar, 512)
    s0 = jnp.zeros((b, 2, HEADS, HEAD_DIM, HEAD_DIM), F32)
    _, _, s_ctx = _delta_scan_call(_delta_prep_call(qc, kc, vc, gc, _chunk_rows(gc)), s0)
    o_f, o_b, _ = _delta_scan_call(_delta_prep_call(ql, kl, vl, gl, _chunk_rows(gl)), s_ctx)
    x1, h2, lg = _out0_call(o_f, o_b, zl, pl_in, x, row(dn_norm[0]), pool_w[0].astype(BF16), row(pool_scale[0]),
                            ev_w_out[0].astype(BF16), mod(0, 2), mod(0, 3), mod(0, 4), row(norm_ffn[0]),
                            router_t[0], 256)
    moe0 = _moe(h2, lg, wgu[0], wdn[0])

    x3, h2, lg = _mix1_call(x1, moe0, mod(0, 5), mod(1, 0), mod(1, 1), row(norm_mix[1]), sc_w_in[0].astype(BF16),
                            sc_conv[0], sc_w_out[0].astype(BF16), mod(1, 2), mod(1, 3), mod(1, 4),
                            row(norm_ffn[1]), router_t[1], 256)
    moe1 = _moe(h2, lg, wgu[1], wdn[1])
    return _final_call(x3, moe1, mod(1, 5), row(norm_final), 512)
```

```python
import functools

import jax
import jax.numpy as jnp
from jax import lax
from jax.experimental import pallas as pl
from jax.experimental.pallas import tpu as pltpu

F32 = jnp.float32
BF16 = jnp.bfloat16
I32 = jnp.int32

EPS = 1e-6
GRID_W = 64
HEADS = 4
HEAD_DIM = 128
CHUNK = 64
POOL_WINDOWS = (2, 4, 8, 16)
POOL_GROUP = 128
LANES = 128
SUBLANES = 8
VMEM_LIMIT = 56 * 1024 * 1024
VALID_BIT = 1 << 30


def _silu(x):
    return x * jax.nn.sigmoid(x)


def _norm_mod(x, g, shift, scale):
    ms = jnp.mean(x * x, axis=-1, keepdims=True)
    return (x * lax.rsqrt(ms + EPS) * g) * (1.0 + scale) + shift


def _dot(a, b):
    return jnp.dot(a, b, preferred_element_type=F32)


def _dot_nt(a, b):
    return lax.dot_general(a, b, (((1,), (1,)), ((), ())), preferred_element_type=F32)


def _dot_tn(a, b):
    return lax.dot_general(a, b, (((0,), (0,)), ((), ())), preferred_element_type=F32)


def _split3(x):
    hi = x.astype(BF16)
    r = x - hi.astype(F32)
    mid = r.astype(BF16)
    lo = (r - mid.astype(F32)).astype(BF16)
    return hi, mid, lo


def _const_spec(shape):
    nd = len(shape)
    return pl.BlockSpec(shape, lambda *_: (0,) * nd, pipeline_mode=pl.Buffered(1))


def _params(*sem):
    return pltpu.CompilerParams(dimension_semantics=sem, vmem_limit_bytes=VMEM_LIMIT)


def _ada_kernel(c_ref, w_ref, b_ref, o_ref):
    s = _silu(c_ref[...])
    o_ref[0] = _dot(s.astype(BF16), w_ref[0].astype(BF16)) + b_ref[0]


def _ada_call(cc, ada_w, ada_b):
    depth, d, n6 = ada_w.shape
    tn = n6 // 4
    return pl.pallas_call(
        _ada_kernel,
        out_shape=jax.ShapeDtypeStruct((depth, SUBLANES, n6), F32),
        grid=(depth, n6 // tn),
        in_specs=[
            pl.BlockSpec((SUBLANES, d), lambda i, j: (0, 0)),
            pl.BlockSpec((1, d, tn), lambda i, j: (i, 0, j)),
            pl.BlockSpec((1, 1, tn), lambda i, j: (i, 0, j)),
        ],
        out_specs=pl.BlockSpec((1, SUBLANES, tn), lambda i, j: (i, 0, j)),
        compiler_params=_params("arbitrary", "arbitrary"),
        name="adaln",
    )(cc, ada_w, ada_b.reshape(depth, 1, n6))


def _proj0_kernel(xp_ref, x_ref, xn_ref, sh_ref, sc_ref, g_ref, wqkv_ref, wrest_ref, conv_ref, gpar_ref,
                  q_ref, k_ref, v_ref, z_ref, p_ref, gt_ref, *, tm):
    i = pl.program_id(1)
    last = pl.num_programs(1) - 1
    x = jnp.concatenate([xp_ref[0], x_ref[0], xn_ref[0]], axis=0)
    h = _norm_mod(x, g_ref[...], sh_ref[0], sc_ref[0])
    row = lax.broadcasted_iota(I32, (tm + 2 * SUBLANES, 1), 0)
    dead = ((row < SUBLANES) & (i == 0)) | ((row >= tm + SUBLANES) & (i == last))
    hb = jnp.where(dead, 0.0, h).astype(BF16)
    proj = _dot(hb, wqkv_ref[...])
    cw = conv_ref[...]
    lo = SUBLANES
    a = (proj[lo - 1:lo - 1 + tm] * cw[0:1] + proj[lo:lo + tm] * cw[1:2] + proj[lo + 1:lo + 1 + tm] * cw[2:3])
    a = _silu(a)
    w = HEADS * HEAD_DIM
    for hd in range(HEADS):
        sl = slice(hd * HEAD_DIM, (hd + 1) * HEAD_DIM)
        qh = a[:, sl]
        kh = a[:, w + hd * HEAD_DIM: w + (hd + 1) * HEAD_DIM]
        qn = qh * lax.rsqrt(jnp.sum(qh * qh, axis=-1, keepdims=True) + EPS) * (HEAD_DIM ** -0.5)
        q_ref[0, :, sl] = qn.astype(BF16)
        k_ref[0, :, sl] = (kh * lax.rsqrt(jnp.sum(kh * kh, axis=-1, keepdims=True) + EPS)).astype(BF16)
    v_ref[0] = a[:, 2 * w:].astype(BF16)
    rest = _dot(hb[lo:lo + tm], wrest_ref[...])
    z_ref[0] = rest[:, :w].astype(BF16)
    p_ref[0] = rest[:, w:2 * w].astype(BF16)
    gates = rest[:, 2 * w:]
    col = lax.broadcasted_iota(I32, (1, LANES), 1)
    xb = gates + gpar_ref[1:2]
    softplus = jnp.maximum(xb, 0.0) + jnp.log1p(jnp.exp(-jnp.abs(xb)))
    log_decay = -jnp.exp(gpar_ref[0:1]) * softplus
    out = jnp.where(col < 2 * HEADS, jax.nn.sigmoid(gates), log_decay)
    gt_ref[0] = out[:, :4 * HEADS]


def _proj0_call(x, shift, scale, gain, wqkv, wrest, conv_w, gpar, tm):
    b, l, d = x.shape
    tm = min(tm, l)
    nt = l // tm
    nb8 = l // SUBLANES
    r = tm // SUBLANES
    w = HEADS * HEAD_DIM
    tok = lambda width: pl.BlockSpec((1, tm, width), lambda bi, i: (bi, i, 0))
    vec = pl.BlockSpec((1, 1, d), lambda bi, i: (bi, 0, 0))
    return pl.pallas_call(
        functools.partial(_proj0_kernel, tm=tm),
        out_shape=[jax.ShapeDtypeStruct((b, l, w), BF16)] * 5 + [jax.ShapeDtypeStruct((b, l, 4 * HEADS), F32)],
        grid=(b, nt),
        in_specs=[
            pl.BlockSpec((1, SUBLANES, d), lambda bi, i: (bi, jnp.maximum(i * r - 1, 0), 0)),
            tok(d),
            pl.BlockSpec((1, SUBLANES, d), lambda bi, i: (bi, jnp.minimum((i + 1) * r, nb8 - 1), 0)),
            vec, vec,
            _const_spec((1, d)),
            _const_spec(wqkv.shape),
            _const_spec(wrest.shape),
            _const_spec(conv_w.shape),
            _const_spec(gpar.shape),
        ],
        out_specs=[tok(w)] * 5 + [tok(4 * HEADS)],
        compiler_params=_params("arbitrary", "arbitrary"),
        name="proj0",
    )(x, x, x, shift, scale, gain, wqkv, wrest, conv_w, gpar)


PREP_CHUNKS = 2
SCAN_CHUNKS = 2


def _delta_prep_kernel(q_ref, k_ref, v_ref, g_ref, gt_ref, u_ref, wk_ref, qd_ref, kdt_ref, qk_ref, cd_ref):
    c = CHUNK
    ri = lax.broadcasted_iota(I32, (c, c), 0)
    ci = lax.broadcasted_iota(I32, (c, c), 1)
    tri_l = jnp.where(ri >= ci, 1.0, 0.0).astype(BF16)
    tri_u = jnp.where(ri <= ci, 1.0, 0.0).astype(BF16)
    eye = jnp.where(ri == ci, 1.0, 0.0)
    incl = (ri >= ci, ri <= ci)
    strict = (ri > ci, ri < ci)
    tot = (c - 1, 0)
    chunks = range(PREP_CHUNKS)
    heads = range(HEADS)
    tok = [slice(j * c, (j + 1) * c) for j in chunks]
    sls = [slice(hd * HEAD_DIM, (hd + 1) * HEAD_DIM) for hd in heads]

    g = [g_ref[0, tok[j], :] for j in chunks]
    g3 = [_split3(g[j]) for j in chunks]
    gt3 = [_split3(gt_ref[0, j]) for j in chunks]
    cols = [(sum(_dot(tri_l, p) for p in g3[j]), sum(_dot(tri_u, p) for p in g3[j])) for j in chunks]
    rows = [(sum(_dot(p, tri_u) for p in gt3[j]), sum(_dot(p, tri_l) for p in gt3[j])) for j in chunks]

    ks = {(j, hd): k_ref[0, tok[j], sls[hd]].astype(F32) for j in chunks for hd in heads}
    qs = {(j, hd): q_ref[0, tok[j], sls[hd]].astype(F32) for j in chunks for hd in heads}
    kbs = {key: ks[key].astype(BF16) for key in ks}
    kk = {key: _dot_nt(kbs[key], kbs[key]) for key in ks}
    qk0 = {key: _dot_nt(qs[key].astype(BF16), kbs[key]) for key in ks}

    combos = [(j, d, hd) for j in chunks for d in range(2) for hd in heads]
    beta, gam, gtot, decay, a = {}, {}, {}, {}, {}
    for j, d, hd in combos:
        key = (j, d, hd)
        bcol = d * HEADS + hd
        dcol = 2 * HEADS + bcol
        beta[key] = g[j][:, bcol:bcol + 1]
        gam[key] = cols[j][d][:, dcol:dcol + 1]
        gtot[key] = gam[key][tot[d]:tot[d] + 1, :]
        diff = gam[key] - rows[j][d][dcol:dcol + 1, :]
        decay[key] = jnp.where(incl[d], jnp.exp(jnp.where(incl[d], diff, 0.0)), 0.0)
        a[key] = jnp.where(strict[d], kk[j, hd] * decay[key], 0.0) * beta[key]
    p = dict(a)
    tinv = {key: eye - a[key] for key in combos}
    for _ in range(c.bit_length() - 2):
        pb = {key: p[key].astype(BF16) for key in combos}
        p = {key: _dot(pb[key], pb[key]) for key in combos}
        tinv = {key: tinv[key] + _dot(tinv[key].astype(BF16), p[key].astype(BF16)) for key in combos}
    eg = {key: jnp.exp(gam[key]) for key in combos}
    uw = {}
    for j, d, hd in combos:
        key = (j, d, hd)
        v = v_ref[0, tok[j], sls[hd]].astype(F32)
        rhs = jnp.concatenate([v * beta[key], ks[j, hd] * (beta[key] * eg[key])], axis=1)
        uw[key] = _dot(tinv[key].astype(BF16), rhs.astype(BF16))
    for j, d, hd in combos:
        key = (j, d, hd)
        u_ref[0, d, tok[j], sls[hd]] = uw[key][:, :HEAD_DIM]
        wk_ref[0, d, tok[j], sls[hd]] = uw[key][:, HEAD_DIM:].astype(BF16)
        qd_ref[0, d, tok[j], sls[hd]] = (qs[j, hd] * eg[key]).astype(BF16)
        kd = ks[j, hd] * jnp.exp(gtot[key] - gam[key])
        kdt_ref[0, d, j, :, hd * c:(hd + 1) * c] = kd.T.astype(BF16)
        qk_ref[0, d, tok[j], hd * c:(hd + 1) * c] = (qk0[j, hd] * decay[key]).astype(BF16)
    for j in chunks:
        for d in range(2):
            cd = [jnp.broadcast_to(jnp.exp(gtot[j, d, hd]), (1, LANES)) for hd in heads]
            cd_ref[0, d, j] = jnp.concatenate(cd + [jnp.zeros((SUBLANES - HEADS, LANES), F32)], axis=0)


def _delta_prep_call(q, k, v, g, gt):
    b, l, w = k.shape
    nc = l // CHUNK
    cps = PREP_CHUNKS
    t = cps * CHUNK
    tok = lambda width: pl.BlockSpec((1, t, width), lambda bi, n: (bi, n, 0))
    dtok = lambda width: pl.BlockSpec((1, 2, t, width), lambda bi, n: (bi, 0, n, 0))
    return pl.pallas_call(
        _delta_prep_kernel,
        out_shape=[
            jax.ShapeDtypeStruct((b, 2, l, w), F32),
            jax.ShapeDtypeStruct((b, 2, l, w), BF16),
            jax.ShapeDtypeStruct((b, 2, l, w), BF16),
            jax.ShapeDtypeStruct((b, 2, nc, HEAD_DIM, HEADS * CHUNK), BF16),
            jax.ShapeDtypeStruct((b, 2, l, HEADS * CHUNK), BF16),
            jax.ShapeDtypeStruct((b, 2, nc, SUBLANES, LANES), F32),
        ],
        grid=(b, nc // cps),
        in_specs=[tok(w), tok(w), tok(w), tok(4 * HEADS),
                  pl.BlockSpec((1, cps, 4 * HEADS, CHUNK), lambda bi, n: (bi, n, 0, 0))],
        out_specs=[dtok(w), dtok(w), dtok(w),
                   pl.BlockSpec((1, 2, cps, HEAD_DIM, HEADS * CHUNK), lambda bi, n: (bi, 0, n, 0, 0)),
                   dtok(HEADS * CHUNK),
                   pl.BlockSpec((1, 2, cps, SUBLANES, LANES), lambda bi, n: (bi, 0, n, 0, 0))],
        compiler_params=_params("arbitrary", "arbitrary"),
        name="delta_prep",
    )(q, k, v, g, gt)


def _delta_scan_kernel(*refs):
    ins = (refs[0:6], refs[6:12])
    s0_ref, of_ref, ob_ref, sfin_ref, s_ref = refs[12:]
    outs = (of_ref, ob_ref)
    n = pl.program_id(1)
    c = CHUNK
    cps = SCAN_CHUNKS

    @pl.when(n == 0)
    def _():
        s_ref[...] = s0_ref[0]

    combos = [(d, hd) for d in range(2) for hd in range(HEADS)]
    sl = lambda hd: slice(hd * HEAD_DIM, (hd + 1) * HEAD_DIM)
    cs = lambda hd: slice(hd * c, (hd + 1) * c)
    s = {(d, hd): s_ref[d, hd] for d, hd in combos}
    for step in range(cps):
        pos = (step, cps - 1 - step)
        tok = [slice(pos[d] * c, (pos[d] + 1) * c) for d in range(2)]
        m1, w, m2 = {}, {}, {}
        for d, hd in combos:
            u_ref, wk_ref, qd_ref, kdt_ref, qk_ref, cd_ref = ins[d]
            lhs = jnp.concatenate([wk_ref[0, 0, tok[d], sl(hd)], qd_ref[0, 0, tok[d], sl(hd)]], axis=0)
            m1[d, hd] = _dot(lhs, s[d, hd].astype(BF16))
        for d, hd in combos:
            w[d, hd] = (ins[d][0][0, 0, tok[d], sl(hd)] - m1[d, hd][:c]).astype(BF16)
        for d, hd in combos:
            u_ref, wk_ref, qd_ref, kdt_ref, qk_ref, cd_ref = ins[d]
            lhs = jnp.concatenate([qk_ref[0, 0, tok[d], cs(hd)], kdt_ref[0, 0, pos[d], :, cs(hd)]], axis=0)
            m2[d, hd] = _dot(lhs, w[d, hd])
        for d, hd in combos:
            outs[d][0, tok[d], sl(hd)] = m1[d, hd][c:] + m2[d, hd][:c]
            s[d, hd] = ins[d][5][0, 0, pos[d], hd:hd + 1, :] * s[d, hd] + m2[d, hd][c:]
    for d, hd in combos:
        s_ref[d, hd] = s[d, hd]

    @pl.when(n == pl.num_programs(1) - 1)
    def _():
        sfin_ref[0] = s_ref[...]


def _delta_scan_call(prep, s0):
    u, wk, qd, kdt, qk, cd = prep
    b, _, l, w = u.shape
    cps = SCAN_CHUNKS
    t = cps * CHUNK
    ns = l // t

    def specs(d, blk):
        tok = lambda width: pl.BlockSpec((1, 1, t, width), lambda bi, n: (bi, d, blk(n), 0))
        return [tok(w), tok(w), tok(w),
                pl.BlockSpec((1, 1, cps, HEAD_DIM, HEADS * CHUNK), lambda bi, n: (bi, d, blk(n), 0, 0)),
                tok(HEADS * CHUNK),
                pl.BlockSpec((1, 1, cps, SUBLANES, LANES), lambda bi, n: (bi, d, blk(n), 0, 0))]

    st = pl.BlockSpec((1, 2, HEADS, HEAD_DIM, HEAD_DIM), lambda bi, n: (bi, 0, 0, 0, 0))
    return pl.pallas_call(
        _delta_scan_kernel,
        out_shape=[jax.ShapeDtypeStruct((b, l, w), F32)] * 2
        + [jax.ShapeDtypeStruct((b, 2, HEADS, HEAD_DIM, HEAD_DIM), F32)],
        grid=(b, ns),
        in_specs=specs(0, lambda n: n) + specs(1, lambda n: ns - 1 - n) + [st],
        out_specs=[pl.BlockSpec((1, t, w), lambda bi, n: (bi, n, 0)),
                   pl.BlockSpec((1, t, w), lambda bi, n: (bi, ns - 1 - n, 0)), st],
        scratch_shapes=[pltpu.VMEM((2, HEADS, HEAD_DIM, HEAD_DIM), F32)],
        compiler_params=_params("arbitrary", "arbitrary"),
        name="delta_scan",
    )(*prep, *prep, s0)


def _ffn_prenorm(x, nffn_ref, sh2_ref, sc2_ref, rt_ref, h2_ref, lg_ref):
    h2 = _norm_mod(x, nffn_ref[...], sh2_ref[0], sc2_ref[0])
    tm, d = x.shape
    pieces = d // LANES
    for j in range(pieces):
        h2_ref[0, pl.ds(j, tm, stride=pieces), :] = h2[:, j * LANES:(j + 1) * LANES]
    lg_ref[0] = _dot_nt(rt_ref[...], h2.astype(BF16))


def _out0_kernel(of_ref, ob_ref, z_ref, p_ref, x_ref, band_ref, cnt_ref, onorm_ref, pw_ref, ps_ref, wout_ref,
                 g1_ref, sh2_ref, sc2_ref, nffn_ref, rt_ref, x1_ref, h2_ref, lg_ref):
    o = of_ref[0] + ob_ref[0]
    z = z_ref[0].astype(F32)
    pin = p_ref[0].astype(F32)
    parts = []
    for hd in range(HEADS):
        sl = slice(hd * HEAD_DIM, (hd + 1) * HEAD_DIM)
        oh = o[:, sl]
        ms = jnp.mean(oh * oh, axis=-1, keepdims=True)
        parts.append(oh * lax.rsqrt(ms + EPS) * onorm_ref[...] * _silu(z[:, sl]))
    for gi in range(len(POOL_WINDOWS)):
        sl = slice(gi * POOL_GROUP, (gi + 1) * POOL_GROUP)
        u = pin[:, sl]
        uh = u.astype(BF16)
        ul = (u - uh.astype(F32)).astype(BF16)
        band = band_ref[gi]
        wsum = _dot(band, uh) + _dot(band, ul)
        grp = wsum / cnt_ref[gi] - u
        parts.append(_dot(grp.astype(BF16), pw_ref[gi]) * ps_ref[:, sl])
    cat = jnp.concatenate(parts, axis=1).astype(BF16)
    x1 = x_ref[0] + g1_ref[0] * _dot(cat, wout_ref[...])
    x1_ref[0] = x1
    _ffn_prenorm(x1, nffn_ref, sh2_ref, sc2_ref, rt_ref, h2_ref, lg_ref)


def _pool_tables(tm):
    t = jnp.arange(tm)
    seg = t // GRID_W
    loc = t % GRID_W
    bands, cnts = [], []
    for w in POOL_WINDOWS:
        lo = jnp.clip(loc - w // 2, 0, GRID_W)
        hi = jnp.clip(loc + w - w // 2, 0, GRID_W)
        inside = (seg[:, None] == seg[None, :]) & (loc[None, :] >= lo[:, None]) & (loc[None, :] < hi[:, None])
        bands.append(inside.astype(BF16))
        cnts.append((hi - lo).astype(F32)[:, None])
    return jnp.stack(bands), jnp.stack(cnts)


def _out0_call(o_f, o_b, z, pin, x, onorm, pool_w, pool_scale, w_out, g1, sh2, sc2, nffn, router_t, tm):
    b, l, d = x.shape
    tm = min(tm, l)
    w = HEADS * HEAD_DIM
    e = router_t.shape[0]
    band, cnt = _pool_tables(tm)
    tok = lambda width: pl.BlockSpec((1, tm, width), lambda bi, i: (bi, i, 0))
    vec = pl.BlockSpec((1, 1, d), lambda bi, i: (bi, 0, 0))
    return pl.pallas_call(
        _out0_kernel,
        out_shape=[jax.ShapeDtypeStruct((b, l, d), F32), jax.ShapeDtypeStruct((b, l * d // LANES, LANES), F32),
                   jax.ShapeDtypeStruct((b, e, l), F32)],
        grid=(b, l // tm),
        in_specs=[tok(w), tok(w), tok(w), tok(w), tok(d),
                  _const_spec(band.shape), _const_spec(cnt.shape), _const_spec(onorm.shape),
                  _const_spec(pool_w.shape), _const_spec(pool_scale.shape), _const_spec(w_out.shape),
                  vec, vec, vec, _const_spec(nffn.shape), _const_spec(router_t.shape)],
        out_specs=[tok(d), pl.BlockSpec((1, tm * d // LANES, LANES), lambda bi, i: (bi, i, 0)),
                   pl.BlockSpec((1, e, tm), lambda bi, i: (bi, 0, i))],
        compiler_params=_params("arbitrary", "arbitrary"),
        name="out0",
    )(o_f, o_b, z, pin, x, band, cnt, onorm, pool_w, pool_scale, w_out, g1, sh2, sc2, nffn, router_t)


def _excl_cumsum_lanes(m):
    rows, n = m.shape
    ri = lax.broadcasted_iota(I32, (LANES, LANES), 0)
    ci = lax.broadcasted_iota(I32, (LANES, LANES), 1)
    upper = jnp.where(ri <= ci, 1.0, 0.0).astype(BF16)
    carry = jnp.zeros((rows, 1), F32)
    outs = []
    for blk in range(n // LANES):
        x = m[:, blk * LANES:(blk + 1) * LANES]
        inc = _dot(x.astype(BF16), upper)
        outs.append(inc - x + carry)
        carry = carry + inc[:, LANES - 1:LANES]
    return jnp.concatenate(outs, axis=1)


def _route_kernel(lg_ref, idx_ref, gate_ref, *, cap):
    x = lg_ref[0]
    e, n = x.shape
    ex = jnp.exp(x - jnp.max(x, axis=0, keepdims=True))
    aff = ex / jnp.sum(ex, axis=0, keepdims=True)

    def count_ge(bits):
        return jnp.sum(jnp.where(aff >= pltpu.bitcast(bits, F32), 1.0, 0.0), axis=1, keepdims=True)

    def bisect(_, lohi):
        lo, hi = lohi
        mid = lo + ((hi - lo + 1) >> 1)
        ok = count_ge(mid) >= cap
        return jnp.where(ok, mid, lo), jnp.where(ok, hi, mid - 1)

    lo0 = jnp.zeros((e, 1), I32)
    hi0 = jnp.full((e, 1), 0x7F800000, I32)
    thr, _ = lax.fori_loop(0, 31, bisect, (lo0, hi0))
    above = jnp.where(aff >= pltpu.bitcast(thr + 1, F32), 1.0, 0.0)
    tied = jnp.where(aff >= pltpu.bitcast(thr, F32), 1.0, 0.0) - above
    need = cap - jnp.sum(above, axis=1, keepdims=True)
    sel = above + tied * jnp.where(_excl_cumsum_lanes(tied) < need, 1.0, 0.0)
    pos = _excl_cumsum_lanes(sel)
    tok = lax.broadcasted_iota(I32, (e, n), 1)
    key = jnp.where(sel > 0.0, (tok - pos.astype(I32)) | VALID_BIT, 0)
    shift = 1
    while shift < n:
        mk = pltpu.roll(key, n - shift, axis=1)
        mt = pltpu.roll(tok, n - shift, axis=1)
        ma = pltpu.roll(aff, n - shift, axis=1)
        take = (mk & shift) != 0
        key = jnp.where(take, mk, jnp.where((key & shift) == 0, key, 0))
        tok = jnp.where(take, mt, tok)
        aff = jnp.where(take, ma, aff)
        shift *= 2
    idx_ref[0] = tok[:, :cap]
    gate_ref[0] = aff[:, :cap]


def _route_call(logits_t, cap):
    b, e, n = logits_t.shape
    return pl.pallas_call(
        functools.partial(_route_kernel, cap=cap),
        out_shape=[jax.ShapeDtypeStruct((b, e, cap), I32), jax.ShapeDtypeStruct((b, e, cap), F32)],
        grid=(b,),
        in_specs=[pl.BlockSpec((1, e, n), lambda bi: (bi, 0, 0))],
        out_specs=[pl.BlockSpec((1, e, cap), lambda bi: (bi, 0, 0))] * 2,
        compiler_params=_params("arbitrary"),
        name="route",
    )(logits_t)


GATHER_UNROLL = 8


def _gather_row(idx_ref, src_ref, dst_ref, s):
    t = idx_ref[0, 0, s]
    dst_ref[pl.ds(pl.multiple_of(s * SUBLANES, SUBLANES), SUBLANES), :] = (
        src_ref[0, pl.ds(pl.multiple_of(t * SUBLANES, SUBLANES), SUBLANES), :])


def _moe_up_kernel(idx_ref, idx_next_ref, src_ref, wg_ref, wu_ref, hid_ref, rows0_ref, rows1_ref, *, cap):
    e = pl.program_id(1)

    @pl.when(e == 0)
    def _():
        def gather(cidx, carry):
            for u in range(GATHER_UNROLL):
                _gather_row(idx_ref, src_ref, rows0_ref, cidx * GATHER_UNROLL + u)
            return carry
        lax.fori_loop(0, cap // GATHER_UNROLL, gather, 0)

    def step(cur_ref, next_ref):
        for s in range(cap):
            _gather_row(idx_next_ref, src_ref, next_ref, s)
        x = jnp.concatenate([cur_ref[pl.ds(j, cap, stride=SUBLANES), :].astype(BF16) for j in range(SUBLANES)],
                            axis=1)
        g = _dot(x, wg_ref[0].astype(BF16))
        u = _dot(x, wu_ref[0].astype(BF16))
        hid_ref[0, 0] = (_silu(g) * u).astype(BF16)

    @pl.when(e % 2 == 0)
    def _():
        step(rows0_ref, rows1_ref)

    @pl.when(e % 2 == 1)
    def _():
        step(rows1_ref, rows0_ref)


def _moe_up_call(idx, src, w_gate, w_up, cap):
    b, nrows, _ = src.shape
    e, d, f = w_gate.shape
    assert e % 2 == 0
    idx = idx.reshape(b * e, 1, cap)
    rows = pltpu.VMEM((cap * d // LANES, LANES), F32)
    wspec = pl.BlockSpec((1, d, f), lambda bi, ei: (ei, 0, 0))
    return pl.pallas_call(
        functools.partial(_moe_up_kernel, cap=cap),
        out_shape=jax.ShapeDtypeStruct((b, e, cap, f), BF16),
        grid=(b, e),
        in_specs=[
            pl.BlockSpec((1, 1, cap), lambda bi, ei: (bi * e + ei, 0, 0), memory_space=pltpu.SMEM),
            pl.BlockSpec((1, 1, cap), lambda bi, ei: (bi * e + jnp.minimum(ei + 1, e - 1), 0, 0),
                         memory_space=pltpu.SMEM),
            pl.BlockSpec((1, nrows, LANES), lambda bi, ei: (bi, 0, 0), pipeline_mode=pl.Buffered(1)),
            wspec, wspec,
        ],
        out_specs=pl.BlockSpec((1, 1, cap, f), lambda bi, ei: (bi, ei, 0, 0)),
        scratch_shapes=[rows, rows],
        compiler_params=_params("arbitrary", "arbitrary"),
        name="moe_up",
    )(idx, idx, src, w_gate, w_up)


SCATTER_UNROLL = 8
DOWN_SPLIT = 1


def _moe_down_kernel(idx_ref, gate_ref, hid_ref, hid_next_ref, wd_ref, wd_next_ref, acc_ref, y0_ref, y1_ref,
                     *, cap, rows):
    e = pl.program_id(1)

    def project(h_ref, w_ref, y_ref):
        y = _dot(h_ref[0, 0], w_ref[0])
        for j in range(rows):
            y_ref[pl.ds(j, cap, stride=rows), :] = y[:, j * LANES:(j + 1) * LANES]

    @pl.when(e == 0)
    def _():
        acc_ref[...] = jnp.zeros_like(acc_ref)
        project(hid_ref, wd_ref, y0_ref)

    def step(cur_ref, next_ref):
        project(hid_next_ref, wd_next_ref, next_ref)
        for base in range(0, cap, SCATTER_UNROLL):
            dsts, vals = [], []
            for s in range(base, base + SCATTER_UNROLL):
                t = idx_ref[0, 0, s]
                dst = pl.ds(pl.multiple_of(t * rows, rows), rows)
                dsts.append(dst)
                vals.append(acc_ref[0, dst, :] + gate_ref[0, 0, s] * cur_ref[s * rows:(s + 1) * rows, :])
            for dst, val in zip(dsts, vals):
                acc_ref[0, dst, :] = val

    @pl.when(e % 2 == 0)
    def _():
        step(y0_ref, y1_ref)

    @pl.when(e % 2 == 1)
    def _():
        step(y1_ref, y0_ref)


def _moe_down_call(idx, gate, hid, wd, n):
    b, e, cap, f = hid.shape
    d = wd.shape[2]
    assert e % 2 == 0
    dw = d // DOWN_SPLIT
    rows = dw // LANES
    nxt = lambda ei: jnp.minimum(ei + 1, e - 1)
    slot = lambda bh, ei: ((bh // DOWN_SPLIT) * e + ei, 0, 0)
    y = pltpu.VMEM((cap * rows, LANES), F32)
    out = pl.pallas_call(
        functools.partial(_moe_down_kernel, cap=cap, rows=rows),
        out_shape=jax.ShapeDtypeStruct((b * DOWN_SPLIT, n * rows, LANES), F32),
        grid=(b * DOWN_SPLIT, e),
        in_specs=[
            pl.BlockSpec((1, 1, cap), slot, memory_space=pltpu.SMEM),
            pl.BlockSpec((1, 1, cap), slot, memory_space=pltpu.SMEM),
            pl.BlockSpec((1, 1, cap, f), lambda bh, ei: (bh // DOWN_SPLIT, ei, 0, 0)),
            pl.BlockSpec((1, 1, cap, f), lambda bh, ei: (bh // DOWN_SPLIT, nxt(ei), 0, 0)),
            pl.BlockSpec((1, f, dw), lambda bh, ei: (ei, 0, bh % DOWN_SPLIT)),
            pl.BlockSpec((1, f, dw), lambda bh, ei: (nxt(ei), 0, bh % DOWN_SPLIT)),
        ],
        out_specs=pl.BlockSpec((1, n * rows, LANES), lambda bh, ei: (bh, 0, 0), pipeline_mode=pl.Buffered(1)),
        scratch_shapes=[y, y],
        compiler_params=_params("arbitrary", "arbitrary"),
        name="moe_down",
    )(idx.reshape(b * e, 1, cap), gate.reshape(b * e, 1, cap), hid, hid, wd, wd)
    return out.reshape(b, DOWN_SPLIT, n * rows, LANES)


def _moe(h2, logits_t, w_gate, w_up, wd):
    b, e, n = logits_t.shape
    cap = 2 * n // e
    idx, gate = _route_call(logits_t, cap)
    hid = _moe_up_call(idx, h2, w_gate, w_up, cap)
    return _moe_down_call(idx, gate, hid, wd, n)


def _join(m_ref, tm):
    rows = m_ref.shape[2] // tm
    return jnp.concatenate([m_ref[0, h, pl.ds(j, tm, stride=rows), :]
                            for h in range(DOWN_SPLIT) for j in range(rows)], axis=1)


def _mix1_kernel(xp_ref, x_ref, xn_ref, mp_ref, m_ref, mn_ref, g2p_ref, sh1_ref, sc1_ref, nmix_ref, win_ref,
                 conv_ref, wout_ref, g1_ref, sh2_ref, sc2_ref, nffn_ref, rt_ref, x3_ref, h2_ref, lg_ref, *, tm):
    i = pl.program_id(1)
    last = pl.num_programs(1) - 1
    d = x_ref.shape[2]
    xs = jnp.concatenate([xp_ref[0], x_ref[0], xn_ref[0]], axis=0)
    ms = jnp.concatenate([_join(mp_ref, SUBLANES), _join(m_ref, tm), _join(mn_ref, SUBLANES)], axis=0)
    x2 = xs + g2p_ref[0] * ms
    h = _norm_mod(x2, nmix_ref[...], sh1_ref[0], sc1_ref[0])
    row = lax.broadcasted_iota(I32, (tm + 2 * SUBLANES, 1), 0)
    dead = ((row < SUBLANES) & (i == 0)) | ((row >= tm + SUBLANES) & (i == last))
    hb = jnp.where(dead, 0.0, h).astype(BF16)
    proj = _dot(hb, win_ref[...])
    lo = SUBLANES
    u = proj[:, d:2 * d] * proj[:, 2 * d:]
    cw = conv_ref[...]
    cv = u[lo - 1:lo - 1 + tm] * cw[0:1] + u[lo:lo + tm] * cw[1:2] + u[lo + 1:lo + 1 + tm] * cw[2:3]
    y = _dot((proj[lo:lo + tm, :d] * cv).astype(BF16), wout_ref[...])
    x3 = x2[lo:lo + tm] + g1_ref[0] * y
    x3_ref[0] = x3
    _ffn_prenorm(x3, nffn_ref, sh2_ref, sc2_ref, rt_ref, h2_ref, lg_ref)


def _mix1_call(x1, moe, g2p, sh1, sc1, nmix, w_in, conv_w, w_out, g1, sh2, sc2, nffn, router_t, tm):
    b, l, d = x1.shape
    tm = min(tm, l)
    nb8 = l // SUBLANES
    r = tm // SUBLANES
    e = router_t.shape[0]
    mr = d // DOWN_SPLIT // LANES
    prev = lambda bi, i: (bi, jnp.maximum(i * r - 1, 0), 0)
    nxt = lambda bi, i: (bi, jnp.minimum((i + 1) * r, nb8 - 1), 0)
    tok = pl.BlockSpec((1, tm, d), lambda bi, i: (bi, i, 0))
    vec = pl.BlockSpec((1, 1, d), lambda bi, i: (bi, 0, 0))
    return pl.pallas_call(
        functools.partial(_mix1_kernel, tm=tm),
        out_shape=[jax.ShapeDtypeStruct((b, l, d), F32), jax.ShapeDtypeStruct((b, l * d // LANES, LANES), F32),
                   jax.ShapeDtypeStruct((b, e, l), F32)],
        grid=(b, l // tm),
        in_specs=[
            pl.BlockSpec((1, SUBLANES, d), prev), tok, pl.BlockSpec((1, SUBLANES, d), nxt),
            pl.BlockSpec((1, DOWN_SPLIT, SUBLANES * mr, LANES), lambda bi, i: (bi, 0, jnp.maximum(i * r - 1, 0), 0)),
            pl.BlockSpec((1, DOWN_SPLIT, tm * mr, LANES), lambda bi, i: (bi, 0, i, 0)),
            pl.BlockSpec((1, DOWN_SPLIT, SUBLANES * mr, LANES),
                         lambda bi, i: (bi, 0, jnp.minimum((i + 1) * r, nb8 - 1), 0)),
            vec, vec, vec, _const_spec(nmix.shape), _const_spec(w_in.shape), _const_spec(conv_w.shape),
            _const_spec(w_out.shape), vec, vec, vec, _const_spec(nffn.shape), _const_spec(router_t.shape),
        ],
        out_specs=[tok, pl.BlockSpec((1, tm * d // LANES, LANES), lambda bi, i: (bi, i, 0)),
                   pl.BlockSpec((1, e, tm), lambda bi, i: (bi, 0, i))],
        compiler_params=_params("arbitrary", "arbitrary"),
        name="mix1",
    )(x1, x1, x1, moe, moe, moe, g2p, sh1, sc1, nmix, w_in, conv_w, w_out, g1, sh2, sc2, nffn, router_t)


def _final_kernel(x_ref, m_ref, g2_ref, nf_ref, o_ref):
    x = x_ref[0] + g2_ref[0] * _join(m_ref, x_ref.shape[1])
    ms = jnp.mean(x * x, axis=-1, keepdims=True)
    o_ref[0] = x * lax.rsqrt(ms + EPS) * nf_ref[...]


def _final_call(x3, moe, g2, nf, tm):
    b, l, d = x3.shape
    tm = min(tm, l)
    mr = d // DOWN_SPLIT // LANES
    tok = pl.BlockSpec((1, tm, d), lambda bi, i: (bi, i, 0))
    return pl.pallas_call(
        _final_kernel,
        out_shape=jax.ShapeDtypeStruct((b, l, d), F32),
        grid=(b, l // tm),
        in_specs=[tok, pl.BlockSpec((1, DOWN_SPLIT, tm * mr, LANES), lambda bi, i: (bi, 0, i, 0)),
                  pl.BlockSpec((1, 1, d), lambda bi, i: (bi, 0, 0)), _const_spec(nf.shape)],
        out_specs=tok,
        compiler_params=_params("arbitrary", "arbitrary"),
        name="final",
    )(x3, moe, g2, nf)


def _layer0_mixer_inputs(ev_w_in, dn_a_log, dn_dt_bias):
    w = HEADS * HEAD_DIM
    qkv_w = 3 * w
    wqkv = ev_w_in[:, :qkv_w].astype(BF16)
    wz = ev_w_in[:, qkv_w:qkv_w + w]
    wg = ev_w_in[:, qkv_w + w:qkv_w + w + 4 * HEADS]
    wp = ev_w_in[:, qkv_w + w + 4 * HEADS:]
    wrest = jnp.concatenate([wz, wp, jnp.pad(wg, ((0, 0), (0, LANES - 4 * HEADS)))], axis=1).astype(BF16)
    pad = jnp.zeros((2 * HEADS,), F32)
    tail = jnp.zeros((LANES - 4 * HEADS,), F32)
    gpar = jnp.stack([jnp.concatenate([pad, dn_a_log.reshape(-1), tail]),
                      jnp.concatenate([pad, dn_dt_bias.reshape(-1), tail])])
    return wqkv, wrest, gpar


def _chunk_rows(g):
    b, l, c = g.shape
    return g.reshape(b, l // CHUNK, CHUNK, c).transpose(0, 1, 3, 2)


def kernel(x, c, ctx, c_ctx, ada_w, ada_b, norm_mix, norm_ffn, norm_final, ev_w_in, dn_conv, dn_a_log, dn_dt_bias,
           dn_norm, pool_w, pool_scale, ev_w_out, sc_w_in, sc_conv, sc_w_out, router, w_gate, w_up, w_down):
    b, n, d = x.shape
    depth = ada_w.shape[0]
    span = max(PREP_CHUNKS, SCAN_CHUNKS) * CHUNK
    assert depth == 2 and b + 1 <= SUBLANES and n % GRID_W == 0 and n % span == 0 and ctx.shape[1] % span == 0

    cc = jnp.concatenate([c, c_ctx[None], jnp.zeros((SUBLANES - b - 1, d), F32)], axis=0)
    mods = _ada_call(cc, ada_w, ada_b)

    def mod(layer, k, rows=slice(0, b)):
        return mods[layer, rows, None, k * d:(k + 1) * d]

    ctx_rows = lambda layer, k: jnp.broadcast_to(mods[layer, b:b + 1, None, k * d:(k + 1) * d], (b, 1, d))
    row = lambda v: v.reshape(1, -1)
    router_t = jnp.swapaxes(router, 1, 2).astype(BF16)
    wdn = w_down.astype(BF16)

    wqkv, wrest, gpar = _layer0_mixer_inputs(ev_w_in[0], dn_a_log[0], dn_dt_bias[0])
    nmix0 = row(norm_mix[0])
    qc, kc, vc, _, _, gc = _proj0_call(ctx, ctx_rows(0, 0), ctx_rows(0, 1), nmix0, wqkv, wrest, dn_conv[0], gpar, 256)
    ql, kl, vl, zl, pl_in, gl = _proj0_call(x, mod(0, 0), mod(0, 1), nmix0, wqkv, wrest, dn_conv[0], gpar, 512)
    s0 = jnp.zeros((b, 2, HEADS, HEAD_DIM, HEAD_DIM), F32)
    _, _, s_ctx = _delta_scan_call(_delta_prep_call(qc, kc, vc, gc, _chunk_rows(gc)), s0)
    o_f, o_b, _ = _delta_scan_call(_delta_prep_call(ql, kl, vl, gl, _chunk_rows(gl)), s_ctx)
    x1, h2, lg = _out0_call(o_f, o_b, zl, pl_in, x, row(dn_norm[0]), pool_w[0].astype(BF16), row(pool_scale[0]),
                            ev_w_out[0].astype(BF16), mod(0, 2), mod(0, 3), mod(0, 4), row(norm_ffn[0]),
                            router_t[0], 256)
    moe0 = _moe(h2, lg, w_gate[0], w_up[0], wdn[0])

    x3, h2, lg = _mix1_call(x1, moe0, mod(0, 5), mod(1, 0), mod(1, 1), row(norm_mix[1]), sc_w_in[0].astype(BF16),
                            sc_conv[0], sc_w_out[0].astype(BF16), mod(1, 2), mod(1, 3), mod(1, 4),
                            row(norm_ffn[1]), router_t[1], 256)
    moe1 = _moe(h2, lg, w_gate[1], w_up[1], wdn[1])
    return _final_call(x3, moe1, mod(1, 5), row(norm_final), 512)
```

```python
import functools

import jax
import jax.numpy as jnp
from jax import lax
from jax.experimental import pallas as pl
from jax.experimental.pallas import tpu as pltpu

F32 = jnp.float32
BF16 = jnp.bfloat16
I32 = jnp.int32

EPS = 1e-6
GRID_W = 64
HEADS = 4
HEAD_DIM = 128
CHUNK = 64
POOL_WINDOWS = (2, 4, 8, 16)
POOL_GROUP = 128
LANES = 128
SUBLANES = 8
VMEM_LIMIT = 56 * 1024 * 1024
VALID_BIT = 1 << 30


def _silu(x):
    return x * jax.nn.sigmoid(x)


def _norm_mod(x, g, shift, scale):
    ms = jnp.mean(x * x, axis=-1, keepdims=True)
    return (x * lax.rsqrt(ms + EPS) * g) * (1.0 + scale) + shift


def _dot(a, b):
    return jnp.dot(a, b, preferred_element_type=F32)


def _dot_nt(a, b):
    return lax.dot_general(a, b, (((1,), (1,)), ((), ())), preferred_element_type=F32)


def _dot_tn(a, b):
    return lax.dot_general(a, b, (((0,), (0,)), ((), ())), preferred_element_type=F32)


def _split3(x):
    hi = x.astype(BF16)
    r = x - hi.astype(F32)
    mid = r.astype(BF16)
    lo = (r - mid.astype(F32)).astype(BF16)
    return hi, mid, lo


def _const_spec(shape):
    nd = len(shape)
    return pl.BlockSpec(shape, lambda *_: (0,) * nd, pipeline_mode=pl.Buffered(1))


def _params(*sem):
    return pltpu.CompilerParams(dimension_semantics=sem, vmem_limit_bytes=VMEM_LIMIT)


def _ada_kernel(c_ref, w_ref, b_ref, o_ref):
    s = _silu(c_ref[...])
    o_ref[0] = _dot(s.astype(BF16), w_ref[0].astype(BF16)) + b_ref[0]


def _ada_call(cc, ada_w, ada_b):
    depth, d, n6 = ada_w.shape
    tn = n6 // 4
    return pl.pallas_call(
        _ada_kernel,
        out_shape=jax.ShapeDtypeStruct((depth, SUBLANES, n6), F32),
        grid=(depth, n6 // tn),
        in_specs=[
            pl.BlockSpec((SUBLANES, d), lambda i, j: (0, 0)),
            pl.BlockSpec((1, d, tn), lambda i, j: (i, 0, j)),
            pl.BlockSpec((1, 1, tn), lambda i, j: (i, 0, j)),
        ],
        out_specs=pl.BlockSpec((1, SUBLANES, tn), lambda i, j: (i, 0, j)),
        compiler_params=_params("arbitrary", "arbitrary"),
        name="adaln",
    )(cc, ada_w, ada_b.reshape(depth, 1, n6))


def _proj0_kernel(xp_ref, x_ref, xn_ref, sh_ref, sc_ref, g_ref, wqkv_ref, wrest_ref, conv_ref, gpar_ref,
                  q_ref, k_ref, v_ref, z_ref, p_ref, gt_ref, *, tm):
    i = pl.program_id(1)
    last = pl.num_programs(1) - 1
    x = jnp.concatenate([xp_ref[0], x_ref[0], xn_ref[0]], axis=0)
    h = _norm_mod(x, g_ref[...], sh_ref[0], sc_ref[0])
    row = lax.broadcasted_iota(I32, (tm + 2 * SUBLANES, 1), 0)
    dead = ((row < SUBLANES) & (i == 0)) | ((row >= tm + SUBLANES) & (i == last))
    hb = jnp.where(dead, 0.0, h).astype(BF16)
    proj = _dot(hb, wqkv_ref[...])
    cw = conv_ref[...]
    lo = SUBLANES
    a = (proj[lo - 1:lo - 1 + tm] * cw[0:1] + proj[lo:lo + tm] * cw[1:2] + proj[lo + 1:lo + 1 + tm] * cw[2:3])
    a = _silu(a)
    w = HEADS * HEAD_DIM
    for hd in range(HEADS):
        sl = slice(hd * HEAD_DIM, (hd + 1) * HEAD_DIM)
        qh = a[:, sl]
        kh = a[:, w + hd * HEAD_DIM: w + (hd + 1) * HEAD_DIM]
        qn = qh * lax.rsqrt(jnp.sum(qh * qh, axis=-1, keepdims=True) + EPS) * (HEAD_DIM ** -0.5)
        q_ref[0, :, sl] = qn.astype(BF16)
        k_ref[0, :, sl] = (kh * lax.rsqrt(jnp.sum(kh * kh, axis=-1, keepdims=True) + EPS)).astype(BF16)
    v_ref[0] = a[:, 2 * w:].astype(BF16)
    rest = _dot(hb[lo:lo + tm], wrest_ref[...])
    z_ref[0] = rest[:, :w].astype(BF16)
    p_ref[0] = rest[:, w:2 * w].astype(BF16)
    gates = rest[:, 2 * w:]
    col = lax.broadcasted_iota(I32, (1, LANES), 1)
    xb = gates + gpar_ref[1:2]
    softplus = jnp.maximum(xb, 0.0) + jnp.log1p(jnp.exp(-jnp.abs(xb)))
    log_decay = -jnp.exp(gpar_ref[0:1]) * softplus
    out = jnp.where(col < 2 * HEADS, jax.nn.sigmoid(gates), log_decay)
    gt_ref[0] = out[:, :4 * HEADS]


def _proj0_call(x, shift, scale, gain, wqkv, wrest, conv_w, gpar, tm):
    b, l, d = x.shape
    tm = min(tm, l)
    nt = l // tm
    nb8 = l // SUBLANES
    r = tm // SUBLANES
    w = HEADS * HEAD_DIM
    tok = lambda width: pl.BlockSpec((1, tm, width), lambda bi, i: (bi, i, 0))
    vec = pl.BlockSpec((1, 1, d), lambda bi, i: (bi, 0, 0))
    return pl.pallas_call(
        functools.partial(_proj0_kernel, tm=tm),
        out_shape=[jax.ShapeDtypeStruct((b, l, w), BF16)] * 5 + [jax.ShapeDtypeStruct((b, l, 4 * HEADS), F32)],
        grid=(b, nt),
        in_specs=[
            pl.BlockSpec((1, SUBLANES, d), lambda bi, i: (bi, jnp.maximum(i * r - 1, 0), 0)),
            tok(d),
            pl.BlockSpec((1, SUBLANES, d), lambda bi, i: (bi, jnp.minimum((i + 1) * r, nb8 - 1), 0)),
            vec, vec,
            _const_spec((1, d)),
            _const_spec(wqkv.shape),
            _const_spec(wrest.shape),
            _const_spec(conv_w.shape),
            _const_spec(gpar.shape),
        ],
        out_specs=[tok(w)] * 5 + [tok(4 * HEADS)],
        compiler_params=_params("arbitrary", "arbitrary"),
        name="proj0",
    )(x, x, x, shift, scale, gain, wqkv, wrest, conv_w, gpar)


PREP_CHUNKS = 4
SCAN_CHUNKS = 2


def _stack_masked(x, block_of_lane, nblocks):
    return jnp.concatenate([jnp.where(block_of_lane == h, x, jnp.zeros_like(x)) for h in range(nblocks)], axis=0)


def _delta_prep_kernel(q_ref, k_ref, v_ref, g_ref, u_ref, wk_ref, qd_ref, kdt_ref, qk_ref, cd_ref):
    c = CHUNK
    wc = HEADS * c
    wd = HEADS * HEAD_DIM
    ri = lax.broadcasted_iota(I32, (c, c), 0)
    ci = lax.broadcasted_iota(I32, (c, c), 1)
    tri_l = jnp.where(ri >= ci, 1.0, 0.0).astype(BF16)
    tri_u = jnp.where(ri <= ci, 1.0, 0.0).astype(BF16)
    row = lax.broadcasted_iota(I32, (c, wc), 0)
    lane = lax.broadcasted_iota(I32, (c, wc), 1)
    pos = lane % c
    blk_c = lax.broadcasted_iota(I32, (1, wc), 1) // c
    blk_d = lax.broadcasted_iota(I32, (1, wd), 1) // HEAD_DIM
    eye = jnp.where(row == pos, 1.0, 0.0)
    incl = (row >= pos, row <= pos)
    strict = (row > pos, row < pos)
    tot = (c - 1, 0)
    chunks = range(PREP_CHUNKS)
    heads = range(HEADS)
    tok = [slice(j * c, (j + 1) * c) for j in chunks]

    def spread(cols, first, width):
        full = [jnp.broadcast_to(cols[:, first + hd:first + hd + 1], (c, LANES)) for hd in heads]
        if width == LANES:
            return jnp.concatenate(full, axis=1)
        half = lax.broadcasted_iota(I32, (c, LANES), 1) < width
        return jnp.concatenate([jnp.where(half, full[2 * i], full[2 * i + 1]) for i in range(HEADS // 2)], axis=1)

    k = [k_ref[0, tok[j], :] for j in chunks]
    q = [q_ref[0, tok[j], :] for j in chunks]
    g = [g_ref[0, tok[j], :] for j in chunks]
    kq = [_dot_nt(jnp.concatenate([k[j], q[j]], axis=0), _stack_masked(k[j], blk_d, HEADS)) for j in chunks]
    g3 = [_split3(g[j]) for j in chunks]
    cum = [(sum(_dot(tri_l, p) for p in g3[j]), sum(_dot(tri_u, p) for p in g3[j])) for j in chunks]

    chains = [(j, d) for j in chunks for d in range(2)]
    beta_d, gam_d, decay, a = {}, {}, {}, {}
    for j, d in chains:
        key = (j, d)
        gam_c = spread(cum[j][d], 2 * HEADS + d * HEADS, c)
        gam_d[key] = spread(cum[j][d], 2 * HEADS + d * HEADS, HEAD_DIM)
        beta_c = spread(g[j], d * HEADS, c)
        beta_d[key] = spread(g[j], d * HEADS, HEAD_DIM)
        gam_r = jnp.sum(jnp.where(row == pos, gam_c, 0.0), axis=0, keepdims=True)
        diff = gam_c - gam_r
        decay[key] = jnp.where(incl[d], jnp.exp(jnp.where(incl[d], diff, 0.0)), 0.0)
        a[key] = jnp.where(strict[d], kq[j][:c] * decay[key], 0.0) * beta_c
    p = dict(a)
    tinv = {key: eye - a[key] for key in chains}
    for _ in range(c.bit_length() - 2):
        pb = {key: p[key].astype(BF16) for key in chains}
        p = {key: _dot(pb[key], _stack_masked(pb[key], blk_c, HEADS)) for key in chains}
        pb = {key: p[key].astype(BF16) for key in chains}
        tinv = {key: tinv[key] + _dot(tinv[key].astype(BF16), _stack_masked(pb[key], blk_c, HEADS))
                for key in chains}
    eg, kf, u, wk = {}, {}, {}, {}
    for j, d in chains:
        key = (j, d)
        kf[key] = k[j].astype(F32)
        eg[key] = jnp.exp(gam_d[key])
        tb = tinv[key].astype(BF16)
        rhs_u = (v_ref[0, tok[j], :].astype(F32) * beta_d[key]).astype(BF16)
        rhs_w = (kf[key] * (beta_d[key] * eg[key])).astype(BF16)
        u[key] = _dot(tb, _stack_masked(rhs_u, blk_d, HEADS))
        wk[key] = _dot(tb, _stack_masked(rhs_w, blk_d, HEADS))
    for j, d in chains:
        key = (j, d)
        gtot = gam_d[key][tot[d]:tot[d] + 1, :]
        u_ref[0, d, tok[j], :] = u[key]
        wk_ref[0, d, tok[j], :] = wk[key].astype(BF16)
        qd_ref[0, d, tok[j], :] = (q[j].astype(F32) * eg[key]).astype(BF16)
        qk_ref[0, d, tok[j], :] = (kq[j][c:] * decay[key]).astype(BF16)
        kd = kf[key] * jnp.exp(gtot - gam_d[key])
        for hd in heads:
            kdt_ref[0, d, j, :, hd * c:(hd + 1) * c] = kd[:, hd * HEAD_DIM:(hd + 1) * HEAD_DIM].T.astype(BF16)
        cd = jnp.exp(gtot)
        cd_ref[0, d, j] = jnp.concatenate([cd[:, hd * HEAD_DIM:(hd + 1) * HEAD_DIM] for hd in heads]
                                          + [jnp.zeros((SUBLANES - HEADS, LANES), F32)], axis=0)


def _delta_prep_call(q, k, v, g):
    b, l, w = k.shape
    nc = l // CHUNK
    cps = PREP_CHUNKS
    t = cps * CHUNK
    tok = lambda width: pl.BlockSpec((1, t, width), lambda bi, n: (bi, n, 0))
    dtok = lambda width: pl.BlockSpec((1, 2, t, width), lambda bi, n: (bi, 0, n, 0))
    return pl.pallas_call(
        _delta_prep_kernel,
        out_shape=[
            jax.ShapeDtypeStruct((b, 2, l, w), F32),
            jax.ShapeDtypeStruct((b, 2, l, w), BF16),
            jax.ShapeDtypeStruct((b, 2, l, w), BF16),
            jax.ShapeDtypeStruct((b, 2, nc, HEAD_DIM, HEADS * CHUNK), BF16),
            jax.ShapeDtypeStruct((b, 2, l, HEADS * CHUNK), BF16),
            jax.ShapeDtypeStruct((b, 2, nc, SUBLANES, LANES), F32),
        ],
        grid=(b, nc // cps),
        in_specs=[tok(w), tok(w), tok(w), tok(4 * HEADS)],
        out_specs=[dtok(w), dtok(w), dtok(w),
                   pl.BlockSpec((1, 2, cps, HEAD_DIM, HEADS * CHUNK), lambda bi, n: (bi, 0, n, 0, 0)),
                   dtok(HEADS * CHUNK),
                   pl.BlockSpec((1, 2, cps, SUBLANES, LANES), lambda bi, n: (bi, 0, n, 0, 0))],
        compiler_params=_params("arbitrary", "arbitrary"),
        name="delta_prep",
    )(q, k, v, g)


def _delta_scan_kernel(*refs):
    ins = (refs[0:6], refs[6:12])
    s0_ref, of_ref, ob_ref, sfin_ref, s_ref = refs[12:]
    outs = (of_ref, ob_ref)
    n = pl.program_id(1)
    c = CHUNK
    cps = SCAN_CHUNKS

    @pl.when(n == 0)
    def _():
        s_ref[...] = s0_ref[0]

    combos = [(d, hd) for d in range(2) for hd in range(HEADS)]
    sl = lambda hd: slice(hd * HEAD_DIM, (hd + 1) * HEAD_DIM)
    cs = lambda hd: slice(hd * c, (hd + 1) * c)
    s = {(d, hd): s_ref[d, hd] for d, hd in combos}
    for step in range(cps):
        pos = (step, cps - 1 - step)
        tok = [slice(pos[d] * c, (pos[d] + 1) * c) for d in range(2)]
        m1, w, m2 = {}, {}, {}
        for d, hd in combos:
            u_ref, wk_ref, qd_ref, kdt_ref, qk_ref, cd_ref = ins[d]
            lhs = jnp.concatenate([wk_ref[0, 0, tok[d], sl(hd)], qd_ref[0, 0, tok[d], sl(hd)]], axis=0)
            m1[d, hd] = _dot(lhs, s[d, hd].astype(BF16))
        for d, hd in combos:
            w[d, hd] = (ins[d][0][0, 0, tok[d], sl(hd)] - m1[d, hd][:c]).astype(BF16)
        for d, hd in combos:
            u_ref, wk_ref, qd_ref, kdt_ref, qk_ref, cd_ref = ins[d]
            lhs = jnp.concatenate([qk_ref[0, 0, tok[d], cs(hd)], kdt_ref[0, 0, pos[d], :, cs(hd)]], axis=0)
            m2[d, hd] = _dot(lhs, w[d, hd])
        for d, hd in combos:
            outs[d][0, tok[d], sl(hd)] = m1[d, hd][c:] + m2[d, hd][:c]
            s[d, hd] = ins[d][5][0, 0, pos[d], hd:hd + 1, :] * s[d, hd] + m2[d, hd][c:]
    for d, hd in combos:
        s_ref[d, hd] = s[d, hd]

    @pl.when(n == pl.num_programs(1) - 1)
    def _():
        sfin_ref[0] = s_ref[...]


def _delta_scan_call(prep, s0):
    u, wk, qd, kdt, qk, cd = prep
    b, _, l, w = u.shape
    cps = SCAN_CHUNKS
    t = cps * CHUNK
    ns = l // t

    def specs(d, blk):
        tok = lambda width: pl.BlockSpec((1, 1, t, width), lambda bi, n: (bi, d, blk(n), 0))
        return [tok(w), tok(w), tok(w),
                pl.BlockSpec((1, 1, cps, HEAD_DIM, HEADS * CHUNK), lambda bi, n: (bi, d, blk(n), 0, 0)),
                tok(HEADS * CHUNK),
                pl.BlockSpec((1, 1, cps, SUBLANES, LANES), lambda bi, n: (bi, d, blk(n), 0, 0))]

    st = pl.BlockSpec((1, 2, HEADS, HEAD_DIM, HEAD_DIM), lambda bi, n: (bi, 0, 0, 0, 0))
    return pl.pallas_call(
        _delta_scan_kernel,
        out_shape=[jax.ShapeDtypeStruct((b, l, w), F32)] * 2
        + [jax.ShapeDtypeStruct((b, 2, HEADS, HEAD_DIM, HEAD_DIM), F32)],
        grid=(b, ns),
        in_specs=specs(0, lambda n: n) + specs(1, lambda n: ns - 1 - n) + [st],
        out_specs=[pl.BlockSpec((1, t, w), lambda bi, n: (bi, n, 0)),
                   pl.BlockSpec((1, t, w), lambda bi, n: (bi, ns - 1 - n, 0)), st],
        scratch_shapes=[pltpu.VMEM((2, HEADS, HEAD_DIM, HEAD_DIM), F32)],
        compiler_params=_params("arbitrary", "arbitrary"),
        name="delta_scan",
    )(*prep, *prep, s0)


def _ffn_prenorm(x, nffn_ref, sh2_ref, sc2_ref, rt_ref, h2_ref, lg_ref):
    h2 = _norm_mod(x, nffn_ref[...], sh2_ref[0], sc2_ref[0])
    tm, d = x.shape
    pieces = d // LANES
    for j in range(pieces):
        h2_ref[0, pl.ds(j, tm, stride=pieces), :] = h2[:, j * LANES:(j + 1) * LANES]
    lg_ref[0] = _dot_nt(rt_ref[...], h2.astype(BF16))


def _out0_kernel(of_ref, ob_ref, z_ref, p_ref, x_ref, band_ref, cnt_ref, onorm_ref, pw_ref, ps_ref, wout_ref,
                 g1_ref, sh2_ref, sc2_ref, nffn_ref, rt_ref, x1_ref, h2_ref, lg_ref):
    o = of_ref[0] + ob_ref[0]
    z = z_ref[0].astype(F32)
    pin = p_ref[0].astype(F32)
    parts = []
    for hd in range(HEADS):
        sl = slice(hd * HEAD_DIM, (hd + 1) * HEAD_DIM)
        oh = o[:, sl]
        ms = jnp.mean(oh * oh, axis=-1, keepdims=True)
        parts.append(oh * lax.rsqrt(ms + EPS) * onorm_ref[...] * _silu(z[:, sl]))
    for gi in range(len(POOL_WINDOWS)):
        sl = slice(gi * POOL_GROUP, (gi + 1) * POOL_GROUP)
        u = pin[:, sl]
        uh = u.astype(BF16)
        ul = (u - uh.astype(F32)).astype(BF16)
        band = band_ref[gi]
        wsum = _dot(band, uh) + _dot(band, ul)
        grp = wsum / cnt_ref[gi] - u
        parts.append(_dot(grp.astype(BF16), pw_ref[gi]) * ps_ref[:, sl])
    cat = jnp.concatenate(parts, axis=1).astype(BF16)
    x1 = x_ref[0] + g1_ref[0] * _dot(cat, wout_ref[...])
    x1_ref[0] = x1
    _ffn_prenorm(x1, nffn_ref, sh2_ref, sc2_ref, rt_ref, h2_ref, lg_ref)


def _pool_tables(tm):
    t = jnp.arange(tm)
    seg = t // GRID_W
    loc = t % GRID_W
    bands, cnts = [], []
    for w in POOL_WINDOWS:
        lo = jnp.clip(loc - w // 2, 0, GRID_W)
        hi = jnp.clip(loc + w - w // 2, 0, GRID_W)
        inside = (seg[:, None] == seg[None, :]) & (loc[None, :] >= lo[:, None]) & (loc[None, :] < hi[:, None])
        bands.append(inside.astype(BF16))
        cnts.append((hi - lo).astype(F32)[:, None])
    return jnp.stack(bands), jnp.stack(cnts)


def _out0_call(o_f, o_b, z, pin, x, onorm, pool_w, pool_scale, w_out, g1, sh2, sc2, nffn, router_t, tm):
    b, l, d = x.shape
    tm = min(tm, l)
    w = HEADS * HEAD_DIM
    e = router_t.shape[0]
    band, cnt = _pool_tables(tm)
    tok = lambda width: pl.BlockSpec((1, tm, width), lambda bi, i: (bi, i, 0))
    vec = pl.BlockSpec((1, 1, d), lambda bi, i: (bi, 0, 0))
    return pl.pallas_call(
        _out0_kernel,
        out_shape=[jax.ShapeDtypeStruct((b, l, d), F32), jax.ShapeDtypeStruct((b, l * d // LANES, LANES), F32),
                   jax.ShapeDtypeStruct((b, e, l), F32)],
        grid=(b, l // tm),
        in_specs=[tok(w), tok(w), tok(w), tok(w), tok(d),
                  _const_spec(band.shape), _const_spec(cnt.shape), _const_spec(onorm.shape),
                  _const_spec(pool_w.shape), _const_spec(pool_scale.shape), _const_spec(w_out.shape),
                  vec, vec, vec, _const_spec(nffn.shape), _const_spec(router_t.shape)],
        out_specs=[tok(d), pl.BlockSpec((1, tm * d // LANES, LANES), lambda bi, i: (bi, i, 0)),
                   pl.BlockSpec((1, e, tm), lambda bi, i: (bi, 0, i))],
        compiler_params=_params("arbitrary", "arbitrary"),
        name="out0",
    )(o_f, o_b, z, pin, x, band, cnt, onorm, pool_w, pool_scale, w_out, g1, sh2, sc2, nffn, router_t)


def _excl_cumsum_lanes(m):
    rows, n = m.shape
    ri = lax.broadcasted_iota(I32, (LANES, LANES), 0)
    ci = lax.broadcasted_iota(I32, (LANES, LANES), 1)
    upper = jnp.where(ri <= ci, 1.0, 0.0).astype(BF16)
    carry = jnp.zeros((rows, 1), F32)
    outs = []
    for blk in range(n // LANES):
        x = m[:, blk * LANES:(blk + 1) * LANES]
        inc = _dot(x.astype(BF16), upper)
        outs.append(inc - x + carry)
        carry = carry + inc[:, LANES - 1:LANES]
    return jnp.concatenate(outs, axis=1)


def _route_kernel(lg_ref, idx_ref, gate_ref, *, cap):
    x = lg_ref[0]
    e, n = x.shape
    ex = jnp.exp(x - jnp.max(x, axis=0, keepdims=True))
    aff = ex / jnp.sum(ex, axis=0, keepdims=True)

    def count_ge(bits):
        return jnp.sum(jnp.where(aff >= pltpu.bitcast(bits, F32), 1.0, 0.0), axis=1, keepdims=True)

    def bisect(_, lohi):
        lo, hi = lohi
        mid = lo + ((hi - lo + 1) >> 1)
        ok = count_ge(mid) >= cap
        return jnp.where(ok, mid, lo), jnp.where(ok, hi, mid - 1)

    lo0 = jnp.zeros((e, 1), I32)
    hi0 = jnp.full((e, 1), 0x7F800000, I32)
    thr, _ = lax.fori_loop(0, 31, bisect, (lo0, hi0))
    above = jnp.where(aff >= pltpu.bitcast(thr + 1, F32), 1.0, 0.0)
    tied = jnp.where(aff >= pltpu.bitcast(thr, F32), 1.0, 0.0) - above
    need = cap - jnp.sum(above, axis=1, keepdims=True)
    sel = above + tied * jnp.where(_excl_cumsum_lanes(tied) < need, 1.0, 0.0)
    pos = _excl_cumsum_lanes(sel)
    tok = lax.broadcasted_iota(I32, (e, n), 1)
    key = jnp.where(sel > 0.0, (tok - pos.astype(I32)) | VALID_BIT, 0)
    shift = 1
    while shift < n:
        mk = pltpu.roll(key, n - shift, axis=1)
        mt = pltpu.roll(tok, n - shift, axis=1)
        ma = pltpu.roll(aff, n - shift, axis=1)
        take = (mk & shift) != 0
        key = jnp.where(take, mk, jnp.where((key & shift) == 0, key, 0))
        tok = jnp.where(take, mt, tok)
        aff = jnp.where(take, ma, aff)
        shift *= 2
    idx_ref[0] = tok[:, :cap]
    gate_ref[0] = aff[:, :cap]


def _route_call(logits_t, cap):
    b, e, n = logits_t.shape
    return pl.pallas_call(
        functools.partial(_route_kernel, cap=cap),
        out_shape=[jax.ShapeDtypeStruct((b, e, cap), I32), jax.ShapeDtypeStruct((b, e, cap), F32)],
        grid=(b,),
        in_specs=[pl.BlockSpec((1, e, n), lambda bi: (bi, 0, 0))],
        out_specs=[pl.BlockSpec((1, e, cap), lambda bi: (bi, 0, 0))] * 2,
        compiler_params=_params("arbitrary"),
        name="route",
    )(logits_t)


GATHER_UNROLL = 8


def _gather_row(idx_ref, src_ref, dst_ref, s):
    t = idx_ref[0, 0, s]
    dst_ref[pl.ds(pl.multiple_of(s * SUBLANES, SUBLANES), SUBLANES), :] = (
        src_ref[0, pl.ds(pl.multiple_of(t * SUBLANES, SUBLANES), SUBLANES), :])


def _moe_up_kernel(idx_ref, idx_next_ref, src_ref, wg_ref, wu_ref, hid_ref, rows0_ref, rows1_ref, *, cap):
    e = pl.program_id(1)

    @pl.when(e == 0)
    def _():
        def gather(cidx, carry):
            for u in range(GATHER_UNROLL):
                _gather_row(idx_ref, src_ref, rows0_ref, cidx * GATHER_UNROLL + u)
            return carry
        lax.fori_loop(0, cap // GATHER_UNROLL, gather, 0)

    def step(cur_ref, next_ref):
        for s in range(cap):
            _gather_row(idx_next_ref, src_ref, next_ref, s)
        x = jnp.concatenate([cur_ref[pl.ds(j, cap, stride=SUBLANES), :].astype(BF16) for j in range(SUBLANES)],
                            axis=1)
        g = _dot(x, wg_ref[0, 0].astype(BF16))
        u = _dot(x, wu_ref[0, 0].astype(BF16))
        hid_ref[0, 0] = (_silu(g) * u).astype(BF16)

    @pl.when(e % 2 == 0)
    def _():
        step(rows0_ref, rows1_ref)

    @pl.when(e % 2 == 1)
    def _():
        step(rows1_ref, rows0_ref)


def _moe_up_call(idx, src, w_gate, w_up, layer, cap):
    b, nrows, _ = src.shape
    _, e, d, f = w_gate.shape
    assert e % 2 == 0
    idx = idx.reshape(b * e, 1, cap)
    rows = pltpu.VMEM((cap * d // LANES, LANES), F32)
    wspec = pl.BlockSpec((1, 1, d, f), lambda bi, ei: (layer, ei, 0, 0))
    return pl.pallas_call(
        functools.partial(_moe_up_kernel, cap=cap),
        out_shape=jax.ShapeDtypeStruct((b, e, cap, f), BF16),
        grid=(b, e),
        in_specs=[
            pl.BlockSpec((1, 1, cap), lambda bi, ei: (bi * e + ei, 0, 0), memory_space=pltpu.SMEM),
            pl.BlockSpec((1, 1, cap), lambda bi, ei: (bi * e + jnp.minimum(ei + 1, e - 1), 0, 0),
                         memory_space=pltpu.SMEM),
            pl.BlockSpec((1, nrows, LANES), lambda bi, ei: (bi, 0, 0), pipeline_mode=pl.Buffered(1)),
            wspec, wspec,
        ],
        out_specs=pl.BlockSpec((1, 1, cap, f), lambda bi, ei: (bi, ei, 0, 0)),
        scratch_shapes=[rows, rows],
        compiler_params=_params("arbitrary", "arbitrary"),
        name="moe_up",
    )(idx, idx, src, w_gate, w_up)


SCATTER_UNROLL = 8
DOWN_SPLIT = 1


def _moe_down_kernel(idx_ref, gate_ref, hid_ref, hid_next_ref, wd_ref, wd_next_ref, acc_ref, y0_ref, y1_ref,
                     *, cap, rows):
    e = pl.program_id(1)

    def project(h_ref, w_ref, y_ref):
        y = _dot(h_ref[0, 0], w_ref[0, 0])
        for j in range(rows):
            y_ref[pl.ds(j, cap, stride=rows), :] = y[:, j * LANES:(j + 1) * LANES]

    @pl.when(e == 0)
    def _():
        acc_ref[...] = jnp.zeros_like(acc_ref)
        project(hid_ref, wd_ref, y0_ref)

    def step(cur_ref, next_ref):
        project(hid_next_ref, wd_next_ref, next_ref)
        for base in range(0, cap, SCATTER_UNROLL):
            dsts, vals = [], []
            for s in range(base, base + SCATTER_UNROLL):
                t = idx_ref[0, 0, s]
                dst = pl.ds(pl.multiple_of(t * rows, rows), rows)
                dsts.append(dst)
                vals.append(acc_ref[0, dst, :] + gate_ref[0, 0, s] * cur_ref[s * rows:(s + 1) * rows, :])
            for dst, val in zip(dsts, vals):
                acc_ref[0, dst, :] = val

    @pl.when(e % 2 == 0)
    def _():
        step(y0_ref, y1_ref)

    @pl.when(e % 2 == 1)
    def _():
        step(y1_ref, y0_ref)


def _moe_down_call(idx, gate, hid, wd, layer, n):
    b, e, cap, f = hid.shape
    d = wd.shape[3]
    assert e % 2 == 0
    dw = d // DOWN_SPLIT
    rows = dw // LANES
    nxt = lambda ei: jnp.minimum(ei + 1, e - 1)
    slot = lambda bh, ei: ((bh // DOWN_SPLIT) * e + ei, 0, 0)
    y = pltpu.VMEM((cap * rows, LANES), F32)
    out = pl.pallas_call(
        functools.partial(_moe_down_kernel, cap=cap, rows=rows),
        out_shape=jax.ShapeDtypeStruct((b * DOWN_SPLIT, n * rows, LANES), F32),
        grid=(b * DOWN_SPLIT, e),
        in_specs=[
            pl.BlockSpec((1, 1, cap), slot, memory_space=pltpu.SMEM),
            pl.BlockSpec((1, 1, cap), slot, memory_space=pltpu.SMEM),
            pl.BlockSpec((1, 1, cap, f), lambda bh, ei: (bh // DOWN_SPLIT, ei, 0, 0)),
            pl.BlockSpec((1, 1, cap, f), lambda bh, ei: (bh // DOWN_SPLIT, nxt(ei), 0, 0)),
            pl.BlockSpec((1, 1, f, dw), lambda bh, ei: (layer, ei, 0, bh % DOWN_SPLIT)),
            pl.BlockSpec((1, 1, f, dw), lambda bh, ei: (layer, nxt(ei), 0, bh % DOWN_SPLIT)),
        ],
        out_specs=pl.BlockSpec((1, n * rows, LANES), lambda bh, ei: (bh, 0, 0), pipeline_mode=pl.Buffered(1)),
        scratch_shapes=[y, y],
        compiler_params=_params("arbitrary", "arbitrary"),
        name="moe_down",
    )(idx.reshape(b * e, 1, cap), gate.reshape(b * e, 1, cap), hid, hid, wd, wd)
    return out.reshape(b, DOWN_SPLIT, n * rows, LANES)


def _moe(h2, logits_t, w_gate, w_up, wd, layer):
    b, e, n = logits_t.shape
    cap = 2 * n // e
    idx, gate = _route_call(logits_t, cap)
    hid = _moe_up_call(idx, h2, w_gate, w_up, layer, cap)
    return _moe_down_call(idx, gate, hid, wd, layer, n)


def _join(m_ref, tm):
    rows = m_ref.shape[2] // tm
    return jnp.concatenate([m_ref[0, h, pl.ds(j, tm, stride=rows), :]
                            for h in range(DOWN_SPLIT) for j in range(rows)], axis=1)


def _mix1_kernel(xp_ref, x_ref, xn_ref, mp_ref, m_ref, mn_ref, g2p_ref, sh1_ref, sc1_ref, nmix_ref, win_ref,
                 conv_ref, wout_ref, g1_ref, sh2_ref, sc2_ref, nffn_ref, rt_ref, x3_ref, h2_ref, lg_ref, *, tm):
    i = pl.program_id(1)
    last = pl.num_programs(1) - 1
    d = x_ref.shape[2]
    xs = jnp.concatenate([xp_ref[0], x_ref[0], xn_ref[0]], axis=0)
    ms = jnp.concatenate([_join(mp_ref, SUBLANES), _join(m_ref, tm), _join(mn_ref, SUBLANES)], axis=0)
    x2 = xs + g2p_ref[0] * ms
    h = _norm_mod(x2, nmix_ref[...], sh1_ref[0], sc1_ref[0])
    row = lax.broadcasted_iota(I32, (tm + 2 * SUBLANES, 1), 0)
    dead = ((row < SUBLANES) & (i == 0)) | ((row >= tm + SUBLANES) & (i == last))
    hb = jnp.where(dead, 0.0, h).astype(BF16)
    proj = _dot(hb, win_ref[...])
    lo = SUBLANES
    u = proj[:, d:2 * d] * proj[:, 2 * d:]
    cw = conv_ref[...]
    cv = u[lo - 1:lo - 1 + tm] * cw[0:1] + u[lo:lo + tm] * cw[1:2] + u[lo + 1:lo + 1 + tm] * cw[2:3]
    y = _dot((proj[lo:lo + tm, :d] * cv).astype(BF16), wout_ref[...])
    x3 = x2[lo:lo + tm] + g1_ref[0] * y
    x3_ref[0] = x3
    _ffn_prenorm(x3, nffn_ref, sh2_ref, sc2_ref, rt_ref, h2_ref, lg_ref)


def _mix1_call(x1, moe, g2p, sh1, sc1, nmix, w_in, conv_w, w_out, g1, sh2, sc2, nffn, router_t, tm):
    b, l, d = x1.shape
    tm = min(tm, l)
    nb8 = l // SUBLANES
    r = tm // SUBLANES
    e = router_t.shape[0]
    mr = d // DOWN_SPLIT // LANES
    prev = lambda bi, i: (bi, jnp.maximum(i * r - 1, 0), 0)
    nxt = lambda bi, i: (bi, jnp.minimum((i + 1) * r, nb8 - 1), 0)
    tok = pl.BlockSpec((1, tm, d), lambda bi, i: (bi, i, 0))
    vec = pl.BlockSpec((1, 1, d), lambda bi, i: (bi, 0, 0))
    return pl.pallas_call(
        functools.partial(_mix1_kernel, tm=tm),
        out_shape=[jax.ShapeDtypeStruct((b, l, d), F32), jax.ShapeDtypeStruct((b, l * d // LANES, LANES), F32),
                   jax.ShapeDtypeStruct((b, e, l), F32)],
        grid=(b, l // tm),
        in_specs=[
            pl.BlockSpec((1, SUBLANES, d), prev), tok, pl.BlockSpec((1, SUBLANES, d), nxt),
            pl.BlockSpec((1, DOWN_SPLIT, SUBLANES * mr, LANES), lambda bi, i: (bi, 0, jnp.maximum(i * r - 1, 0), 0)),
            pl.BlockSpec((1, DOWN_SPLIT, tm * mr, LANES), lambda bi, i: (bi, 0, i, 0)),
            pl.BlockSpec((1, DOWN_SPLIT, SUBLANES * mr, LANES),
                         lambda bi, i: (bi, 0, jnp.minimum((i + 1) * r, nb8 - 1), 0)),
            vec, vec, vec, _const_spec(nmix.shape), _const_spec(w_in.shape), _const_spec(conv_w.shape),
            _const_spec(w_out.shape), vec, vec, vec, _const_spec(nffn.shape), _const_spec(router_t.shape),
        ],
        out_specs=[tok, pl.BlockSpec((1, tm * d // LANES, LANES), lambda bi, i: (bi, i, 0)),
                   pl.BlockSpec((1, e, tm), lambda bi, i: (bi, 0, i))],
        compiler_params=_params("arbitrary", "arbitrary"),
        name="mix1",
    )(x1, x1, x1, moe, moe, moe, g2p, sh1, sc1, nmix, w_in, conv_w, w_out, g1, sh2, sc2, nffn, router_t)


def _final_kernel(x_ref, m_ref, g2_ref, nf_ref, o_ref):
    x = x_ref[0] + g2_ref[0] * _join(m_ref, x_ref.shape[1])
    ms = jnp.mean(x * x, axis=-1, keepdims=True)
    o_ref[0] = x * lax.rsqrt(ms + EPS) * nf_ref[...]


def _final_call(x3, moe, g2, nf, tm):
    b, l, d = x3.shape
    tm = min(tm, l)
    mr = d // DOWN_SPLIT // LANES
    tok = pl.BlockSpec((1, tm, d), lambda bi, i: (bi, i, 0))
    return pl.pallas_call(
        _final_kernel,
        out_shape=jax.ShapeDtypeStruct((b, l, d), F32),
        grid=(b, l // tm),
        in_specs=[tok, pl.BlockSpec((1, DOWN_SPLIT, tm * mr, LANES), lambda bi, i: (bi, 0, i, 0)),
                  pl.BlockSpec((1, 1, d), lambda bi, i: (bi, 0, 0)), _const_spec(nf.shape)],
        out_specs=tok,
        compiler_params=_params("arbitrary", "arbitrary"),
        name="final",
    )(x3, moe, g2, nf)


def _layer0_mixer_inputs(ev_w_in, dn_a_log, dn_dt_bias):
    w = HEADS * HEAD_DIM
    qkv_w = 3 * w
    wqkv = ev_w_in[:, :qkv_w].astype(BF16)
    wz = ev_w_in[:, qkv_w:qkv_w + w]
    wg = ev_w_in[:, qkv_w + w:qkv_w + w + 4 * HEADS]
    wp = ev_w_in[:, qkv_w + w + 4 * HEADS:]
    wrest = jnp.concatenate([wz, wp, jnp.pad(wg, ((0, 0), (0, LANES - 4 * HEADS)))], axis=1).astype(BF16)
    pad = jnp.zeros((2 * HEADS,), F32)
    tail = jnp.zeros((LANES - 4 * HEADS,), F32)
    gpar = jnp.stack([jnp.concatenate([pad, dn_a_log.reshape(-1), tail]),
                      jnp.concatenate([pad, dn_dt_bias.reshape(-1), tail])])
    return wqkv, wrest, gpar


def kernel(x, c, ctx, c_ctx, ada_w, ada_b, norm_mix, norm_ffn, norm_final, ev_w_in, dn_conv, dn_a_log, dn_dt_bias,
           dn_norm, pool_w, pool_scale, ev_w_out, sc_w_in, sc_conv, sc_w_out, router, w_gate, w_up, w_down):
    b, n, d = x.shape
    depth = ada_w.shape[0]
    span = max(PREP_CHUNKS, SCAN_CHUNKS) * CHUNK
    assert depth == 2 and b + 1 <= SUBLANES and n % GRID_W == 0 and n % span == 0 and ctx.shape[1] % span == 0

    cc = jnp.concatenate([c, c_ctx[None], jnp.zeros((SUBLANES - b - 1, d), F32)], axis=0)
    mods = _ada_call(cc, ada_w, ada_b)

    def mod(layer, k, rows=slice(0, b)):
        return mods[layer, rows, None, k * d:(k + 1) * d]

    ctx_rows = lambda layer, k: jnp.broadcast_to(mods[layer, b:b + 1, None, k * d:(k + 1) * d], (b, 1, d))
    row = lambda v: v.reshape(1, -1)
    router_t = jnp.swapaxes(router, 1, 2).astype(BF16)
    wdn = w_down.astype(BF16)

    wqkv, wrest, gpar = _layer0_mixer_inputs(ev_w_in[0], dn_a_log[0], dn_dt_bias[0])
    nmix0 = row(norm_mix[0])
    qc, kc, vc, _, _, gc = _proj0_call(ctx, ctx_rows(0, 0), ctx_rows(0, 1), nmix0, wqkv, wrest, dn_conv[0], gpar, 256)
    ql, kl, vl, zl, pl_in, gl = _proj0_call(x, mod(0, 0), mod(0, 1), nmix0, wqkv, wrest, dn_conv[0], gpar, 512)
    s0 = jnp.zeros((b, 2, HEADS, HEAD_DIM, HEAD_DIM), F32)
    _, _, s_ctx = _delta_scan_call(_delta_prep_call(qc, kc, vc, gc), s0)
    o_f, o_b, _ = _delta_scan_call(_delta_prep_call(ql, kl, vl, gl), s_ctx)
    x1, h2, lg = _out0_call(o_f, o_b, zl, pl_in, x, row(dn_norm[0]), pool_w[0].astype(BF16), row(pool_scale[0]),
                            ev_w_out[0].astype(BF16), mod(0, 2), mod(0, 3), mod(0, 4), row(norm_ffn[0]),
                            router_t[0], 256)
    moe0 = _moe(h2, lg, w_gate, w_up, wdn, 0)

    x3, h2, lg = _mix1_call(x1, moe0, mod(0, 5), mod(1, 0), mod(1, 1), row(norm_mix[1]), sc_w_in[0].astype(BF16),
                            sc_conv[0], sc_w_out[0].astype(BF16), mod(1, 2), mod(1, 3), mod(1, 4),
                            row(norm_ffn[1]), router_t[1], 256)
    moe1 = _moe(h2, lg, w_gate, w_up, wdn, 1)
    return _final_call(x3, moe1, mod(1, 5), row(norm_final), 512)
```

```python
import functools

import jax
import jax.numpy as jnp
from jax import lax
from jax.experimental import pallas as pl
from jax.experimental.pallas import tpu as pltpu

F32 = jnp.float32
BF16 = jnp.bfloat16
I32 = jnp.int32

EPS = 1e-6
GRID_W = 64
HEADS = 4
HEAD_DIM = 128
CHUNK = 64
POOL_WINDOWS = (2, 4, 8, 16)
POOL_GROUP = 128
LANES = 128
SUBLANES = 8
VMEM_LIMIT = 56 * 1024 * 1024
VALID_BIT = 1 << 30


def _silu(x):
    return x * jax.nn.sigmoid(x)


def _norm_mod(x, g, shift, scale):
    ms = jnp.mean(x * x, axis=-1, keepdims=True)
    return (x * lax.rsqrt(ms + EPS) * g) * (1.0 + scale) + shift


def _dot(a, b):
    return jnp.dot(a, b, preferred_element_type=F32)


def _dot_nt(a, b):
    return lax.dot_general(a, b, (((1,), (1,)), ((), ())), preferred_element_type=F32)


def _dot_tn(a, b):
    return lax.dot_general(a, b, (((0,), (0,)), ((), ())), preferred_element_type=F32)


def _split3(x):
    hi = x.astype(BF16)
    r = x - hi.astype(F32)
    mid = r.astype(BF16)
    lo = (r - mid.astype(F32)).astype(BF16)
    return hi, mid, lo


def _const_spec(shape):
    nd = len(shape)
    return pl.BlockSpec(shape, lambda *_: (0,) * nd, pipeline_mode=pl.Buffered(1))


def _params(*sem):
    return pltpu.CompilerParams(dimension_semantics=sem, vmem_limit_bytes=VMEM_LIMIT)


def _ada_kernel(c_ref, w_ref, b_ref, o_ref):
    s = _silu(c_ref[...])
    o_ref[0] = _dot(s.astype(BF16), w_ref[0].astype(BF16)) + b_ref[0]


def _ada_call(cc, ada_w, ada_b):
    depth, d, n6 = ada_w.shape
    tn = n6 // 4
    return pl.pallas_call(
        _ada_kernel,
        out_shape=jax.ShapeDtypeStruct((depth, SUBLANES, n6), F32),
        grid=(depth, n6 // tn),
        in_specs=[
            pl.BlockSpec((SUBLANES, d), lambda i, j: (0, 0)),
            pl.BlockSpec((1, d, tn), lambda i, j: (i, 0, j)),
            pl.BlockSpec((1, 1, tn), lambda i, j: (i, 0, j)),
        ],
        out_specs=pl.BlockSpec((1, SUBLANES, tn), lambda i, j: (i, 0, j)),
        compiler_params=_params("arbitrary", "arbitrary"),
        name="adaln",
    )(cc, ada_w, ada_b.reshape(depth, 1, n6))


def _proj0_kernel(xp_ref, x_ref, xn_ref, sh_ref, sc_ref, g_ref, wqkv_ref, wrest_ref, conv_ref, gpar_ref,
                  q_ref, k_ref, v_ref, z_ref, p_ref, gt_ref, *, tm):
    i = pl.program_id(1)
    last = pl.num_programs(1) - 1
    norm = lambda t: _norm_mod(t, g_ref[...], sh_ref[0], sc_ref[0])
    hb = jnp.concatenate([jnp.where(i == 0, 0.0, norm(xp_ref[0])), norm(x_ref[0]),
                          jnp.where(i == last, 0.0, norm(xn_ref[0]))], axis=0).astype(BF16)
    proj = _dot(hb, wqkv_ref[...])
    cw = conv_ref[...]
    lo = SUBLANES
    a = (proj[lo - 1:lo - 1 + tm] * cw[0:1] + proj[lo:lo + tm] * cw[1:2] + proj[lo + 1:lo + 1 + tm] * cw[2:3])
    a = _silu(a)
    w = HEADS * HEAD_DIM
    for hd in range(HEADS):
        sl = slice(hd * HEAD_DIM, (hd + 1) * HEAD_DIM)
        qh = a[:, sl]
        kh = a[:, w + hd * HEAD_DIM: w + (hd + 1) * HEAD_DIM]
        qn = qh * lax.rsqrt(jnp.sum(qh * qh, axis=-1, keepdims=True) + EPS) * (HEAD_DIM ** -0.5)
        q_ref[0, :, sl] = qn.astype(BF16)
        k_ref[0, :, sl] = (kh * lax.rsqrt(jnp.sum(kh * kh, axis=-1, keepdims=True) + EPS)).astype(BF16)
    v_ref[0] = a[:, 2 * w:].astype(BF16)
    rest = _dot(hb[lo:lo + tm], wrest_ref[...])
    z_ref[0] = rest[:, :w].astype(BF16)
    p_ref[0] = rest[:, w:2 * w].astype(BF16)
    gates = rest[:, 2 * w:]
    col = lax.broadcasted_iota(I32, (1, LANES), 1)
    xb = gates + gpar_ref[1:2]
    softplus = jnp.maximum(xb, 0.0) + jnp.log1p(jnp.exp(-jnp.abs(xb)))
    log_decay = -jnp.exp(gpar_ref[0:1]) * softplus
    out = jnp.where(col < 2 * HEADS, jax.nn.sigmoid(gates), log_decay)
    gt_ref[0] = out[:, :4 * HEADS]


def _proj0_call(x, shift, scale, gain, wqkv, wrest, conv_w, gpar, tm):
    b, l, d = x.shape
    tm = min(tm, l)
    nt = l // tm
    nb8 = l // SUBLANES
    r = tm // SUBLANES
    w = HEADS * HEAD_DIM
    tok = lambda width: pl.BlockSpec((1, tm, width), lambda bi, i: (bi, i, 0))
    vec = pl.BlockSpec((1, 1, d), lambda bi, i: (bi, 0, 0))
    return pl.pallas_call(
        functools.partial(_proj0_kernel, tm=tm),
        out_shape=[jax.ShapeDtypeStruct((b, l, w), BF16)] * 5 + [jax.ShapeDtypeStruct((b, l, 4 * HEADS), F32)],
        grid=(b, nt),
        in_specs=[
            pl.BlockSpec((1, SUBLANES, d), lambda bi, i: (bi, jnp.maximum(i * r - 1, 0), 0)),
            tok(d),
            pl.BlockSpec((1, SUBLANES, d), lambda bi, i: (bi, jnp.minimum((i + 1) * r, nb8 - 1), 0)),
            vec, vec,
            _const_spec((1, d)),
            _const_spec(wqkv.shape),
            _const_spec(wrest.shape),
            _const_spec(conv_w.shape),
            _const_spec(gpar.shape),
        ],
        out_specs=[tok(w)] * 5 + [tok(4 * HEADS)],
        compiler_params=_params("arbitrary", "arbitrary"),
        name="proj0",
    )(x, x, x, shift, scale, gain, wqkv, wrest, conv_w, gpar)


PREP_CHUNKS = 4
SCAN_CHUNKS = 4


def _stack_masked(x, block_of_lane, nblocks):
    return jnp.concatenate([jnp.where(block_of_lane == h, x, jnp.zeros_like(x)) for h in range(nblocks)], axis=0)


def _delta_prep_kernel(q_ref, k_ref, v_ref, g_ref, u_ref, wk_ref, qd_ref, kdt_ref, qk_ref, cd_ref):
    c = CHUNK
    wc = HEADS * c
    wd = HEADS * HEAD_DIM
    ri = lax.broadcasted_iota(I32, (c, c), 0)
    ci = lax.broadcasted_iota(I32, (c, c), 1)
    tri_l = jnp.where(ri >= ci, 1.0, 0.0).astype(BF16)
    tri_u = jnp.where(ri <= ci, 1.0, 0.0).astype(BF16)
    row = lax.broadcasted_iota(I32, (c, wc), 0)
    lane = lax.broadcasted_iota(I32, (c, wc), 1)
    pos = lane % c
    blk_c = lax.broadcasted_iota(I32, (1, wc), 1) // c
    blk_d = lax.broadcasted_iota(I32, (1, wd), 1) // HEAD_DIM
    eye = jnp.where(row == pos, 1.0, 0.0)
    incl = (row >= pos, row <= pos)
    strict = (row > pos, row < pos)
    tot = (c - 1, 0)
    chunks = range(PREP_CHUNKS)
    heads = range(HEADS)
    tok = [slice(j * c, (j + 1) * c) for j in chunks]

    def spread(cols, first, width):
        full = [jnp.broadcast_to(cols[:, first + hd:first + hd + 1], (c, LANES)) for hd in heads]
        if width == LANES:
            return jnp.concatenate(full, axis=1)
        half = lax.broadcasted_iota(I32, (c, LANES), 1) < width
        return jnp.concatenate([jnp.where(half, full[2 * i], full[2 * i + 1]) for i in range(HEADS // 2)], axis=1)

    k = [k_ref[0, tok[j], :] for j in chunks]
    q = [q_ref[0, tok[j], :] for j in chunks]
    g = [g_ref[0, tok[j], :] for j in chunks]
    kq = [_dot_nt(jnp.concatenate([k[j], q[j]], axis=0), _stack_masked(k[j], blk_d, HEADS)) for j in chunks]
    g3 = [_split3(g[j]) for j in chunks]
    cum = [(sum(_dot(tri_l, p) for p in g3[j]), sum(_dot(tri_u, p) for p in g3[j])) for j in chunks]

    chains = [(j, d) for j in chunks for d in range(2)]
    beta_d, gam_d, decay, a = {}, {}, {}, {}
    for j, d in chains:
        key = (j, d)
        gam_c = spread(cum[j][d], 2 * HEADS + d * HEADS, c)
        gam_d[key] = spread(cum[j][d], 2 * HEADS + d * HEADS, HEAD_DIM)
        beta_c = spread(g[j], d * HEADS, c)
        beta_d[key] = spread(g[j], d * HEADS, HEAD_DIM)
        gam_r = jnp.sum(jnp.where(row == pos, gam_c, 0.0), axis=0, keepdims=True)
        diff = gam_c - gam_r
        decay[key] = jnp.where(incl[d], jnp.exp(jnp.where(incl[d], diff, 0.0)), 0.0)
        a[key] = jnp.where(strict[d], kq[j][:c] * decay[key], 0.0) * beta_c
    p = dict(a)
    tinv = {key: eye - a[key] for key in chains}
    for _ in range(c.bit_length() - 2):
        pb = {key: p[key].astype(BF16) for key in chains}
        p = {key: _dot(pb[key], _stack_masked(pb[key], blk_c, HEADS)) for key in chains}
        pb = {key: p[key].astype(BF16) for key in chains}
        tinv = {key: tinv[key] + _dot(tinv[key].astype(BF16), _stack_masked(pb[key], blk_c, HEADS))
                for key in chains}
    eg, kf, u, wk = {}, {}, {}, {}
    for j, d in chains:
        key = (j, d)
        kf[key] = k[j].astype(F32)
        eg[key] = jnp.exp(gam_d[key])
        tb = tinv[key].astype(BF16)
        rhs_u = (v_ref[0, tok[j], :].astype(F32) * beta_d[key]).astype(BF16)
        rhs_w = (kf[key] * (beta_d[key] * eg[key])).astype(BF16)
        u[key] = _dot(tb, _stack_masked(rhs_u, blk_d, HEADS))
        wk[key] = _dot(tb, _stack_masked(rhs_w, blk_d, HEADS))
    for j, d in chains:
        key = (j, d)
        gtot = gam_d[key][tot[d]:tot[d] + 1, :]
        u_ref[0, d, tok[j], :] = u[key]
        wk_ref[0, d, tok[j], :] = wk[key].astype(BF16)
        qd_ref[0, d, tok[j], :] = (q[j].astype(F32) * eg[key]).astype(BF16)
        qk_ref[0, d, tok[j], :] = (kq[j][c:] * decay[key]).astype(BF16)
        kd = kf[key] * jnp.exp(gtot - gam_d[key])
        for hd in heads:
            kdt_ref[0, d, j, :, hd * c:(hd + 1) * c] = kd[:, hd * HEAD_DIM:(hd + 1) * HEAD_DIM].T.astype(BF16)
        cd = jnp.exp(gtot)
        cd_ref[0, d, j] = jnp.concatenate([cd[:, hd * HEAD_DIM:(hd + 1) * HEAD_DIM] for hd in heads]
                                          + [jnp.zeros((SUBLANES - HEADS, LANES), F32)], axis=0)


def _delta_prep_call(q, k, v, g):
    b, l, w = k.shape
    nc = l // CHUNK
    cps = PREP_CHUNKS
    t = cps * CHUNK
    tok = lambda width: pl.BlockSpec((1, t, width), lambda bi, n: (bi, n, 0))
    dtok = lambda width: pl.BlockSpec((1, 2, t, width), lambda bi, n: (bi, 0, n, 0))
    return pl.pallas_call(
        _delta_prep_kernel,
        out_shape=[
            jax.ShapeDtypeStruct((b, 2, l, w), F32),
            jax.ShapeDtypeStruct((b, 2, l, w), BF16),
            jax.ShapeDtypeStruct((b, 2, l, w), BF16),
            jax.ShapeDtypeStruct((b, 2, nc, HEAD_DIM, HEADS * CHUNK), BF16),
            jax.ShapeDtypeStruct((b, 2, l, HEADS * CHUNK), BF16),
            jax.ShapeDtypeStruct((b, 2, nc, SUBLANES, LANES), F32),
        ],
        grid=(b, nc // cps),
        in_specs=[tok(w), tok(w), tok(w), tok(4 * HEADS)],
        out_specs=[dtok(w), dtok(w), dtok(w),
                   pl.BlockSpec((1, 2, cps, HEAD_DIM, HEADS * CHUNK), lambda bi, n: (bi, 0, n, 0, 0)),
                   dtok(HEADS * CHUNK),
                   pl.BlockSpec((1, 2, cps, SUBLANES, LANES), lambda bi, n: (bi, 0, n, 0, 0))],
        compiler_params=_params("arbitrary", "arbitrary"),
        name="delta_prep",
    )(q, k, v, g)


def _delta_scan_kernel(*refs):
    ins = (refs[0:6], refs[6:12])
    s0_ref, of_ref, ob_ref, sfin_ref, s_ref = refs[12:]
    outs = (of_ref, ob_ref)
    n = pl.program_id(1)
    c = CHUNK
    cps = SCAN_CHUNKS

    @pl.when(n == 0)
    def _():
        s_ref[...] = s0_ref[0]

    combos = [(d, hd) for d in range(2) for hd in range(HEADS)]
    sl = lambda hd: slice(hd * HEAD_DIM, (hd + 1) * HEAD_DIM)
    cs = lambda hd: slice(hd * c, (hd + 1) * c)
    s = {(d, hd): s_ref[d, hd] for d, hd in combos}
    for step in range(cps):
        pos = (step, cps - 1 - step)
        tok = [slice(pos[d] * c, (pos[d] + 1) * c) for d in range(2)]
        m1, w, m2 = {}, {}, {}
        for d, hd in combos:
            u_ref, wk_ref, qd_ref, kdt_ref, qk_ref, cd_ref = ins[d]
            lhs = jnp.concatenate([wk_ref[0, 0, tok[d], sl(hd)], qd_ref[0, 0, tok[d], sl(hd)]], axis=0)
            m1[d, hd] = _dot(lhs, s[d, hd].astype(BF16))
        for d, hd in combos:
            w[d, hd] = (ins[d][0][0, 0, tok[d], sl(hd)] - m1[d, hd][:c]).astype(BF16)
        for d, hd in combos:
            u_ref, wk_ref, qd_ref, kdt_ref, qk_ref, cd_ref = ins[d]
            lhs = jnp.concatenate([qk_ref[0, 0, tok[d], cs(hd)], kdt_ref[0, 0, pos[d], :, cs(hd)]], axis=0)
            m2[d, hd] = _dot(lhs, w[d, hd])
        for d, hd in combos:
            outs[d][0, tok[d], sl(hd)] = m1[d, hd][c:] + m2[d, hd][:c]
            s[d, hd] = ins[d][5][0, 0, pos[d], hd:hd + 1, :] * s[d, hd] + m2[d, hd][c:]
    for d, hd in combos:
        s_ref[d, hd] = s[d, hd]

    @pl.when(n == pl.num_programs(1) - 1)
    def _():
        sfin_ref[0] = s_ref[...]


def _delta_scan_call(prep, s0):
    u, wk, qd, kdt, qk, cd = prep
    b, _, l, w = u.shape
    cps = SCAN_CHUNKS
    t = cps * CHUNK
    ns = l // t

    def specs(d, blk):
        tok = lambda width: pl.BlockSpec((1, 1, t, width), lambda bi, n: (bi, d, blk(n), 0))
        return [tok(w), tok(w), tok(w),
                pl.BlockSpec((1, 1, cps, HEAD_DIM, HEADS * CHUNK), lambda bi, n: (bi, d, blk(n), 0, 0)),
                tok(HEADS * CHUNK),
                pl.BlockSpec((1, 1, cps, SUBLANES, LANES), lambda bi, n: (bi, d, blk(n), 0, 0))]

    st = pl.BlockSpec((1, 2, HEADS, HEAD_DIM, HEAD_DIM), lambda bi, n: (bi, 0, 0, 0, 0))
    return pl.pallas_call(
        _delta_scan_kernel,
        out_shape=[jax.ShapeDtypeStruct((b, l, w), F32)] * 2
        + [jax.ShapeDtypeStruct((b, 2, HEADS, HEAD_DIM, HEAD_DIM), F32)],
        grid=(b, ns),
        in_specs=specs(0, lambda n: n) + specs(1, lambda n: ns - 1 - n) + [st],
        out_specs=[pl.BlockSpec((1, t, w), lambda bi, n: (bi, n, 0)),
                   pl.BlockSpec((1, t, w), lambda bi, n: (bi, ns - 1 - n, 0)), st],
        scratch_shapes=[pltpu.VMEM((2, HEADS, HEAD_DIM, HEAD_DIM), F32)],
        compiler_params=_params("arbitrary", "arbitrary"),
        name="delta_scan",
    )(*prep, *prep, s0)


def _ffn_prenorm(x, nffn_ref, sh2_ref, sc2_ref, rt_ref, h2_ref, lg_ref):
    h2 = _norm_mod(x, nffn_ref[...], sh2_ref[0], sc2_ref[0])
    tm, d = x.shape
    pieces = d // LANES
    for j in range(pieces):
        h2_ref[0, pl.ds(j, tm, stride=pieces), :] = h2[:, j * LANES:(j + 1) * LANES]
    lg_ref[0] = _dot_nt(rt_ref[...], h2.astype(BF16))


def _out0_kernel(of_ref, ob_ref, z_ref, p_ref, x_ref, band_ref, cnt_ref, onorm_ref, pw_ref, ps_ref, wout_ref,
                 g1_ref, sh2_ref, sc2_ref, nffn_ref, rt_ref, x1_ref, h2_ref, lg_ref):
    o = of_ref[0] + ob_ref[0]
    z = z_ref[0].astype(F32)
    pin = p_ref[0].astype(F32)
    parts = []
    for hd in range(HEADS):
        sl = slice(hd * HEAD_DIM, (hd + 1) * HEAD_DIM)
        oh = o[:, sl]
        ms = jnp.mean(oh * oh, axis=-1, keepdims=True)
        parts.append(oh * lax.rsqrt(ms + EPS) * onorm_ref[...] * _silu(z[:, sl]))
    for gi in range(len(POOL_WINDOWS)):
        sl = slice(gi * POOL_GROUP, (gi + 1) * POOL_GROUP)
        u = pin[:, sl]
        uh = u.astype(BF16)
        ul = (u - uh.astype(F32)).astype(BF16)
        band = band_ref[gi]
        wsum = _dot(band, uh) + _dot(band, ul)
        grp = wsum / cnt_ref[gi] - u
        parts.append(_dot(grp.astype(BF16), pw_ref[gi]) * ps_ref[:, sl])
    cat = jnp.concatenate(parts, axis=1).astype(BF16)
    x1 = x_ref[0] + g1_ref[0] * _dot(cat, wout_ref[...])
    x1_ref[0] = x1
    _ffn_prenorm(x1, nffn_ref, sh2_ref, sc2_ref, rt_ref, h2_ref, lg_ref)


def _pool_tables(tm):
    t = jnp.arange(tm)
    seg = t // GRID_W
    loc = t % GRID_W
    bands, cnts = [], []
    for w in POOL_WINDOWS:
        lo = jnp.clip(loc - w // 2, 0, GRID_W)
        hi = jnp.clip(loc + w - w // 2, 0, GRID_W)
        inside = (seg[:, None] == seg[None, :]) & (loc[None, :] >= lo[:, None]) & (loc[None, :] < hi[:, None])
        bands.append(inside.astype(BF16))
        cnts.append((hi - lo).astype(F32)[:, None])
    return jnp.stack(bands), jnp.stack(cnts)


def _out0_call(o_f, o_b, z, pin, x, onorm, pool_w, pool_scale, w_out, g1, sh2, sc2, nffn, router_t, tm):
    b, l, d = x.shape
    tm = min(tm, l)
    w = HEADS * HEAD_DIM
    e = router_t.shape[0]
    band, cnt = _pool_tables(tm)
    tok = lambda width: pl.BlockSpec((1, tm, width), lambda bi, i: (bi, i, 0))
    vec = pl.BlockSpec((1, 1, d), lambda bi, i: (bi, 0, 0))
    return pl.pallas_call(
        _out0_kernel,
        out_shape=[jax.ShapeDtypeStruct((b, l, d), F32), jax.ShapeDtypeStruct((b, l * d // LANES, LANES), F32),
                   jax.ShapeDtypeStruct((b, e, l), F32)],
        grid=(b, l // tm),
        in_specs=[tok(w), tok(w), tok(w), tok(w), tok(d),
                  _const_spec(band.shape), _const_spec(cnt.shape), _const_spec(onorm.shape),
                  _const_spec(pool_w.shape), _const_spec(pool_scale.shape), _const_spec(w_out.shape),
                  vec, vec, vec, _const_spec(nffn.shape), _const_spec(router_t.shape)],
        out_specs=[tok(d), pl.BlockSpec((1, tm * d // LANES, LANES), lambda bi, i: (bi, i, 0)),
                   pl.BlockSpec((1, e, tm), lambda bi, i: (bi, 0, i))],
        compiler_params=_params("arbitrary", "arbitrary"),
        name="out0",
    )(o_f, o_b, z, pin, x, band, cnt, onorm, pool_w, pool_scale, w_out, g1, sh2, sc2, nffn, router_t)


def _excl_cumsum_lanes(m):
    rows, n = m.shape
    ri = lax.broadcasted_iota(I32, (LANES, LANES), 0)
    ci = lax.broadcasted_iota(I32, (LANES, LANES), 1)
    upper = jnp.where(ri <= ci, 1.0, 0.0).astype(BF16)
    carry = jnp.zeros((rows, 1), F32)
    outs = []
    for blk in range(n // LANES):
        x = m[:, blk * LANES:(blk + 1) * LANES]
        inc = _dot(x.astype(BF16), upper)
        outs.append(inc - x + carry)
        carry = carry + inc[:, LANES - 1:LANES]
    return jnp.concatenate(outs, axis=1)


def _route_kernel(lg_ref, idx_ref, gate_ref, *, cap):
    x = lg_ref[0]
    e, n = x.shape
    ex = jnp.exp(x - jnp.max(x, axis=0, keepdims=True))
    aff = ex / jnp.sum(ex, axis=0, keepdims=True)

    def count_ge(bits):
        return jnp.sum(jnp.where(aff >= pltpu.bitcast(bits, F32), 1.0, 0.0), axis=1, keepdims=True)

    def bisect(_, lohi):
        lo, hi = lohi
        mid = lo + ((hi - lo + 1) >> 1)
        ok = count_ge(mid) >= cap
        return jnp.where(ok, mid, lo), jnp.where(ok, hi, mid - 1)

    lo0 = jnp.zeros((e, 1), I32)
    hi0 = jnp.full((e, 1), 0x7F800000, I32)
    thr, _ = lax.fori_loop(0, 31, bisect, (lo0, hi0))
    above = jnp.where(aff >= pltpu.bitcast(thr + 1, F32), 1.0, 0.0)
    tied = jnp.where(aff >= pltpu.bitcast(thr, F32), 1.0, 0.0) - above
    need = cap - jnp.sum(above, axis=1, keepdims=True)
    sel = above + tied * jnp.where(_excl_cumsum_lanes(tied) < need, 1.0, 0.0)
    pos = _excl_cumsum_lanes(sel)
    tok = lax.broadcasted_iota(I32, (e, n), 1)
    key = jnp.where(sel > 0.0, (tok - pos.astype(I32)) | VALID_BIT, 0)
    shift = 1
    while shift < n:
        mk = pltpu.roll(key, n - shift, axis=1)
        mt = pltpu.roll(tok, n - shift, axis=1)
        ma = pltpu.roll(aff, n - shift, axis=1)
        take = (mk & shift) != 0
        key = jnp.where(take, mk, jnp.where((key & shift) == 0, key, 0))
        tok = jnp.where(take, mt, tok)
        aff = jnp.where(take, ma, aff)
        shift *= 2
    idx_ref[0] = tok[:, :cap]
    gate_ref[0] = aff[:, :cap]


def _route_call(logits_t, cap):
    b, e, n = logits_t.shape
    return pl.pallas_call(
        functools.partial(_route_kernel, cap=cap),
        out_shape=[jax.ShapeDtypeStruct((b, e, cap), I32), jax.ShapeDtypeStruct((b, e, cap), F32)],
        grid=(b,),
        in_specs=[pl.BlockSpec((1, e, n), lambda bi: (bi, 0, 0))],
        out_specs=[pl.BlockSpec((1, e, cap), lambda bi: (bi, 0, 0))] * 2,
        compiler_params=_params("arbitrary"),
        name="route",
    )(logits_t)


GATHER_UNROLL = 8


def _gather_row(idx_ref, src_ref, dst_ref, s):
    t = idx_ref[0, 0, s]
    dst_ref[pl.ds(pl.multiple_of(s * SUBLANES, SUBLANES), SUBLANES), :] = (
        src_ref[0, pl.ds(pl.multiple_of(t * SUBLANES, SUBLANES), SUBLANES), :])


def _moe_up_kernel(idx_ref, idx_next_ref, src_ref, wg_ref, wu_ref, hid_ref, rows0_ref, rows1_ref, *, cap):
    e = pl.program_id(1)

    @pl.when(e == 0)
    def _():
        def gather(cidx, carry):
            for u in range(GATHER_UNROLL):
                _gather_row(idx_ref, src_ref, rows0_ref, cidx * GATHER_UNROLL + u)
            return carry
        lax.fori_loop(0, cap // GATHER_UNROLL, gather, 0)

    def step(cur_ref, next_ref):
        for s in range(cap):
            _gather_row(idx_next_ref, src_ref, next_ref, s)
        x = jnp.concatenate([cur_ref[pl.ds(j, cap, stride=SUBLANES), :].astype(BF16) for j in range(SUBLANES)],
                            axis=1)
        g = _dot(x, wg_ref[0, 0].astype(BF16))
        u = _dot(x, wu_ref[0, 0].astype(BF16))
        hid_ref[0, 0] = (_silu(g) * u).astype(BF16)

    @pl.when(e % 2 == 0)
    def _():
        step(rows0_ref, rows1_ref)

    @pl.when(e % 2 == 1)
    def _():
        step(rows1_ref, rows0_ref)


def _moe_up_call(idx, src, w_gate, w_up, layer, cap):
    b, nrows, _ = src.shape
    _, e, d, f = w_gate.shape
    assert e % 2 == 0
    idx = idx.reshape(b * e, 1, cap)
    rows = pltpu.VMEM((cap * d // LANES, LANES), F32)
    wspec = pl.BlockSpec((1, 1, d, f), lambda bi, ei: (layer, ei, 0, 0))
    return pl.pallas_call(
        functools.partial(_moe_up_kernel, cap=cap),
        out_shape=jax.ShapeDtypeStruct((b, e, cap, f), BF16),
        grid=(b, e),
        in_specs=[
            pl.BlockSpec((1, 1, cap), lambda bi, ei: (bi * e + ei, 0, 0), memory_space=pltpu.SMEM),
            pl.BlockSpec((1, 1, cap), lambda bi, ei: (bi * e + jnp.minimum(ei + 1, e - 1), 0, 0),
                         memory_space=pltpu.SMEM),
            pl.BlockSpec((1, nrows, LANES), lambda bi, ei: (bi, 0, 0), pipeline_mode=pl.Buffered(1)),
            wspec, wspec,
        ],
        out_specs=pl.BlockSpec((1, 1, cap, f), lambda bi, ei: (bi, ei, 0, 0)),
        scratch_shapes=[rows, rows],
        compiler_params=_params("arbitrary", "arbitrary"),
        name="moe_up",
    )(idx, idx, src, w_gate, w_up)


SCATTER_UNROLL = 8
DOWN_SPLIT = 1


def _moe_down_kernel(idx_ref, gate_ref, hid_ref, hid_next_ref, wd_ref, wd_next_ref, acc_ref, y0_ref, y1_ref,
                     *, cap, rows):
    e = pl.program_id(1)

    def project(h_ref, w_ref, y_ref):
        y = _dot(h_ref[0, 0], w_ref[0, 0])
        for j in range(rows):
            y_ref[pl.ds(j, cap, stride=rows), :] = y[:, j * LANES:(j + 1) * LANES]

    @pl.when(e == 0)
    def _():
        acc_ref[...] = jnp.zeros_like(acc_ref)
        project(hid_ref, wd_ref, y0_ref)

    def step(cur_ref, next_ref):
        project(hid_next_ref, wd_next_ref, next_ref)
        for base in range(0, cap, SCATTER_UNROLL):
            dsts, vals = [], []
            for s in range(base, base + SCATTER_UNROLL):
                t = idx_ref[0, 0, s]
                dst = pl.ds(pl.multiple_of(t * rows, rows), rows)
                dsts.append(dst)
                vals.append(acc_ref[0, dst, :] + gate_ref[0, 0, s] * cur_ref[s * rows:(s + 1) * rows, :])
            for dst, val in zip(dsts, vals):
                acc_ref[0, dst, :] = val

    @pl.when(e % 2 == 0)
    def _():
        step(y0_ref, y1_ref)

    @pl.when(e % 2 == 1)
    def _():
        step(y1_ref, y0_ref)


def _moe_down_call(idx, gate, hid, wd, layer, n):
    b, e, cap, f = hid.shape
    d = wd.shape[3]
    assert e % 2 == 0
    dw = d // DOWN_SPLIT
    rows = dw // LANES
    nxt = lambda ei: jnp.minimum(ei + 1, e - 1)
    slot = lambda bh, ei: ((bh // DOWN_SPLIT) * e + ei, 0, 0)
    y = pltpu.VMEM((cap * rows, LANES), F32)
    out = pl.pallas_call(
        functools.partial(_moe_down_kernel, cap=cap, rows=rows),
        out_shape=jax.ShapeDtypeStruct((b * DOWN_SPLIT, n * rows, LANES), F32),
        grid=(b * DOWN_SPLIT, e),
        in_specs=[
            pl.BlockSpec((1, 1, cap), slot, memory_space=pltpu.SMEM),
            pl.BlockSpec((1, 1, cap), slot, memory_space=pltpu.SMEM),
            pl.BlockSpec((1, 1, cap, f), lambda bh, ei: (bh // DOWN_SPLIT, ei, 0, 0)),
            pl.BlockSpec((1, 1, cap, f), lambda bh, ei: (bh // DOWN_SPLIT, nxt(ei), 0, 0)),
            pl.BlockSpec((1, 1, f, dw), lambda bh, ei: (layer, ei, 0, bh % DOWN_SPLIT)),
            pl.BlockSpec((1, 1, f, dw), lambda bh, ei: (layer, nxt(ei), 0, bh % DOWN_SPLIT)),
        ],
        out_specs=pl.BlockSpec((1, n * rows, LANES), lambda bh, ei: (bh, 0, 0), pipeline_mode=pl.Buffered(1)),
        scratch_shapes=[y, y],
        compiler_params=_params("arbitrary", "arbitrary"),
        name="moe_down",
    )(idx.reshape(b * e, 1, cap), gate.reshape(b * e, 1, cap), hid, hid, wd, wd)
    return out.reshape(b, DOWN_SPLIT, n * rows, LANES)


def _moe(h2, logits_t, w_gate, w_up, wd, layer):
    b, e, n = logits_t.shape
    cap = 2 * n // e
    idx, gate = _route_call(logits_t, cap)
    hid = _moe_up_call(idx, h2, w_gate, w_up, layer, cap)
    return _moe_down_call(idx, gate, hid, wd, layer, n)


def _join(m_ref, tm):
    rows = m_ref.shape[2] // tm
    return jnp.concatenate([m_ref[0, h, pl.ds(j, tm, stride=rows), :]
                            for h in range(DOWN_SPLIT) for j in range(rows)], axis=1)


def _mix1_kernel(xp_ref, x_ref, xn_ref, mp_ref, m_ref, mn_ref, g2p_ref, sh1_ref, sc1_ref, nmix_ref, win_ref,
                 conv_ref, wout_ref, g1_ref, sh2_ref, sc2_ref, nffn_ref, rt_ref, x3_ref, h2_ref, lg_ref, *, tm):
    i = pl.program_id(1)
    last = pl.num_programs(1) - 1
    d = x_ref.shape[2]
    x2 = x_ref[0] + g2p_ref[0] * _join(m_ref, tm)
    x2p = xp_ref[0] + g2p_ref[0] * _join(mp_ref, SUBLANES)
    x2n = xn_ref[0] + g2p_ref[0] * _join(mn_ref, SUBLANES)
    norm = lambda t: _norm_mod(t, nmix_ref[...], sh1_ref[0], sc1_ref[0])
    hb = jnp.concatenate([jnp.where(i == 0, 0.0, norm(x2p)), norm(x2), jnp.where(i == last, 0.0, norm(x2n))],
                         axis=0).astype(BF16)
    proj = _dot(hb, win_ref[...])
    lo = SUBLANES
    u = proj[:, d:2 * d] * proj[:, 2 * d:]
    cw = conv_ref[...]
    cv = u[lo - 1:lo - 1 + tm] * cw[0:1] + u[lo:lo + tm] * cw[1:2] + u[lo + 1:lo + 1 + tm] * cw[2:3]
    y = _dot((proj[lo:lo + tm, :d] * cv).astype(BF16), wout_ref[...])
    x3 = x2 + g1_ref[0] * y
    x3_ref[0] = x3
    _ffn_prenorm(x3, nffn_ref, sh2_ref, sc2_ref, rt_ref, h2_ref, lg_ref)


def _mix1_call(x1, moe, g2p, sh1, sc1, nmix, w_in, conv_w, w_out, g1, sh2, sc2, nffn, router_t, tm):
    b, l, d = x1.shape
    tm = min(tm, l)
    nb8 = l // SUBLANES
    r = tm // SUBLANES
    e = router_t.shape[0]
    mr = d // DOWN_SPLIT // LANES
    prev = lambda bi, i: (bi, jnp.maximum(i * r - 1, 0), 0)
    nxt = lambda bi, i: (bi, jnp.minimum((i + 1) * r, nb8 - 1), 0)
    tok = pl.BlockSpec((1, tm, d), lambda bi, i: (bi, i, 0))
    vec = pl.BlockSpec((1, 1, d), lambda bi, i: (bi, 0, 0))
    return pl.pallas_call(
        functools.partial(_mix1_kernel, tm=tm),
        out_shape=[jax.ShapeDtypeStruct((b, l, d), F32), jax.ShapeDtypeStruct((b, l * d // LANES, LANES), F32),
                   jax.ShapeDtypeStruct((b, e, l), F32)],
        grid=(b, l // tm),
        in_specs=[
            pl.BlockSpec((1, SUBLANES, d), prev), tok, pl.BlockSpec((1, SUBLANES, d), nxt),
            pl.BlockSpec((1, DOWN_SPLIT, SUBLANES * mr, LANES), lambda bi, i: (bi, 0, jnp.maximum(i * r - 1, 0), 0)),
            pl.BlockSpec((1, DOWN_SPLIT, tm * mr, LANES), lambda bi, i: (bi, 0, i, 0)),
            pl.BlockSpec((1, DOWN_SPLIT, SUBLANES * mr, LANES),
                         lambda bi, i: (bi, 0, jnp.minimum((i + 1) * r, nb8 - 1), 0)),
            vec, vec, vec, _const_spec(nmix.shape), _const_spec(w_in.shape), _const_spec(conv_w.shape),
            _const_spec(w_out.shape), vec, vec, vec, _const_spec(nffn.shape), _const_spec(router_t.shape),
        ],
        out_specs=[tok, pl.BlockSpec((1, tm * d // LANES, LANES), lambda bi, i: (bi, i, 0)),
                   pl.BlockSpec((1, e, tm), lambda bi, i: (bi, 0, i))],
        compiler_params=_params("arbitrary", "arbitrary"),
        name="mix1",
    )(x1, x1, x1, moe, moe, moe, g2p, sh1, sc1, nmix, w_in, conv_w, w_out, g1, sh2, sc2, nffn, router_t)


def _final_kernel(x_ref, m_ref, g2_ref, nf_ref, o_ref):
    x = x_ref[0] + g2_ref[0] * _join(m_ref, x_ref.shape[1])
    ms = jnp.mean(x * x, axis=-1, keepdims=True)
    o_ref[0] = x * lax.rsqrt(ms + EPS) * nf_ref[...]


def _final_call(x3, moe, g2, nf, tm):
    b, l, d = x3.shape
    tm = min(tm, l)
    mr = d // DOWN_SPLIT // LANES
    tok = pl.BlockSpec((1, tm, d), lambda bi, i: (bi, i, 0))
    return pl.pallas_call(
        _final_kernel,
        out_shape=jax.ShapeDtypeStruct((b, l, d), F32),
        grid=(b, l // tm),
        in_specs=[tok, pl.BlockSpec((1, DOWN_SPLIT, tm * mr, LANES), lambda bi, i: (bi, 0, i, 0)),
                  pl.BlockSpec((1, 1, d), lambda bi, i: (bi, 0, 0)), _const_spec(nf.shape)],
        out_specs=tok,
        compiler_params=_params("arbitrary", "arbitrary"),
        name="final",
    )(x3, moe, g2, nf)


def _layer0_mixer_inputs(ev_w_in, dn_a_log, dn_dt_bias):
    w = HEADS * HEAD_DIM
    qkv_w = 3 * w
    wqkv = ev_w_in[:, :qkv_w].astype(BF16)
    wz = ev_w_in[:, qkv_w:qkv_w + w]
    wg = ev_w_in[:, qkv_w + w:qkv_w + w + 4 * HEADS]
    wp = ev_w_in[:, qkv_w + w + 4 * HEADS:]
    wrest = jnp.concatenate([wz, wp, jnp.pad(wg, ((0, 0), (0, LANES - 4 * HEADS)))], axis=1).astype(BF16)
    pad = jnp.zeros((2 * HEADS,), F32)
    tail = jnp.zeros((LANES - 4 * HEADS,), F32)
    gpar = jnp.stack([jnp.concatenate([pad, dn_a_log.reshape(-1), tail]),
                      jnp.concatenate([pad, dn_dt_bias.reshape(-1), tail])])
    return wqkv, wrest, gpar


def kernel(x, c, ctx, c_ctx, ada_w, ada_b, norm_mix, norm_ffn, norm_final, ev_w_in, dn_conv, dn_a_log, dn_dt_bias,
           dn_norm, pool_w, pool_scale, ev_w_out, sc_w_in, sc_conv, sc_w_out, router, w_gate, w_up, w_down):
    b, n, d = x.shape
    depth = ada_w.shape[0]
    span = max(PREP_CHUNKS, SCAN_CHUNKS) * CHUNK
    assert depth == 2 and b + 1 <= SUBLANES and n % GRID_W == 0 and n % span == 0 and ctx.shape[1] % span == 0

    cc = jnp.concatenate([c, c_ctx[None], jnp.zeros((SUBLANES - b - 1, d), F32)], axis=0)
    mods = _ada_call(cc, ada_w, ada_b)

    def mod(layer, k, rows=slice(0, b)):
        return mods[layer, rows, None, k * d:(k + 1) * d]

    ctx_rows = lambda layer, k: jnp.broadcast_to(mods[layer, b:b + 1, None, k * d:(k + 1) * d], (b, 1, d))
    row = lambda v: v.reshape(1, -1)
    router_t = jnp.swapaxes(router, 1, 2).astype(BF16)
    wdn = w_down.astype(BF16)

    wqkv, wrest, gpar = _layer0_mixer_inputs(ev_w_in[0], dn_a_log[0], dn_dt_bias[0])
    nmix0 = row(norm_mix[0])
    qc, kc, vc, _, _, gc = _proj0_call(ctx, ctx_rows(0, 0), ctx_rows(0, 1), nmix0, wqkv, wrest, dn_conv[0], gpar, 256)
    ql, kl, vl, zl, pl_in, gl = _proj0_call(x, mod(0, 0), mod(0, 1), nmix0, wqkv, wrest, dn_conv[0], gpar, 512)
    s0 = jnp.zeros((b, 2, HEADS, HEAD_DIM, HEAD_DIM), F32)
    _, _, s_ctx = _delta_scan_call(_delta_prep_call(qc, kc, vc, gc), s0)
    o_f, o_b, _ = _delta_scan_call(_delta_prep_call(ql, kl, vl, gl), s_ctx)
    x1, h2, lg = _out0_call(o_f, o_b, zl, pl_in, x, row(dn_norm[0]), pool_w[0].astype(BF16), row(pool_scale[0]),
                            ev_w_out[0].astype(BF16), mod(0, 2), mod(0, 3), mod(0, 4), row(norm_ffn[0]),
                            router_t[0], 256)
    moe0 = _moe(h2, lg, w_gate, w_up, wdn, 0)

    x3, h2, lg = _mix1_call(x1, moe0, mod(0, 5), mod(1, 0), mod(1, 1), row(norm_mix[1]), sc_w_in[0].astype(BF16),
                            sc_conv[0], sc_w_out[0].astype(BF16), mod(1, 2), mod(1, 3), mod(1, 4),
                            row(norm_ffn[1]), router_t[1], 512)
    moe1 = _moe(h2, lg, w_gate, w_up, wdn, 1)
    return _final_call(x3, moe1, mod(1, 5), row(norm_final), 1024)
```

```python
import functools
import math

import jax
import jax.numpy as jnp
from jax import lax
from jax.experimental import pallas as pl
from jax.experimental.pallas import tpu as pltpu

F32 = jnp.float32
BF16 = jnp.bfloat16
I32 = jnp.int32

EPS = 1e-6
GRID_W = 64
HEADS = 4
HEAD_DIM = 128
CHUNK = 64
POOL_WINDOWS = (2, 4, 8, 16)
POOL_GROUP = 128
LANES = 128
SUBLANES = 8
VMEM_LIMIT = 56 * 1024 * 1024
VALID_BIT = 1 << 30


def _silu(x):
    return x * jax.nn.sigmoid(x)


def _norm_mod(x, g, shift, scale):
    ms = jnp.mean(x * x, axis=-1, keepdims=True)
    return (x * lax.rsqrt(ms + EPS) * g) * (1.0 + scale) + shift


def _dot(a, b):
    return jnp.dot(a, b, preferred_element_type=F32)


def _dot_nt(a, b):
    return lax.dot_general(a, b, (((1,), (1,)), ((), ())), preferred_element_type=F32)


def _dot_tn(a, b):
    return lax.dot_general(a, b, (((0,), (0,)), ((), ())), preferred_element_type=F32)


def _split3(x):
    hi = x.astype(BF16)
    r = x - hi.astype(F32)
    mid = r.astype(BF16)
    lo = (r - mid.astype(F32)).astype(BF16)
    return hi, mid, lo


def _from_token_tiles(t, rows):
    p = t.shape[0] // rows
    chunks = jnp.swapaxes(t.reshape(rows, p, LANES), 0, 1)
    return jnp.concatenate([chunks[j] for j in range(p)], axis=1)


def _const_spec(shape):
    nd = len(shape)
    return pl.BlockSpec(shape, lambda *_: (0,) * nd, pipeline_mode=pl.Buffered(1))


def _params(*sem):
    return pltpu.CompilerParams(dimension_semantics=sem, vmem_limit_bytes=VMEM_LIMIT)


def _ada_kernel(c_ref, w_ref, b_ref, o_ref):
    s = _silu(c_ref[...])
    o_ref[0] = _dot(s.astype(BF16), w_ref[0].astype(BF16)) + b_ref[0]


def _ada_call(cc, ada_w, ada_b):
    depth, d, n6 = ada_w.shape
    tn = n6 // 4
    return pl.pallas_call(
        _ada_kernel,
        out_shape=jax.ShapeDtypeStruct((depth, SUBLANES, n6), F32),
        grid=(depth, n6 // tn),
        in_specs=[
            pl.BlockSpec((SUBLANES, d), lambda i, j: (0, 0)),
            pl.BlockSpec((1, d, tn), lambda i, j: (i, 0, j)),
            pl.BlockSpec((1, 1, tn), lambda i, j: (i, 0, j)),
        ],
        out_specs=pl.BlockSpec((1, SUBLANES, tn), lambda i, j: (i, 0, j)),
        compiler_params=_params("arbitrary", "arbitrary"),
        name="adaln",
    )(cc, ada_w, ada_b.reshape(depth, 1, n6))


def _proj0_kernel(xp_ref, x_ref, xn_ref, sh_ref, sc_ref, g_ref, wqkv_ref, wrest_ref, conv_ref, gpar_ref,
                  q_ref, k_ref, v_ref, z_ref, p_ref, gt_ref, *, tm):
    i = pl.program_id(1)
    last = pl.num_programs(1) - 1
    norm = lambda t: _norm_mod(t, g_ref[...], sh_ref[0], sc_ref[0])
    hb = jnp.concatenate([jnp.where(i == 0, 0.0, norm(xp_ref[0])), norm(x_ref[0]),
                          jnp.where(i == last, 0.0, norm(xn_ref[0]))], axis=0).astype(BF16)
    proj = _dot(hb, wqkv_ref[...])
    cw = conv_ref[...]
    lo = SUBLANES
    a = (proj[lo - 1:lo - 1 + tm] * cw[0:1] + proj[lo:lo + tm] * cw[1:2] + proj[lo + 1:lo + 1 + tm] * cw[2:3])
    a = _silu(a)
    w = HEADS * HEAD_DIM
    for hd in range(HEADS):
        sl = slice(hd * HEAD_DIM, (hd + 1) * HEAD_DIM)
        qh = a[:, sl]
        kh = a[:, w + hd * HEAD_DIM: w + (hd + 1) * HEAD_DIM]
        qn = qh * lax.rsqrt(jnp.sum(qh * qh, axis=-1, keepdims=True) + EPS) * (HEAD_DIM ** -0.5)
        q_ref[0, :, sl] = qn.astype(BF16)
        k_ref[0, :, sl] = (kh * lax.rsqrt(jnp.sum(kh * kh, axis=-1, keepdims=True) + EPS)).astype(BF16)
    v_ref[0] = a[:, 2 * w:].astype(BF16)
    rest = _dot(hb[lo:lo + tm], wrest_ref[...])
    z_ref[0] = rest[:, :w].astype(BF16)
    p_ref[0] = rest[:, w:2 * w].astype(BF16)
    gates = rest[:, 2 * w:]
    col = lax.broadcasted_iota(I32, (1, LANES), 1)
    xb = gates + gpar_ref[1:2]
    softplus = jnp.maximum(xb, 0.0) + jnp.log1p(jnp.exp(-jnp.abs(xb)))
    log_decay = -jnp.exp(gpar_ref[0:1]) * softplus
    out = jnp.where(col < 2 * HEADS, jax.nn.sigmoid(gates), log_decay)
    gt_ref[0] = out[:, :4 * HEADS]


def _proj0_call(x, shift, scale, gain, wqkv, wrest, conv_w, gpar, tm):
    b, l, d = x.shape
    tm = min(tm, l)
    nt = l // tm
    nb8 = l // SUBLANES
    r = tm // SUBLANES
    w = HEADS * HEAD_DIM
    tok = lambda width: pl.BlockSpec((1, tm, width), lambda bi, i: (bi, i, 0))
    vec = pl.BlockSpec((1, 1, d), lambda bi, i: (bi, 0, 0))
    return pl.pallas_call(
        functools.partial(_proj0_kernel, tm=tm),
        out_shape=[jax.ShapeDtypeStruct((b, l, w), BF16)] * 5 + [jax.ShapeDtypeStruct((b, l, 4 * HEADS), F32)],
        grid=(b, nt),
        in_specs=[
            pl.BlockSpec((1, SUBLANES, d), lambda bi, i: (bi, jnp.maximum(i * r - 1, 0), 0)),
            tok(d),
            pl.BlockSpec((1, SUBLANES, d), lambda bi, i: (bi, jnp.minimum((i + 1) * r, nb8 - 1), 0)),
            vec, vec,
            _const_spec((1, d)),
            _const_spec(wqkv.shape),
            _const_spec(wrest.shape),
            _const_spec(conv_w.shape),
            _const_spec(gpar.shape),
        ],
        out_specs=[tok(w)] * 5 + [tok(4 * HEADS)],
        compiler_params=_params("arbitrary", "arbitrary"),
        name="proj0",
    )(x, x, x, shift, scale, gain, wqkv, wrest, conv_w, gpar)


PREP_CHUNKS = 4
SCAN_CHUNKS = 8


def _stack_masked(x, block_of_lane, nblocks):
    return jnp.concatenate([jnp.where(block_of_lane == h, x, jnp.zeros_like(x)) for h in range(nblocks)], axis=0)


def _delta_prep_kernel(q_ref, k_ref, v_ref, g_ref, u_ref, wk_ref, qd_ref, kdt_ref, qk_ref, cd_ref):
    c = CHUNK
    wc = HEADS * c
    wd = HEADS * HEAD_DIM
    ri = lax.broadcasted_iota(I32, (c, c), 0)
    ci = lax.broadcasted_iota(I32, (c, c), 1)
    tri_l = jnp.where(ri >= ci, 1.0, 0.0).astype(BF16)
    tri_u = jnp.where(ri <= ci, 1.0, 0.0).astype(BF16)
    row = lax.broadcasted_iota(I32, (c, wc), 0)
    lane = lax.broadcasted_iota(I32, (c, wc), 1)
    pos = lane % c
    blk_c = lax.broadcasted_iota(I32, (1, wc), 1) // c
    blk_d = lax.broadcasted_iota(I32, (1, wd), 1) // HEAD_DIM
    eye = jnp.where(row == pos, 1.0, 0.0)
    incl = (row >= pos, row <= pos)
    strict = (row > pos, row < pos)
    tot = (c - 1, 0)
    chunks = range(PREP_CHUNKS)
    heads = range(HEADS)
    tok = [slice(j * c, (j + 1) * c) for j in chunks]

    def spread(cols, first, width):
        full = [jnp.broadcast_to(cols[:, first + hd:first + hd + 1], (c, LANES)) for hd in heads]
        if width == LANES:
            return jnp.concatenate(full, axis=1)
        half = lax.broadcasted_iota(I32, (c, LANES), 1) < width
        return jnp.concatenate([jnp.where(half, full[2 * i], full[2 * i + 1]) for i in range(HEADS // 2)], axis=1)

    k = [k_ref[0, tok[j], :] for j in chunks]
    q = [q_ref[0, tok[j], :] for j in chunks]
    g = [g_ref[0, tok[j], :] for j in chunks]
    kq = [_dot_nt(jnp.concatenate([k[j], q[j]], axis=0), _stack_masked(k[j], blk_d, HEADS)) for j in chunks]
    g3 = [_split3(g[j]) for j in chunks]
    cum = [(sum(_dot(tri_l, p) for p in g3[j]), sum(_dot(tri_u, p) for p in g3[j])) for j in chunks]

    chains = [(j, d) for j in chunks for d in range(2)]
    beta_d, gam_d, decay, a = {}, {}, {}, {}
    for j, d in chains:
        key = (j, d)
        gam_c = spread(cum[j][d], 2 * HEADS + d * HEADS, c)
        gam_d[key] = spread(cum[j][d], 2 * HEADS + d * HEADS, HEAD_DIM)
        beta_c = spread(g[j], d * HEADS, c)
        beta_d[key] = spread(g[j], d * HEADS, HEAD_DIM)
        gam_r = jnp.sum(jnp.where(row == pos, gam_c, 0.0), axis=0, keepdims=True)
        diff = gam_c - gam_r
        decay[key] = jnp.where(incl[d], jnp.exp(jnp.where(incl[d], diff, 0.0)), 0.0)
        a[key] = jnp.where(strict[d], kq[j][:c] * decay[key], 0.0) * beta_c
    p = dict(a)
    tinv = {key: eye - a[key] for key in chains}
    for _ in range(c.bit_length() - 2):
        pb = {key: p[key].astype(BF16) for key in chains}
        p = {key: _dot(pb[key], _stack_masked(pb[key], blk_c, HEADS)) for key in chains}
        pb = {key: p[key].astype(BF16) for key in chains}
        tinv = {key: tinv[key] + _dot(tinv[key].astype(BF16), _stack_masked(pb[key], blk_c, HEADS))
                for key in chains}
    eg, kf, u, wk = {}, {}, {}, {}
    for j, d in chains:
        key = (j, d)
        kf[key] = k[j].astype(F32)
        eg[key] = jnp.exp(gam_d[key])
        tb = tinv[key].astype(BF16)
        rhs_u = (v_ref[0, tok[j], :].astype(F32) * beta_d[key]).astype(BF16)
        rhs_w = (kf[key] * (beta_d[key] * eg[key])).astype(BF16)
        u[key] = _dot(tb, _stack_masked(rhs_u, blk_d, HEADS))
        wk[key] = _dot(tb, _stack_masked(rhs_w, blk_d, HEADS))
    for j, d in chains:
        key = (j, d)
        gtot = gam_d[key][tot[d]:tot[d] + 1, :]
        u_ref[0, d, tok[j], :] = u[key]
        wk_ref[0, d, tok[j], :] = wk[key].astype(BF16)
        qd_ref[0, d, tok[j], :] = (q[j].astype(F32) * eg[key]).astype(BF16)
        qk_ref[0, d, tok[j], :] = (kq[j][c:] * decay[key]).astype(BF16)
        kd = kf[key] * jnp.exp(gtot - gam_d[key])
        for hd in heads:
            kdt_ref[0, d, j, :, hd * c:(hd + 1) * c] = kd[:, hd * HEAD_DIM:(hd + 1) * HEAD_DIM].T.astype(BF16)
        cd = jnp.exp(gtot)
        cd_ref[0, d, j] = jnp.concatenate([cd[:, hd * HEAD_DIM:(hd + 1) * HEAD_DIM] for hd in heads]
                                          + [jnp.zeros((SUBLANES - HEADS, LANES), F32)], axis=0)


def _delta_prep_call(q, k, v, g):
    b, l, w = k.shape
    nc = l // CHUNK
    cps = PREP_CHUNKS
    t = cps * CHUNK
    tok = lambda width: pl.BlockSpec((1, t, width), lambda bi, n: (bi, n, 0))
    dtok = lambda width: pl.BlockSpec((1, 2, t, width), lambda bi, n: (bi, 0, n, 0))
    return pl.pallas_call(
        _delta_prep_kernel,
        out_shape=[
            jax.ShapeDtypeStruct((b, 2, l, w), F32),
            jax.ShapeDtypeStruct((b, 2, l, w), BF16),
            jax.ShapeDtypeStruct((b, 2, l, w), BF16),
            jax.ShapeDtypeStruct((b, 2, nc, HEAD_DIM, HEADS * CHUNK), BF16),
            jax.ShapeDtypeStruct((b, 2, l, HEADS * CHUNK), BF16),
            jax.ShapeDtypeStruct((b, 2, nc, SUBLANES, LANES), F32),
        ],
        grid=(b, nc // cps),
        in_specs=[tok(w), tok(w), tok(w), tok(4 * HEADS)],
        out_specs=[dtok(w), dtok(w), dtok(w),
                   pl.BlockSpec((1, 2, cps, HEAD_DIM, HEADS * CHUNK), lambda bi, n: (bi, 0, n, 0, 0)),
                   dtok(HEADS * CHUNK),
                   pl.BlockSpec((1, 2, cps, SUBLANES, LANES), lambda bi, n: (bi, 0, n, 0, 0))],
        compiler_params=_params("arbitrary", "arbitrary"),
        name="delta_prep",
    )(q, k, v, g)


def _delta_scan_kernel(*refs, cps):
    ins = (refs[0:6], refs[6:12])
    s0_ref, of_ref, ob_ref, sfin_ref, s_ref = refs[12:]
    outs = (of_ref, ob_ref)
    n = pl.program_id(1)
    c = CHUNK

    @pl.when(n == 0)
    def _():
        s_ref[...] = s0_ref[0]

    combos = [(d, hd) for d in range(2) for hd in range(HEADS)]
    sl = lambda hd: slice(hd * HEAD_DIM, (hd + 1) * HEAD_DIM)
    cs = lambda hd: slice(hd * c, (hd + 1) * c)
    s = {(d, hd): s_ref[d, hd] for d, hd in combos}
    for step in range(cps):
        pos = (step, cps - 1 - step)
        tok = [slice(pos[d] * c, (pos[d] + 1) * c) for d in range(2)]
        m1, w, m2 = {}, {}, {}
        for d, hd in combos:
            u_ref, wk_ref, qd_ref, kdt_ref, qk_ref, cd_ref = ins[d]
            lhs = jnp.concatenate([wk_ref[0, 0, tok[d], sl(hd)], qd_ref[0, 0, tok[d], sl(hd)]], axis=0)
            m1[d, hd] = _dot(lhs, s[d, hd].astype(BF16))
        for d, hd in combos:
            w[d, hd] = (ins[d][0][0, 0, tok[d], sl(hd)] - m1[d, hd][:c]).astype(BF16)
        for d, hd in combos:
            u_ref, wk_ref, qd_ref, kdt_ref, qk_ref, cd_ref = ins[d]
            lhs = jnp.concatenate([qk_ref[0, 0, tok[d], cs(hd)], kdt_ref[0, 0, pos[d], :, cs(hd)]], axis=0)
            m2[d, hd] = _dot(lhs, w[d, hd])
        for d, hd in combos:
            outs[d][0, tok[d], sl(hd)] = m1[d, hd][c:] + m2[d, hd][:c]
            s[d, hd] = ins[d][5][0, 0, pos[d], hd:hd + 1, :] * s[d, hd] + m2[d, hd][c:]
    for d, hd in combos:
        s_ref[d, hd] = s[d, hd]

    @pl.when(n == pl.num_programs(1) - 1)
    def _():
        sfin_ref[0] = s_ref[...]


def _delta_scan_call(prep, s0):
    u, wk, qd, kdt, qk, cd = prep
    b, _, l, w = u.shape
    cps = math.gcd(SCAN_CHUNKS, l // CHUNK)
    t = cps * CHUNK
    ns = l // t

    def specs(d, blk):
        tok = lambda width: pl.BlockSpec((1, 1, t, width), lambda bi, n: (bi, d, blk(n), 0))
        return [tok(w), tok(w), tok(w),
                pl.BlockSpec((1, 1, cps, HEAD_DIM, HEADS * CHUNK), lambda bi, n: (bi, d, blk(n), 0, 0)),
                tok(HEADS * CHUNK),
                pl.BlockSpec((1, 1, cps, SUBLANES, LANES), lambda bi, n: (bi, d, blk(n), 0, 0))]

    st = pl.BlockSpec((1, 2, HEADS, HEAD_DIM, HEAD_DIM), lambda bi, n: (bi, 0, 0, 0, 0))
    return pl.pallas_call(
        functools.partial(_delta_scan_kernel, cps=cps),
        out_shape=[jax.ShapeDtypeStruct((b, l, w), F32)] * 2
        + [jax.ShapeDtypeStruct((b, 2, HEADS, HEAD_DIM, HEAD_DIM), F32)],
        grid=(b, ns),
        in_specs=specs(0, lambda n: n) + specs(1, lambda n: ns - 1 - n) + [st],
        out_specs=[pl.BlockSpec((1, t, w), lambda bi, n: (bi, n, 0)),
                   pl.BlockSpec((1, t, w), lambda bi, n: (bi, ns - 1 - n, 0)), st],
        scratch_shapes=[pltpu.VMEM((2, HEADS, HEAD_DIM, HEAD_DIM), F32)],
        compiler_params=_params("arbitrary", "arbitrary"),
        name="delta_scan",
    )(*prep, *prep, s0)


def _ffn_prenorm(x, nffn_ref, sh2_ref, sc2_ref, rt_ref, h2_ref, lg_ref):
    h2 = _norm_mod(x, nffn_ref[...], sh2_ref[0], sc2_ref[0])
    tm, d = x.shape
    pieces = d // LANES
    for j in range(pieces):
        h2_ref[0, pl.ds(j, tm, stride=pieces), :] = h2[:, j * LANES:(j + 1) * LANES]
    lg_ref[0] = _dot_nt(rt_ref[...], h2.astype(BF16))


def _out0_kernel(of_ref, ob_ref, z_ref, p_ref, x_ref, band_ref, cnt_ref, onorm_ref, pw_ref, ps_ref, wout_ref,
                 g1_ref, sh2_ref, sc2_ref, nffn_ref, rt_ref, x1_ref, h2_ref, lg_ref):
    o = of_ref[0] + ob_ref[0]
    z = z_ref[0].astype(F32)
    pin = p_ref[0].astype(F32)
    parts = []
    for hd in range(HEADS):
        sl = slice(hd * HEAD_DIM, (hd + 1) * HEAD_DIM)
        oh = o[:, sl]
        ms = jnp.mean(oh * oh, axis=-1, keepdims=True)
        parts.append(oh * lax.rsqrt(ms + EPS) * onorm_ref[...] * _silu(z[:, sl]))
    for gi in range(len(POOL_WINDOWS)):
        sl = slice(gi * POOL_GROUP, (gi + 1) * POOL_GROUP)
        u = pin[:, sl]
        uh = u.astype(BF16)
        ul = (u - uh.astype(F32)).astype(BF16)
        band = band_ref[gi]
        wsum = _dot(band, uh) + _dot(band, ul)
        grp = wsum / cnt_ref[gi] - u
        parts.append(_dot(grp.astype(BF16), pw_ref[gi]) * ps_ref[:, sl])
    cat = jnp.concatenate(parts, axis=1).astype(BF16)
    x1 = x_ref[0] + g1_ref[0] * _dot(cat, wout_ref[...])
    x1_ref[0] = x1
    _ffn_prenorm(x1, nffn_ref, sh2_ref, sc2_ref, rt_ref, h2_ref, lg_ref)


def _pool_tables(tm):
    t = jnp.arange(tm)
    seg = t // GRID_W
    loc = t % GRID_W
    bands, cnts = [], []
    for w in POOL_WINDOWS:
        lo = jnp.clip(loc - w // 2, 0, GRID_W)
        hi = jnp.clip(loc + w - w // 2, 0, GRID_W)
        inside = (seg[:, None] == seg[None, :]) & (loc[None, :] >= lo[:, None]) & (loc[None, :] < hi[:, None])
        bands.append(inside.astype(BF16))
        cnts.append((hi - lo).astype(F32)[:, None])
    return jnp.stack(bands), jnp.stack(cnts)


def _out0_call(o_f, o_b, z, pin, x, onorm, pool_w, pool_scale, w_out, g1, sh2, sc2, nffn, router_t, tm):
    b, l, d = x.shape
    tm = min(tm, l)
    w = HEADS * HEAD_DIM
    e = router_t.shape[0]
    band, cnt = _pool_tables(tm)
    tok = lambda width: pl.BlockSpec((1, tm, width), lambda bi, i: (bi, i, 0))
    vec = pl.BlockSpec((1, 1, d), lambda bi, i: (bi, 0, 0))
    return pl.pallas_call(
        _out0_kernel,
        out_shape=[jax.ShapeDtypeStruct((b, l, d), F32), jax.ShapeDtypeStruct((b, l * d // LANES, LANES), F32),
                   jax.ShapeDtypeStruct((b, e, l), F32)],
        grid=(b, l // tm),
        in_specs=[tok(w), tok(w), tok(w), tok(w), tok(d),
                  _const_spec(band.shape), _const_spec(cnt.shape), _const_spec(onorm.shape),
                  _const_spec(pool_w.shape), _const_spec(pool_scale.shape), _const_spec(w_out.shape),
                  vec, vec, vec, _const_spec(nffn.shape), _const_spec(router_t.shape)],
        out_specs=[tok(d), pl.BlockSpec((1, tm * d // LANES, LANES), lambda bi, i: (bi, i, 0)),
                   pl.BlockSpec((1, e, tm), lambda bi, i: (bi, 0, i))],
        compiler_params=_params("arbitrary", "arbitrary"),
        name="out0",
    )(o_f, o_b, z, pin, x, band, cnt, onorm, pool_w, pool_scale, w_out, g1, sh2, sc2, nffn, router_t)


def _excl_cumsum_lanes(m):
    rows, n = m.shape
    ri = lax.broadcasted_iota(I32, (LANES, LANES), 0)
    ci = lax.broadcasted_iota(I32, (LANES, LANES), 1)
    upper = jnp.where(ri <= ci, 1.0, 0.0).astype(BF16)
    carry = jnp.zeros((rows, 1), F32)
    outs = []
    for blk in range(n // LANES):
        x = m[:, blk * LANES:(blk + 1) * LANES]
        inc = _dot(x.astype(BF16), upper)
        outs.append(inc - x + carry)
        carry = carry + inc[:, LANES - 1:LANES]
    return jnp.concatenate(outs, axis=1)


def _route_kernel(lg_ref, idx_ref, gate_ref, *, cap):
    x = lg_ref[0]
    e, n = x.shape
    ex = jnp.exp(x - jnp.max(x, axis=0, keepdims=True))
    aff = ex / jnp.sum(ex, axis=0, keepdims=True)

    def count_ge(bits):
        return jnp.sum(jnp.where(aff >= pltpu.bitcast(bits, F32), 1.0, 0.0), axis=1, keepdims=True)

    def bisect(_, lohi):
        lo, hi = lohi
        mid = lo + ((hi - lo + 1) >> 1)
        ok = count_ge(mid) >= cap
        return jnp.where(ok, mid, lo), jnp.where(ok, hi, mid - 1)

    lo0 = jnp.zeros((e, 1), I32)
    hi0 = jnp.full((e, 1), 0x7F800000, I32)
    thr, _ = lax.fori_loop(0, 31, bisect, (lo0, hi0))
    above = jnp.where(aff >= pltpu.bitcast(thr + 1, F32), 1.0, 0.0)
    tied = jnp.where(aff >= pltpu.bitcast(thr, F32), 1.0, 0.0) - above
    need = cap - jnp.sum(above, axis=1, keepdims=True)
    sel = above + tied * jnp.where(_excl_cumsum_lanes(tied) < need, 1.0, 0.0)
    pos = _excl_cumsum_lanes(sel)
    tok = lax.broadcasted_iota(I32, (e, n), 1)
    key = jnp.where(sel > 0.0, (tok - pos.astype(I32)) | VALID_BIT, 0)
    shift = 1
    while shift < n:
        mk = pltpu.roll(key, n - shift, axis=1)
        mt = pltpu.roll(tok, n - shift, axis=1)
        ma = pltpu.roll(aff, n - shift, axis=1)
        take = (mk & shift) != 0
        key = jnp.where(take, mk, jnp.where((key & shift) == 0, key, 0))
        tok = jnp.where(take, mt, tok)
        aff = jnp.where(take, ma, aff)
        shift *= 2
    idx_ref[0] = tok[:, :cap]
    gate_ref[0] = aff[:, :cap]


def _route_call(logits_t, cap):
    b, e, n = logits_t.shape
    return pl.pallas_call(
        functools.partial(_route_kernel, cap=cap),
        out_shape=[jax.ShapeDtypeStruct((b, e, cap), I32), jax.ShapeDtypeStruct((b, e, cap), F32)],
        grid=(b,),
        in_specs=[pl.BlockSpec((1, e, n), lambda bi: (bi, 0, 0))],
        out_specs=[pl.BlockSpec((1, e, cap), lambda bi: (bi, 0, 0))] * 2,
        compiler_params=_params("arbitrary"),
        name="route",
    )(logits_t)


GATHER_UNROLL = 8


def _gather_row(idx_ref, src_ref, dst_ref, s):
    t = idx_ref[0, 0, s]
    dst_ref[pl.ds(pl.multiple_of(s * SUBLANES, SUBLANES), SUBLANES), :] = (
        src_ref[0, pl.ds(pl.multiple_of(t * SUBLANES, SUBLANES), SUBLANES), :])


def _moe_up_kernel(idx_ref, idx_next_ref, src_ref, wg_ref, wu_ref, hid_ref, rows0_ref, rows1_ref, *, cap):
    e = pl.program_id(1)

    @pl.when(e == 0)
    def _():
        def gather(cidx, carry):
            for u in range(GATHER_UNROLL):
                _gather_row(idx_ref, src_ref, rows0_ref, cidx * GATHER_UNROLL + u)
            return carry
        lax.fori_loop(0, cap // GATHER_UNROLL, gather, 0)

    def step(cur_ref, next_ref):
        for s in range(cap):
            _gather_row(idx_next_ref, src_ref, next_ref, s)
        x = _from_token_tiles(cur_ref[...], cap).astype(BF16)
        g = _dot(x, wg_ref[0, 0].astype(BF16))
        u = _dot(x, wu_ref[0, 0].astype(BF16))
        hid_ref[0, 0] = (_silu(g) * u).astype(BF16)

    @pl.when(e % 2 == 0)
    def _():
        step(rows0_ref, rows1_ref)

    @pl.when(e % 2 == 1)
    def _():
        step(rows1_ref, rows0_ref)


def _moe_up_call(idx, src, w_gate, w_up, layer, cap):
    b, nrows, _ = src.shape
    _, e, d, f = w_gate.shape
    assert e % 2 == 0
    idx = idx.reshape(b * e, 1, cap)
    rows = pltpu.VMEM((cap * d // LANES, LANES), F32)
    wspec = pl.BlockSpec((1, 1, d, f), lambda bi, ei: (layer, ei, 0, 0))
    return pl.pallas_call(
        functools.partial(_moe_up_kernel, cap=cap),
        out_shape=jax.ShapeDtypeStruct((b, e, cap, f), BF16),
        grid=(b, e),
        in_specs=[
            pl.BlockSpec((1, 1, cap), lambda bi, ei: (bi * e + ei, 0, 0), memory_space=pltpu.SMEM),
            pl.BlockSpec((1, 1, cap), lambda bi, ei: (bi * e + jnp.minimum(ei + 1, e - 1), 0, 0),
                         memory_space=pltpu.SMEM),
            pl.BlockSpec((1, nrows, LANES), lambda bi, ei: (bi, 0, 0), pipeline_mode=pl.Buffered(1)),
            wspec, wspec,
        ],
        out_specs=pl.BlockSpec((1, 1, cap, f), lambda bi, ei: (bi, ei, 0, 0)),
        scratch_shapes=[rows, rows],
        compiler_params=_params("arbitrary", "arbitrary"),
        name="moe_up",
    )(idx, idx, src, w_gate, w_up)


SCATTER_UNROLL = 8
DOWN_SPLIT = 1


def _moe_down_kernel(idx_ref, gate_ref, hid_ref, hid_next_ref, wd_ref, wd_next_ref, acc_ref, y0_ref, y1_ref,
                     *, cap, rows):
    e = pl.program_id(1)

    def project(h_ref, w_ref, y_ref):
        y = _dot(h_ref[0, 0], w_ref[0, 0].astype(BF16))
        for j in range(rows):
            y_ref[pl.ds(j, cap, stride=rows), :] = y[:, j * LANES:(j + 1) * LANES]

    @pl.when(e == 0)
    def _():
        acc_ref[...] = jnp.zeros_like(acc_ref)
        project(hid_ref, wd_ref, y0_ref)

    def step(cur_ref, next_ref):
        project(hid_next_ref, wd_next_ref, next_ref)
        for base in range(0, cap, SCATTER_UNROLL):
            dsts, vals = [], []
            for s in range(base, base + SCATTER_UNROLL):
                t = idx_ref[0, 0, s]
                dst = pl.ds(pl.multiple_of(t * rows, rows), rows)
                dsts.append(dst)
                vals.append(acc_ref[0, dst, :] + gate_ref[0, 0, s] * cur_ref[s * rows:(s + 1) * rows, :])
            for dst, val in zip(dsts, vals):
                acc_ref[0, dst, :] = val

    @pl.when(e % 2 == 0)
    def _():
        step(y0_ref, y1_ref)

    @pl.when(e % 2 == 1)
    def _():
        step(y1_ref, y0_ref)


def _moe_down_call(idx, gate, hid, wd, layer, n):
    b, e, cap, f = hid.shape
    d = wd.shape[3]
    assert e % 2 == 0
    dw = d // DOWN_SPLIT
    rows = dw // LANES
    nxt = lambda ei: jnp.minimum(ei + 1, e - 1)
    slot = lambda bh, ei: ((bh // DOWN_SPLIT) * e + ei, 0, 0)
    y = pltpu.VMEM((cap * rows, LANES), F32)
    out = pl.pallas_call(
        functools.partial(_moe_down_kernel, cap=cap, rows=rows),
        out_shape=jax.ShapeDtypeStruct((b * DOWN_SPLIT, n * rows, LANES), F32),
        grid=(b * DOWN_SPLIT, e),
        in_specs=[
            pl.BlockSpec((1, 1, cap), slot, memory_space=pltpu.SMEM),
            pl.BlockSpec((1, 1, cap), slot, memory_space=pltpu.SMEM),
            pl.BlockSpec((1, 1, cap, f), lambda bh, ei: (bh // DOWN_SPLIT, ei, 0, 0)),
            pl.BlockSpec((1, 1, cap, f), lambda bh, ei: (bh // DOWN_SPLIT, nxt(ei), 0, 0)),
            pl.BlockSpec((1, 1, f, dw), lambda bh, ei: (layer, ei, 0, bh % DOWN_SPLIT)),
            pl.BlockSpec((1, 1, f, dw), lambda bh, ei: (layer, nxt(ei), 0, bh % DOWN_SPLIT)),
        ],
        out_specs=pl.BlockSpec((1, n * rows, LANES), lambda bh, ei: (bh, 0, 0), pipeline_mode=pl.Buffered(1)),
        scratch_shapes=[y, y],
        compiler_params=_params("arbitrary", "arbitrary"),
        name="moe_down",
    )(idx.reshape(b * e, 1, cap), gate.reshape(b * e, 1, cap), hid, hid, wd, wd)
    return out.reshape(b, DOWN_SPLIT, n * rows, LANES)


def _moe(h2, logits_t, w_gate, w_up, wd, layer):
    b, e, n = logits_t.shape
    cap = 2 * n // e
    idx, gate = _route_call(logits_t, cap)
    hid = _moe_up_call(idx, h2, w_gate, w_up, layer, cap)
    return _moe_down_call(idx, gate, hid, wd, layer, n)


def _join(m_ref, tm):
    rows = m_ref.shape[2] // tm
    return jnp.concatenate([m_ref[0, h, pl.ds(j, tm, stride=rows), :]
                            for h in range(DOWN_SPLIT) for j in range(rows)], axis=1)


def _mix1_kernel(xp_ref, x_ref, xn_ref, mp_ref, m_ref, mn_ref, g2p_ref, sh1_ref, sc1_ref, nmix_ref, win_ref,
                 conv_ref, wout_ref, g1_ref, sh2_ref, sc2_ref, nffn_ref, rt_ref, x3_ref, h2_ref, lg_ref, *, tm):
    i = pl.program_id(1)
    last = pl.num_programs(1) - 1
    d = x_ref.shape[2]
    x2 = x_ref[0] + g2p_ref[0] * _join(m_ref, tm)
    x2p = xp_ref[0] + g2p_ref[0] * _join(mp_ref, SUBLANES)
    x2n = xn_ref[0] + g2p_ref[0] * _join(mn_ref, SUBLANES)
    norm = lambda t: _norm_mod(t, nmix_ref[...], sh1_ref[0], sc1_ref[0])
    hb = jnp.concatenate([jnp.where(i == 0, 0.0, norm(x2p)), norm(x2), jnp.where(i == last, 0.0, norm(x2n))],
                         axis=0).astype(BF16)
    proj = _dot(hb, win_ref[...])
    lo = SUBLANES
    u = proj[:, d:2 * d] * proj[:, 2 * d:]
    cw = conv_ref[...]
    cv = u[lo - 1:lo - 1 + tm] * cw[0:1] + u[lo:lo + tm] * cw[1:2] + u[lo + 1:lo + 1 + tm] * cw[2:3]
    y = _dot((proj[lo:lo + tm, :d] * cv).astype(BF16), wout_ref[...])
    x3 = x2 + g1_ref[0] * y
    x3_ref[0] = x3
    _ffn_prenorm(x3, nffn_ref, sh2_ref, sc2_ref, rt_ref, h2_ref, lg_ref)


def _mix1_call(x1, moe, g2p, sh1, sc1, nmix, w_in, conv_w, w_out, g1, sh2, sc2, nffn, router_t, tm):
    b, l, d = x1.shape
    tm = min(tm, l)
    nb8 = l // SUBLANES
    r = tm // SUBLANES
    e = router_t.shape[0]
    mr = d // DOWN_SPLIT // LANES
    prev = lambda bi, i: (bi, jnp.maximum(i * r - 1, 0), 0)
    nxt = lambda bi, i: (bi, jnp.minimum((i + 1) * r, nb8 - 1), 0)
    tok = pl.BlockSpec((1, tm, d), lambda bi, i: (bi, i, 0))
    vec = pl.BlockSpec((1, 1, d), lambda bi, i: (bi, 0, 0))
    return pl.pallas_call(
        functools.partial(_mix1_kernel, tm=tm),
        out_shape=[jax.ShapeDtypeStruct((b, l, d), F32), jax.ShapeDtypeStruct((b, l * d // LANES, LANES), F32),
                   jax.ShapeDtypeStruct((b, e, l), F32)],
        grid=(b, l // tm),
        in_specs=[
            pl.BlockSpec((1, SUBLANES, d), prev), tok, pl.BlockSpec((1, SUBLANES, d), nxt),
            pl.BlockSpec((1, DOWN_SPLIT, SUBLANES * mr, LANES), lambda bi, i: (bi, 0, jnp.maximum(i * r - 1, 0), 0)),
            pl.BlockSpec((1, DOWN_SPLIT, tm * mr, LANES), lambda bi, i: (bi, 0, i, 0)),
            pl.BlockSpec((1, DOWN_SPLIT, SUBLANES * mr, LANES),
                         lambda bi, i: (bi, 0, jnp.minimum((i + 1) * r, nb8 - 1), 0)),
            vec, vec, vec, _const_spec(nmix.shape), _const_spec(w_in.shape), _const_spec(conv_w.shape),
            _const_spec(w_out.shape), vec, vec, vec, _const_spec(nffn.shape), _const_spec(router_t.shape),
        ],
        out_specs=[tok, pl.BlockSpec((1, tm * d // LANES, LANES), lambda bi, i: (bi, i, 0)),
                   pl.BlockSpec((1, e, tm), lambda bi, i: (bi, 0, i))],
        compiler_params=_params("arbitrary", "arbitrary"),
        name="mix1",
    )(x1, x1, x1, moe, moe, moe, g2p, sh1, sc1, nmix, w_in, conv_w, w_out, g1, sh2, sc2, nffn, router_t)


def _final_kernel(x_ref, m_ref, g2_ref, nf_ref, o_ref):
    x = x_ref[0] + g2_ref[0] * _join(m_ref, x_ref.shape[1])
    ms = jnp.mean(x * x, axis=-1, keepdims=True)
    o_ref[0] = x * lax.rsqrt(ms + EPS) * nf_ref[...]


def _final_call(x3, moe, g2, nf, tm):
    b, l, d = x3.shape
    tm = min(tm, l)
    mr = d // DOWN_SPLIT // LANES
    tok = pl.BlockSpec((1, tm, d), lambda bi, i: (bi, i, 0))
    return pl.pallas_call(
        _final_kernel,
        out_shape=jax.ShapeDtypeStruct((b, l, d), F32),
        grid=(b, l // tm),
        in_specs=[tok, pl.BlockSpec((1, DOWN_SPLIT, tm * mr, LANES), lambda bi, i: (bi, 0, i, 0)),
                  pl.BlockSpec((1, 1, d), lambda bi, i: (bi, 0, 0)), _const_spec(nf.shape)],
        out_specs=tok,
        compiler_params=_params("arbitrary", "arbitrary"),
        name="final",
    )(x3, moe, g2, nf)


def _layer0_mixer_inputs(ev_w_in, dn_a_log, dn_dt_bias):
    w = HEADS * HEAD_DIM
    qkv_w = 3 * w
    wqkv = ev_w_in[:, :qkv_w].astype(BF16)
    wz = ev_w_in[:, qkv_w:qkv_w + w]
    wg = ev_w_in[:, qkv_w + w:qkv_w + w + 4 * HEADS]
    wp = ev_w_in[:, qkv_w + w + 4 * HEADS:]
    wrest = jnp.concatenate([wz, wp, jnp.pad(wg, ((0, 0), (0, LANES - 4 * HEADS)))], axis=1).astype(BF16)
    pad = jnp.zeros((2 * HEADS,), F32)
    tail = jnp.zeros((LANES - 4 * HEADS,), F32)
    gpar = jnp.stack([jnp.concatenate([pad, dn_a_log.reshape(-1), tail]),
                      jnp.concatenate([pad, dn_dt_bias.reshape(-1), tail])])
    return wqkv, wrest, gpar


def kernel(x, c, ctx, c_ctx, ada_w, ada_b, norm_mix, norm_ffn, norm_final, ev_w_in, dn_conv, dn_a_log, dn_dt_bias,
           dn_norm, pool_w, pool_scale, ev_w_out, sc_w_in, sc_conv, sc_w_out, router, w_gate, w_up, w_down):
    b, n, d = x.shape
    depth = ada_w.shape[0]
    span = PREP_CHUNKS * CHUNK
    assert depth == 2 and b + 1 <= SUBLANES and n % GRID_W == 0 and n % span == 0 and ctx.shape[1] % span == 0

    cc = jnp.concatenate([c, c_ctx[None], jnp.zeros((SUBLANES - b - 1, d), F32)], axis=0)
    mods = _ada_call(cc, ada_w, ada_b)

    def mod(layer, k, rows=slice(0, b)):
        return mods[layer, rows, None, k * d:(k + 1) * d]

    ctx_rows = lambda layer, k: jnp.broadcast_to(mods[layer, b:b + 1, None, k * d:(k + 1) * d], (b, 1, d))
    row = lambda v: v.reshape(1, -1)
    router_t = jnp.swapaxes(router, 1, 2).astype(BF16)

    wqkv, wrest, gpar = _layer0_mixer_inputs(ev_w_in[0], dn_a_log[0], dn_dt_bias[0])
    nmix0 = row(norm_mix[0])
    qc, kc, vc, _, _, gc = _proj0_call(ctx, ctx_rows(0, 0), ctx_rows(0, 1), nmix0, wqkv, wrest, dn_conv[0], gpar, 256)
    ql, kl, vl, zl, pl_in, gl = _proj0_call(x, mod(0, 0), mod(0, 1), nmix0, wqkv, wrest, dn_conv[0], gpar, 512)
    s0 = jnp.zeros((b, 2, HEADS, HEAD_DIM, HEAD_DIM), F32)
    _, _, s_ctx = _delta_scan_call(_delta_prep_call(qc, kc, vc, gc), s0)
    o_f, o_b, _ = _delta_scan_call(_delta_prep_call(ql, kl, vl, gl), s_ctx)
    x1, h2, lg = _out0_call(o_f, o_b, zl, pl_in, x, row(dn_norm[0]), pool_w[0].astype(BF16), row(pool_scale[0]),
                            ev_w_out[0].astype(BF16), mod(0, 2), mod(0, 3), mod(0, 4), row(norm_ffn[0]),
                            router_t[0], 256)
    moe0 = _moe(h2, lg, w_gate, w_up, w_down, 0)

    x3, h2, lg = _mix1_call(x1, moe0, mod(0, 5), mod(1, 0), mod(1, 1), row(norm_mix[1]), sc_w_in[0].astype(BF16),
                            sc_conv[0], sc_w_out[0].astype(BF16), mod(1, 2), mod(1, 3), mod(1, 4),
                            row(norm_ffn[1]), router_t[1], 512)
    moe1 = _moe(h2, lg, w_gate, w_up, w_down, 1)
    return _final_call(x3, moe1, mod(1, 5), row(norm_final), 1024)
```

```python
import functools
import math

import jax
import jax.numpy as jnp
from jax import lax
from jax.experimental import pallas as pl
from jax.experimental.pallas import tpu as pltpu

F32 = jnp.float32
BF16 = jnp.bfloat16
I32 = jnp.int32

EPS = 1e-6
GRID_W = 64
HEADS = 4
HEAD_DIM = 128
CHUNK = 64
POOL_WINDOWS = (2, 4, 8, 16)
POOL_GROUP = 128
LANES = 128
SUBLANES = 8
VMEM_LIMIT = 56 * 1024 * 1024
VALID_BIT = 1 << 30


def _silu(x):
    return x * jax.nn.sigmoid(x)


def _norm_mod(x, g, shift, scale):
    ms = jnp.mean(x * x, axis=-1, keepdims=True)
    return (x * lax.rsqrt(ms + EPS) * g) * (1.0 + scale) + shift


def _dot(a, b):
    return jnp.dot(a, b, preferred_element_type=F32)


def _dot_nt(a, b):
    return lax.dot_general(a, b, (((1,), (1,)), ((), ())), preferred_element_type=F32)


def _dot_tn(a, b):
    return lax.dot_general(a, b, (((0,), (0,)), ((), ())), preferred_element_type=F32)


def _split3(x):
    hi = x.astype(BF16)
    r = x - hi.astype(F32)
    mid = r.astype(BF16)
    lo = (r - mid.astype(F32)).astype(BF16)
    return hi, mid, lo


def _from_token_tiles(t, rows):
    p = t.shape[0] // rows
    chunks = jnp.swapaxes(t.reshape(rows, p, LANES), 0, 1)
    return jnp.concatenate([chunks[j] for j in range(p)], axis=1)


def _const_spec(shape):
    nd = len(shape)
    return pl.BlockSpec(shape, lambda *_: (0,) * nd, pipeline_mode=pl.Buffered(1))


def _params(*sem):
    return pltpu.CompilerParams(dimension_semantics=sem, vmem_limit_bytes=VMEM_LIMIT)


def _ada_kernel(c_ref, w_ref, b_ref, o_ref):
    s = _silu(c_ref[...])
    o_ref[0] = _dot(s.astype(BF16), w_ref[0].astype(BF16)) + b_ref[0]


def _ada_call(cc, ada_w, ada_b):
    depth, d, n6 = ada_w.shape
    tn = n6 // 4
    return pl.pallas_call(
        _ada_kernel,
        out_shape=jax.ShapeDtypeStruct((depth, SUBLANES, n6), F32),
        grid=(depth, n6 // tn),
        in_specs=[
            pl.BlockSpec((SUBLANES, d), lambda i, j: (0, 0)),
            pl.BlockSpec((1, d, tn), lambda i, j: (i, 0, j)),
            pl.BlockSpec((1, 1, tn), lambda i, j: (i, 0, j)),
        ],
        out_specs=pl.BlockSpec((1, SUBLANES, tn), lambda i, j: (i, 0, j)),
        compiler_params=_params("arbitrary", "arbitrary"),
        name="adaln",
    )(cc, ada_w, ada_b.reshape(depth, 1, n6))


def _proj0_kernel(xp_ref, x_ref, xn_ref, sh_ref, sc_ref, g_ref, wqkv_ref, wrest_ref, conv_ref, gpar_ref,
                  q_ref, k_ref, v_ref, z_ref, p_ref, gt_ref, *, tm):
    i = pl.program_id(1)
    last = pl.num_programs(1) - 1
    norm = lambda t: _norm_mod(t, g_ref[...], sh_ref[0], sc_ref[0])
    hb = jnp.concatenate([jnp.where(i == 0, 0.0, norm(xp_ref[0])), norm(x_ref[0]),
                          jnp.where(i == last, 0.0, norm(xn_ref[0]))], axis=0).astype(BF16)
    proj = _dot(hb, wqkv_ref[...])
    cw = conv_ref[...]
    lo = SUBLANES
    a = (proj[lo - 1:lo - 1 + tm] * cw[0:1] + proj[lo:lo + tm] * cw[1:2] + proj[lo + 1:lo + 1 + tm] * cw[2:3])
    a = _silu(a)
    w = HEADS * HEAD_DIM
    for hd in range(HEADS):
        sl = slice(hd * HEAD_DIM, (hd + 1) * HEAD_DIM)
        qh = a[:, sl]
        kh = a[:, w + hd * HEAD_DIM: w + (hd + 1) * HEAD_DIM]
        qn = qh * lax.rsqrt(jnp.sum(qh * qh, axis=-1, keepdims=True) + EPS) * (HEAD_DIM ** -0.5)
        q_ref[0, :, sl] = qn.astype(BF16)
        k_ref[0, :, sl] = (kh * lax.rsqrt(jnp.sum(kh * kh, axis=-1, keepdims=True) + EPS)).astype(BF16)
    v_ref[0] = a[:, 2 * w:].astype(BF16)
    rest = _dot(hb[lo:lo + tm], wrest_ref[...])
    z_ref[0] = rest[:, :w].astype(BF16)
    p_ref[0] = rest[:, w:2 * w].astype(BF16)
    gates = rest[:, 2 * w:]
    col = lax.broadcasted_iota(I32, (1, LANES), 1)
    xb = gates + gpar_ref[1:2]
    softplus = jnp.maximum(xb, 0.0) + jnp.log1p(jnp.exp(-jnp.abs(xb)))
    log_decay = -jnp.exp(gpar_ref[0:1]) * softplus
    out = jnp.where(col < 2 * HEADS, jax.nn.sigmoid(gates), log_decay)
    gt_ref[0] = out[:, :4 * HEADS]


def _proj0_call(x, shift, scale, gain, wqkv, wrest, conv_w, gpar, tm):
    b, l, d = x.shape
    tm = min(tm, l)
    nt = l // tm
    nb8 = l // SUBLANES
    r = tm // SUBLANES
    w = HEADS * HEAD_DIM
    tok = lambda width: pl.BlockSpec((1, tm, width), lambda bi, i: (bi, i, 0))
    vec = pl.BlockSpec((1, 1, d), lambda bi, i: (bi, 0, 0))
    return pl.pallas_call(
        functools.partial(_proj0_kernel, tm=tm),
        out_shape=[jax.ShapeDtypeStruct((b, l, w), BF16)] * 5 + [jax.ShapeDtypeStruct((b, l, 4 * HEADS), F32)],
        grid=(b, nt),
        in_specs=[
            pl.BlockSpec((1, SUBLANES, d), lambda bi, i: (bi, jnp.maximum(i * r - 1, 0), 0)),
            tok(d),
            pl.BlockSpec((1, SUBLANES, d), lambda bi, i: (bi, jnp.minimum((i + 1) * r, nb8 - 1), 0)),
            vec, vec,
            _const_spec((1, d)),
            _const_spec(wqkv.shape),
            _const_spec(wrest.shape),
            _const_spec(conv_w.shape),
            _const_spec(gpar.shape),
        ],
        out_specs=[tok(w)] * 5 + [tok(4 * HEADS)],
        compiler_params=_params("arbitrary", "arbitrary"),
        name="proj0",
    )(x, x, x, shift, scale, gain, wqkv, wrest, conv_w, gpar)


PREP_CHUNKS = 8
SCAN_CHUNKS = 8


def _stack_masked(x, block_of_lane, nblocks):
    return jnp.concatenate([jnp.where(block_of_lane == h, x, jnp.zeros_like(x)) for h in range(nblocks)], axis=0)


def _delta_prep_kernel(q_ref, k_ref, v_ref, g_ref, u_ref, wk_ref, qd_ref, kdt_ref, qk_ref, cd_ref, *, cps):
    c = CHUNK
    wc = HEADS * c
    wd = HEADS * HEAD_DIM
    ri = lax.broadcasted_iota(I32, (c, c), 0)
    ci = lax.broadcasted_iota(I32, (c, c), 1)
    tri_l = jnp.where(ri >= ci, 1.0, 0.0).astype(BF16)
    tri_u = jnp.where(ri <= ci, 1.0, 0.0).astype(BF16)
    row = lax.broadcasted_iota(I32, (c, wc), 0)
    lane = lax.broadcasted_iota(I32, (c, wc), 1)
    pos = lane % c
    blk_c = lax.broadcasted_iota(I32, (1, wc), 1) // c
    blk_d = lax.broadcasted_iota(I32, (1, wd), 1) // HEAD_DIM
    eye = jnp.where(row == pos, 1.0, 0.0)
    incl = (row >= pos, row <= pos)
    strict = (row > pos, row < pos)
    tot = (c - 1, 0)
    chunks = range(cps)
    heads = range(HEADS)
    tok = [slice(j * c, (j + 1) * c) for j in chunks]

    def spread(cols, first, width):
        full = [jnp.broadcast_to(cols[:, first + hd:first + hd + 1], (c, LANES)) for hd in heads]
        if width == LANES:
            return jnp.concatenate(full, axis=1)
        half = lax.broadcasted_iota(I32, (c, LANES), 1) < width
        return jnp.concatenate([jnp.where(half, full[2 * i], full[2 * i + 1]) for i in range(HEADS // 2)], axis=1)

    k = [k_ref[0, tok[j], :] for j in chunks]
    q = [q_ref[0, tok[j], :] for j in chunks]
    g = [g_ref[0, tok[j], :] for j in chunks]
    kq = [_dot_nt(jnp.concatenate([k[j], q[j]], axis=0), _stack_masked(k[j], blk_d, HEADS)) for j in chunks]
    g3 = [_split3(g[j]) for j in chunks]
    cum = [(sum(_dot(tri_l, p) for p in g3[j]), sum(_dot(tri_u, p) for p in g3[j])) for j in chunks]

    chains = [(j, d) for j in chunks for d in range(2)]
    beta_d, gam_d, decay, a = {}, {}, {}, {}
    for j, d in chains:
        key = (j, d)
        gam_c = spread(cum[j][d], 2 * HEADS + d * HEADS, c)
        gam_d[key] = spread(cum[j][d], 2 * HEADS + d * HEADS, HEAD_DIM)
        beta_c = spread(g[j], d * HEADS, c)
        beta_d[key] = spread(g[j], d * HEADS, HEAD_DIM)
        gam_r = jnp.sum(jnp.where(row == pos, gam_c, 0.0), axis=0, keepdims=True)
        diff = gam_c - gam_r
        decay[key] = jnp.where(incl[d], jnp.exp(jnp.where(incl[d], diff, 0.0)), 0.0)
        a[key] = jnp.where(strict[d], kq[j][:c] * decay[key], 0.0) * beta_c
    p = dict(a)
    tinv = {key: eye - a[key] for key in chains}
    for _ in range(c.bit_length() - 2):
        pb = {key: p[key].astype(BF16) for key in chains}
        p = {key: _dot(pb[key], _stack_masked(pb[key], blk_c, HEADS)) for key in chains}
        pb = {key: p[key].astype(BF16) for key in chains}
        tinv = {key: tinv[key] + _dot(tinv[key].astype(BF16), _stack_masked(pb[key], blk_c, HEADS))
                for key in chains}
    eg, kf, u, wk = {}, {}, {}, {}
    for j, d in chains:
        key = (j, d)
        kf[key] = k[j].astype(F32)
        eg[key] = jnp.exp(gam_d[key])
        tb = tinv[key].astype(BF16)
        rhs_u = (v_ref[0, tok[j], :].astype(F32) * beta_d[key]).astype(BF16)
        rhs_w = (kf[key] * (beta_d[key] * eg[key])).astype(BF16)
        u[key] = _dot(tb, _stack_masked(rhs_u, blk_d, HEADS))
        wk[key] = _dot(tb, _stack_masked(rhs_w, blk_d, HEADS))
    for j, d in chains:
        key = (j, d)
        gtot = gam_d[key][tot[d]:tot[d] + 1, :]
        u_ref[0, d, tok[j], :] = u[key]
        wk_ref[0, d, tok[j], :] = wk[key].astype(BF16)
        qd_ref[0, d, tok[j], :] = (q[j].astype(F32) * eg[key]).astype(BF16)
        qk_ref[0, d, tok[j], :] = (kq[j][c:] * decay[key]).astype(BF16)
        kd = kf[key] * jnp.exp(gtot - gam_d[key])
        for hd in heads:
            kdt_ref[0, d, j, :, hd * c:(hd + 1) * c] = kd[:, hd * HEAD_DIM:(hd + 1) * HEAD_DIM].T.astype(BF16)
        cd = jnp.exp(gtot)
        cd_ref[0, d, j] = jnp.concatenate([cd[:, hd * HEAD_DIM:(hd + 1) * HEAD_DIM] for hd in heads]
                                          + [jnp.zeros((SUBLANES - HEADS, LANES), F32)], axis=0)


def _delta_prep_call(q, k, v, g):
    b, l, w = k.shape
    nc = l // CHUNK
    cps = math.gcd(PREP_CHUNKS, nc)
    t = cps * CHUNK
    tok = lambda width: pl.BlockSpec((1, t, width), lambda bi, n: (bi, n, 0))
    dtok = lambda width: pl.BlockSpec((1, 2, t, width), lambda bi, n: (bi, 0, n, 0))
    return pl.pallas_call(
        functools.partial(_delta_prep_kernel, cps=cps),
        out_shape=[
            jax.ShapeDtypeStruct((b, 2, l, w), F32),
            jax.ShapeDtypeStruct((b, 2, l, w), BF16),
            jax.ShapeDtypeStruct((b, 2, l, w), BF16),
            jax.ShapeDtypeStruct((b, 2, nc, HEAD_DIM, HEADS * CHUNK), BF16),
            jax.ShapeDtypeStruct((b, 2, l, HEADS * CHUNK), BF16),
            jax.ShapeDtypeStruct((b, 2, nc, SUBLANES, LANES), F32),
        ],
        grid=(b, nc // cps),
        in_specs=[tok(w), tok(w), tok(w), tok(4 * HEADS)],
        out_specs=[dtok(w), dtok(w), dtok(w),
                   pl.BlockSpec((1, 2, cps, HEAD_DIM, HEADS * CHUNK), lambda bi, n: (bi, 0, n, 0, 0)),
                   dtok(HEADS * CHUNK),
                   pl.BlockSpec((1, 2, cps, SUBLANES, LANES), lambda bi, n: (bi, 0, n, 0, 0))],
        compiler_params=_params("arbitrary", "arbitrary"),
        name="delta_prep",
    )(q, k, v, g)


def _delta_scan_kernel(*refs, cps):
    ins = (refs[0:6], refs[6:12])
    s0_ref, of_ref, ob_ref, sfin_ref, s_ref = refs[12:]
    outs = (of_ref, ob_ref)
    n = pl.program_id(1)
    c = CHUNK

    @pl.when(n == 0)
    def _():
        s_ref[...] = s0_ref[0]

    combos = [(d, hd) for d in range(2) for hd in range(HEADS)]
    sl = lambda hd: slice(hd * HEAD_DIM, (hd + 1) * HEAD_DIM)
    cs = lambda hd: slice(hd * c, (hd + 1) * c)
    s = {(d, hd): s_ref[d, hd] for d, hd in combos}
    for step in range(cps):
        pos = (step, cps - 1 - step)
        tok = [slice(pos[d] * c, (pos[d] + 1) * c) for d in range(2)]
        m1, w, m2 = {}, {}, {}
        for d, hd in combos:
            u_ref, wk_ref, qd_ref, kdt_ref, qk_ref, cd_ref = ins[d]
            lhs = jnp.concatenate([wk_ref[0, 0, tok[d], sl(hd)], qd_ref[0, 0, tok[d], sl(hd)]], axis=0)
            m1[d, hd] = _dot(lhs, s[d, hd].astype(BF16))
        for d, hd in combos:
            w[d, hd] = (ins[d][0][0, 0, tok[d], sl(hd)] - m1[d, hd][:c]).astype(BF16)
        for d, hd in combos:
            u_ref, wk_ref, qd_ref, kdt_ref, qk_ref, cd_ref = ins[d]
            lhs = jnp.concatenate([qk_ref[0, 0, tok[d], cs(hd)], kdt_ref[0, 0, pos[d], :, cs(hd)]], axis=0)
            m2[d, hd] = _dot(lhs, w[d, hd])
        for d, hd in combos:
            outs[d][0, tok[d], sl(hd)] = (m1[d, hd][c:] + m2[d, hd][:c]).astype(BF16)
            s[d, hd] = ins[d][5][0, 0, pos[d], hd:hd + 1, :] * s[d, hd] + m2[d, hd][c:]
    for d, hd in combos:
        s_ref[d, hd] = s[d, hd]

    @pl.when(n == pl.num_programs(1) - 1)
    def _():
        sfin_ref[0] = s_ref[...]


def _delta_scan_call(prep, s0):
    u, wk, qd, kdt, qk, cd = prep
    b, _, l, w = u.shape
    cps = math.gcd(SCAN_CHUNKS, l // CHUNK)
    t = cps * CHUNK
    ns = l // t

    def specs(d, blk):
        tok = lambda width: pl.BlockSpec((1, 1, t, width), lambda bi, n: (bi, d, blk(n), 0))
        return [tok(w), tok(w), tok(w),
                pl.BlockSpec((1, 1, cps, HEAD_DIM, HEADS * CHUNK), lambda bi, n: (bi, d, blk(n), 0, 0)),
                tok(HEADS * CHUNK),
                pl.BlockSpec((1, 1, cps, SUBLANES, LANES), lambda bi, n: (bi, d, blk(n), 0, 0))]

    st = pl.BlockSpec((1, 2, HEADS, HEAD_DIM, HEAD_DIM), lambda bi, n: (bi, 0, 0, 0, 0))
    return pl.pallas_call(
        functools.partial(_delta_scan_kernel, cps=cps),
        out_shape=[jax.ShapeDtypeStruct((b, l, w), BF16)] * 2
        + [jax.ShapeDtypeStruct((b, 2, HEADS, HEAD_DIM, HEAD_DIM), F32)],
        grid=(b, ns),
        in_specs=specs(0, lambda n: n) + specs(1, lambda n: ns - 1 - n) + [st],
        out_specs=[pl.BlockSpec((1, t, w), lambda bi, n: (bi, n, 0)),
                   pl.BlockSpec((1, t, w), lambda bi, n: (bi, ns - 1 - n, 0)), st],
        scratch_shapes=[pltpu.VMEM((2, HEADS, HEAD_DIM, HEAD_DIM), F32)],
        compiler_params=_params("arbitrary", "arbitrary"),
        name="delta_scan",
    )(*prep, *prep, s0)


def _ffn_prenorm(x, nffn_ref, sh2_ref, sc2_ref, rt_ref, h2_ref, lg_ref):
    h2 = _norm_mod(x, nffn_ref[...], sh2_ref[0], sc2_ref[0])
    tm, d = x.shape
    pieces = d // LANES
    for j in range(pieces):
        h2_ref[0, pl.ds(j, tm, stride=pieces), :] = h2[:, j * LANES:(j + 1) * LANES]
    lg_ref[0] = _dot_nt(rt_ref[...], h2.astype(BF16))


def _out0_kernel(of_ref, ob_ref, z_ref, p_ref, x_ref, band_ref, cnt_ref, onorm_ref, pw_ref, ps_ref, wout_ref,
                 g1_ref, sh2_ref, sc2_ref, nffn_ref, rt_ref, x1_ref, h2_ref, lg_ref):
    o = of_ref[0].astype(F32) + ob_ref[0].astype(F32)
    z = z_ref[0].astype(F32)
    pin = p_ref[0].astype(F32)
    parts = []
    for hd in range(HEADS):
        sl = slice(hd * HEAD_DIM, (hd + 1) * HEAD_DIM)
        oh = o[:, sl]
        ms = jnp.mean(oh * oh, axis=-1, keepdims=True)
        parts.append(oh * lax.rsqrt(ms + EPS) * onorm_ref[...] * _silu(z[:, sl]))
    pr = band_ref.shape[1]
    for gi in range(len(POOL_WINDOWS)):
        sl = slice(gi * POOL_GROUP, (gi + 1) * POOL_GROUP)
        band = band_ref[gi]
        grp = []
        for r0 in range(0, pin.shape[0], pr):
            u = pin[r0:r0 + pr, sl]
            uh = u.astype(BF16)
            ul = (u - uh.astype(F32)).astype(BF16)
            wsum = _dot(band, uh) + _dot(band, ul)
            grp.append(wsum / cnt_ref[gi] - u)
        parts.append(_dot(jnp.concatenate(grp, axis=0).astype(BF16), pw_ref[gi]) * ps_ref[:, sl])
    cat = jnp.concatenate(parts, axis=1).astype(BF16)
    x1 = x_ref[0] + g1_ref[0] * _dot(cat, wout_ref[...])
    x1_ref[0] = x1
    _ffn_prenorm(x1, nffn_ref, sh2_ref, sc2_ref, rt_ref, h2_ref, lg_ref)


POOL_ROWS = 256


def _pool_tables(tm):
    t = jnp.arange(tm)
    seg = t // GRID_W
    loc = t % GRID_W
    bands, cnts = [], []
    for w in POOL_WINDOWS:
        lo = jnp.clip(loc - w // 2, 0, GRID_W)
        hi = jnp.clip(loc + w - w // 2, 0, GRID_W)
        inside = (seg[:, None] == seg[None, :]) & (loc[None, :] >= lo[:, None]) & (loc[None, :] < hi[:, None])
        bands.append(inside.astype(BF16))
        cnts.append((hi - lo).astype(F32)[:, None])
    return jnp.stack(bands), jnp.stack(cnts)


def _out0_call(o_f, o_b, z, pin, x, onorm, pool_w, pool_scale, w_out, g1, sh2, sc2, nffn, router_t, tm):
    b, l, d = x.shape
    tm = min(tm, l)
    w = HEADS * HEAD_DIM
    e = router_t.shape[0]
    band, cnt = _pool_tables(math.gcd(tm, POOL_ROWS))
    tok = lambda width: pl.BlockSpec((1, tm, width), lambda bi, i: (bi, i, 0))
    vec = pl.BlockSpec((1, 1, d), lambda bi, i: (bi, 0, 0))
    return pl.pallas_call(
        _out0_kernel,
        out_shape=[jax.ShapeDtypeStruct((b, l, d), F32), jax.ShapeDtypeStruct((b, l * d // LANES, LANES), F32),
                   jax.ShapeDtypeStruct((b, e, l), F32)],
        grid=(b, l // tm),
        in_specs=[tok(w), tok(w), tok(w), tok(w), tok(d),
                  _const_spec(band.shape), _const_spec(cnt.shape), _const_spec(onorm.shape),
                  _const_spec(pool_w.shape), _const_spec(pool_scale.shape), _const_spec(w_out.shape),
                  vec, vec, vec, _const_spec(nffn.shape), _const_spec(router_t.shape)],
        out_specs=[tok(d), pl.BlockSpec((1, tm * d // LANES, LANES), lambda bi, i: (bi, i, 0)),
                   pl.BlockSpec((1, e, tm), lambda bi, i: (bi, 0, i))],
        compiler_params=_params("arbitrary", "arbitrary"),
        name="out0",
    )(o_f, o_b, z, pin, x, band, cnt, onorm, pool_w, pool_scale, w_out, g1, sh2, sc2, nffn, router_t)


def _excl_cumsum_lanes(m):
    rows, n = m.shape
    ri = lax.broadcasted_iota(I32, (LANES, LANES), 0)
    ci = lax.broadcasted_iota(I32, (LANES, LANES), 1)
    upper = jnp.where(ri <= ci, 1.0, 0.0).astype(BF16)
    carry = jnp.zeros((rows, 1), F32)
    outs = []
    for blk in range(n // LANES):
        x = m[:, blk * LANES:(blk + 1) * LANES]
        inc = _dot(x.astype(BF16), upper)
        outs.append(inc - x + carry)
        carry = carry + inc[:, LANES - 1:LANES]
    return jnp.concatenate(outs, axis=1)


def _route_kernel(lg_ref, idx_ref, gate_ref, *, cap):
    x = lg_ref[0]
    e, n = x.shape
    ex = jnp.exp(x - jnp.max(x, axis=0, keepdims=True))
    aff = ex / jnp.sum(ex, axis=0, keepdims=True)

    def count_ge(bits):
        return jnp.sum(jnp.where(aff >= pltpu.bitcast(bits, F32), 1.0, 0.0), axis=1, keepdims=True)

    def bisect(_, lohi):
        lo, hi = lohi
        mid = lo + ((hi - lo + 1) >> 1)
        ok = count_ge(mid) >= cap
        return jnp.where(ok, mid, lo), jnp.where(ok, hi, mid - 1)

    lo0 = jnp.zeros((e, 1), I32)
    hi0 = jnp.full((e, 1), 0x7F800000, I32)
    thr, _ = lax.fori_loop(0, 31, bisect, (lo0, hi0))
    above = jnp.where(aff >= pltpu.bitcast(thr + 1, F32), 1.0, 0.0)
    tied = jnp.where(aff >= pltpu.bitcast(thr, F32), 1.0, 0.0) - above
    need = cap - jnp.sum(above, axis=1, keepdims=True)
    sel = above + tied * jnp.where(_excl_cumsum_lanes(tied) < need, 1.0, 0.0)
    pos = _excl_cumsum_lanes(sel)
    tok = lax.broadcasted_iota(I32, (e, n), 1)
    key = jnp.where(sel > 0.0, (tok - pos.astype(I32)) | VALID_BIT, 0)
    shift = 1
    while shift < n:
        mk = pltpu.roll(key, n - shift, axis=1)
        mt = pltpu.roll(tok, n - shift, axis=1)
        ma = pltpu.roll(aff, n - shift, axis=1)
        take = (mk & shift) != 0
        key = jnp.where(take, mk, jnp.where((key & shift) == 0, key, 0))
        tok = jnp.where(take, mt, tok)
        aff = jnp.where(take, ma, aff)
        shift *= 2
    idx_ref[0] = tok[:, :cap]
    gate_ref[0] = aff[:, :cap]


def _route_call(logits_t, cap):
    b, e, n = logits_t.shape
    return pl.pallas_call(
        functools.partial(_route_kernel, cap=cap),
        out_shape=[jax.ShapeDtypeStruct((b, e, cap), I32), jax.ShapeDtypeStruct((b, e, cap), F32)],
        grid=(b,),
        in_specs=[pl.BlockSpec((1, e, n), lambda bi: (bi, 0, 0))],
        out_specs=[pl.BlockSpec((1, e, cap), lambda bi: (bi, 0, 0))] * 2,
        compiler_params=_params("arbitrary"),
        name="route",
    )(logits_t)


GATHER_UNROLL = 8


def _gather_row(idx_ref, src_ref, dst_ref, s):
    t = idx_ref[0, 0, s]
    dst_ref[pl.ds(pl.multiple_of(s * SUBLANES, SUBLANES), SUBLANES), :] = (
        src_ref[0, pl.ds(pl.multiple_of(t * SUBLANES, SUBLANES), SUBLANES), :])


def _moe_up_kernel(idx_ref, idx_next_ref, src_ref, wg_ref, wu_ref, hid_ref, rows0_ref, rows1_ref, *, cap):
    e = pl.program_id(1)

    @pl.when(e == 0)
    def _():
        def gather(cidx, carry):
            for u in range(GATHER_UNROLL):
                _gather_row(idx_ref, src_ref, rows0_ref, cidx * GATHER_UNROLL + u)
            return carry
        lax.fori_loop(0, cap // GATHER_UNROLL, gather, 0)

    def step(cur_ref, next_ref):
        for s in range(cap):
            _gather_row(idx_next_ref, src_ref, next_ref, s)
        x = _from_token_tiles(cur_ref[...], cap).astype(BF16)
        g = _dot(x, wg_ref[0, 0].astype(BF16))
        u = _dot(x, wu_ref[0, 0].astype(BF16))
        hid_ref[0, 0] = (_silu(g) * u).astype(BF16)

    @pl.when(e % 2 == 0)
    def _():
        step(rows0_ref, rows1_ref)

    @pl.when(e % 2 == 1)
    def _():
        step(rows1_ref, rows0_ref)


def _moe_up_call(idx, src, w_gate, w_up, layer, cap):
    b, nrows, _ = src.shape
    _, e, d, f = w_gate.shape
    assert e % 2 == 0
    idx = idx.reshape(b * e, 1, cap)
    rows = pltpu.VMEM((cap * d // LANES, LANES), F32)
    wspec = pl.BlockSpec((1, 1, d, f), lambda bi, ei: (layer, ei, 0, 0))
    return pl.pallas_call(
        functools.partial(_moe_up_kernel, cap=cap),
        out_shape=jax.ShapeDtypeStruct((b, e, cap, f), BF16),
        grid=(b, e),
        in_specs=[
            pl.BlockSpec((1, 1, cap), lambda bi, ei: (bi * e + ei, 0, 0), memory_space=pltpu.SMEM),
            pl.BlockSpec((1, 1, cap), lambda bi, ei: (bi * e + jnp.minimum(ei + 1, e - 1), 0, 0),
                         memory_space=pltpu.SMEM),
            pl.BlockSpec((1, nrows, LANES), lambda bi, ei: (bi, 0, 0), pipeline_mode=pl.Buffered(1)),
            wspec, wspec,
        ],
        out_specs=pl.BlockSpec((1, 1, cap, f), lambda bi, ei: (bi, ei, 0, 0)),
        scratch_shapes=[rows, rows],
        compiler_params=_params("arbitrary", "arbitrary"),
        name="moe_up",
    )(idx, idx, src, w_gate, w_up)


SCATTER_UNROLL = 8
DOWN_SPLIT = 1


def _moe_down_kernel(idx_ref, gate_ref, hid_ref, hid_next_ref, wd_ref, wd_next_ref, acc_ref, y0_ref, y1_ref,
                     *, cap, rows):
    e = pl.program_id(1)

    def project(h_ref, w_ref, y_ref):
        y = _dot(h_ref[0, 0], w_ref[0, 0].astype(BF16))
        for j in range(rows):
            y_ref[pl.ds(j, cap, stride=rows), :] = y[:, j * LANES:(j + 1) * LANES]

    @pl.when(e == 0)
    def _():
        acc_ref[...] = jnp.zeros_like(acc_ref)
        project(hid_ref, wd_ref, y0_ref)

    def step(cur_ref, next_ref):
        project(hid_next_ref, wd_next_ref, next_ref)
        for base in range(0, cap, SCATTER_UNROLL):
            dsts, vals = [], []
            for s in range(base, base + SCATTER_UNROLL):
                t = idx_ref[0, 0, s]
                dst = pl.ds(pl.multiple_of(t * rows, rows), rows)
                dsts.append(dst)
                vals.append(acc_ref[0, dst, :] + gate_ref[0, 0, s] * cur_ref[s * rows:(s + 1) * rows, :])
            for dst, val in zip(dsts, vals):
                acc_ref[0, dst, :] = val

    @pl.when(e % 2 == 0)
    def _():
        step(y0_ref, y1_ref)

    @pl.when(e % 2 == 1)
    def _():
        step(y1_ref, y0_ref)


def _moe_down_call(idx, gate, hid, wd, layer, n):
    b, e, cap, f = hid.shape
    d = wd.shape[3]
    assert e % 2 == 0
    dw = d // DOWN_SPLIT
    rows = dw // LANES
    nxt = lambda ei: jnp.minimum(ei + 1, e - 1)
    slot = lambda bh, ei: ((bh // DOWN_SPLIT) * e + ei, 0, 0)
    y = pltpu.VMEM((cap * rows, LANES), F32)
    out = pl.pallas_call(
        functools.partial(_moe_down_kernel, cap=cap, rows=rows),
        out_shape=jax.ShapeDtypeStruct((b * DOWN_SPLIT, n * rows, LANES), F32),
        grid=(b * DOWN_SPLIT, e),
        in_specs=[
            pl.BlockSpec((1, 1, cap), slot, memory_space=pltpu.SMEM),
            pl.BlockSpec((1, 1, cap), slot, memory_space=pltpu.SMEM),
            pl.BlockSpec((1, 1, cap, f), lambda bh, ei: (bh // DOWN_SPLIT, ei, 0, 0)),
            pl.BlockSpec((1, 1, cap, f), lambda bh, ei: (bh // DOWN_SPLIT, nxt(ei), 0, 0)),
            pl.BlockSpec((1, 1, f, dw), lambda bh, ei: (layer, ei, 0, bh % DOWN_SPLIT)),
            pl.BlockSpec((1, 1, f, dw), lambda bh, ei: (layer, nxt(ei), 0, bh % DOWN_SPLIT)),
        ],
        out_specs=pl.BlockSpec((1, n * rows, LANES), lambda bh, ei: (bh, 0, 0), pipeline_mode=pl.Buffered(1)),
        scratch_shapes=[y, y],
        compiler_params=_params("arbitrary", "arbitrary"),
        name="moe_down",
    )(idx.reshape(b * e, 1, cap), gate.reshape(b * e, 1, cap), hid, hid, wd, wd)
    return out.reshape(b, DOWN_SPLIT, n * rows, LANES)


def _moe(h2, logits_t, w_gate, w_up, wd, layer):
    b, e, n = logits_t.shape
    cap = 2 * n // e
    idx, gate = _route_call(logits_t, cap)
    hid = _moe_up_call(idx, h2, w_gate, w_up, layer, cap)
    return _moe_down_call(idx, gate, hid, wd, layer, n)


def _join(m_ref, tm):
    rows = m_ref.shape[2] // tm
    return jnp.concatenate([m_ref[0, h, pl.ds(j, tm, stride=rows), :]
                            for h in range(DOWN_SPLIT) for j in range(rows)], axis=1)


def _mix1_kernel(xp_ref, x_ref, xn_ref, mp_ref, m_ref, mn_ref, g2p_ref, sh1_ref, sc1_ref, nmix_ref, win_ref,
                 conv_ref, wout_ref, g1_ref, sh2_ref, sc2_ref, nffn_ref, rt_ref, x3_ref, h2_ref, lg_ref, *, tm):
    i = pl.program_id(1)
    last = pl.num_programs(1) - 1
    d = x_ref.shape[2]
    x2 = x_ref[0] + g2p_ref[0] * _join(m_ref, tm)
    x2p = xp_ref[0] + g2p_ref[0] * _join(mp_ref, SUBLANES)
    x2n = xn_ref[0] + g2p_ref[0] * _join(mn_ref, SUBLANES)
    norm = lambda t: _norm_mod(t, nmix_ref[...], sh1_ref[0], sc1_ref[0])
    hb = jnp.concatenate([jnp.where(i == 0, 0.0, norm(x2p)), norm(x2), jnp.where(i == last, 0.0, norm(x2n))],
                         axis=0).astype(BF16)
    proj = _dot(hb, win_ref[...])
    lo = SUBLANES
    u = proj[:, d:2 * d] * proj[:, 2 * d:]
    cw = conv_ref[...]
    cv = u[lo - 1:lo - 1 + tm] * cw[0:1] + u[lo:lo + tm] * cw[1:2] + u[lo + 1:lo + 1 + tm] * cw[2:3]
    y = _dot((proj[lo:lo + tm, :d] * cv).astype(BF16), wout_ref[...])
    x3 = x2 + g1_ref[0] * y
    x3_ref[0] = x3
    _ffn_prenorm(x3, nffn_ref, sh2_ref, sc2_ref, rt_ref, h2_ref, lg_ref)


def _mix1_call(x1, moe, g2p, sh1, sc1, nmix, w_in, conv_w, w_out, g1, sh2, sc2, nffn, router_t, tm):
    b, l, d = x1.shape
    tm = min(tm, l)
    nb8 = l // SUBLANES
    r = tm // SUBLANES
    e = router_t.shape[0]
    mr = d // DOWN_SPLIT // LANES
    prev = lambda bi, i: (bi, jnp.maximum(i * r - 1, 0), 0)
    nxt = lambda bi, i: (bi, jnp.minimum((i + 1) * r, nb8 - 1), 0)
    tok = pl.BlockSpec((1, tm, d), lambda bi, i: (bi, i, 0))
    vec = pl.BlockSpec((1, 1, d), lambda bi, i: (bi, 0, 0))
    return pl.pallas_call(
        functools.partial(_mix1_kernel, tm=tm),
        out_shape=[jax.ShapeDtypeStruct((b, l, d), F32), jax.ShapeDtypeStruct((b, l * d // LANES, LANES), F32),
                   jax.ShapeDtypeStruct((b, e, l), F32)],
        grid=(b, l // tm),
        in_specs=[
            pl.BlockSpec((1, SUBLANES, d), prev), tok, pl.BlockSpec((1, SUBLANES, d), nxt),
            pl.BlockSpec((1, DOWN_SPLIT, SUBLANES * mr, LANES), lambda bi, i: (bi, 0, jnp.maximum(i * r - 1, 0), 0)),
            pl.BlockSpec((1, DOWN_SPLIT, tm * mr, LANES), lambda bi, i: (bi, 0, i, 0)),
            pl.BlockSpec((1, DOWN_SPLIT, SUBLANES * mr, LANES),
                         lambda bi, i: (bi, 0, jnp.minimum((i + 1) * r, nb8 - 1), 0)),
            vec, vec, vec, _const_spec(nmix.shape), _const_spec(w_in.shape), _const_spec(conv_w.shape),
            _const_spec(w_out.shape), vec, vec, vec, _const_spec(nffn.shape), _const_spec(router_t.shape),
        ],
        out_specs=[tok, pl.BlockSpec((1, tm * d // LANES, LANES), lambda bi, i: (bi, i, 0)),
                   pl.BlockSpec((1, e, tm), lambda bi, i: (bi, 0, i))],
        compiler_params=_params("arbitrary", "arbitrary"),
        name="mix1",
    )(x1, x1, x1, moe, moe, moe, g2p, sh1, sc1, nmix, w_in, conv_w, w_out, g1, sh2, sc2, nffn, router_t)


def _final_kernel(x_ref, m_ref, g2_ref, nf_ref, o_ref):
    x = x_ref[0] + g2_ref[0] * _join(m_ref, x_ref.shape[1])
    ms = jnp.mean(x * x, axis=-1, keepdims=True)
    o_ref[0] = x * lax.rsqrt(ms + EPS) * nf_ref[...]


def _final_call(x3, moe, g2, nf, tm):
    b, l, d = x3.shape
    tm = min(tm, l)
    mr = d // DOWN_SPLIT // LANES
    tok = pl.BlockSpec((1, tm, d), lambda bi, i: (bi, i, 0))
    return pl.pallas_call(
        _final_kernel,
        out_shape=jax.ShapeDtypeStruct((b, l, d), F32),
        grid=(b, l // tm),
        in_specs=[tok, pl.BlockSpec((1, DOWN_SPLIT, tm * mr, LANES), lambda bi, i: (bi, 0, i, 0)),
                  pl.BlockSpec((1, 1, d), lambda bi, i: (bi, 0, 0)), _const_spec(nf.shape)],
        out_specs=tok,
        compiler_params=_params("arbitrary", "arbitrary"),
        name="final",
    )(x3, moe, g2, nf)


def _layer0_mixer_inputs(ev_w_in, dn_a_log, dn_dt_bias):
    w = HEADS * HEAD_DIM
    qkv_w = 3 * w
    wqkv = ev_w_in[:, :qkv_w].astype(BF16)
    wz = ev_w_in[:, qkv_w:qkv_w + w]
    wg = ev_w_in[:, qkv_w + w:qkv_w + w + 4 * HEADS]
    wp = ev_w_in[:, qkv_w + w + 4 * HEADS:]
    wrest = jnp.concatenate([wz, wp, jnp.pad(wg, ((0, 0), (0, LANES - 4 * HEADS)))], axis=1).astype(BF16)
    pad = jnp.zeros((2 * HEADS,), F32)
    tail = jnp.zeros((LANES - 4 * HEADS,), F32)
    gpar = jnp.stack([jnp.concatenate([pad, dn_a_log.reshape(-1), tail]),
                      jnp.concatenate([pad, dn_dt_bias.reshape(-1), tail])])
    return wqkv, wrest, gpar


def kernel(x, c, ctx, c_ctx, ada_w, ada_b, norm_mix, norm_ffn, norm_final, ev_w_in, dn_conv, dn_a_log, dn_dt_bias,
           dn_norm, pool_w, pool_scale, ev_w_out, sc_w_in, sc_conv, sc_w_out, router, w_gate, w_up, w_down):
    b, n, d = x.shape
    depth = ada_w.shape[0]
    assert depth == 2 and b + 1 <= SUBLANES and n % GRID_W == 0 and n % CHUNK == 0 and ctx.shape[1] % CHUNK == 0

    cc = jnp.concatenate([c, c_ctx[None], jnp.zeros((SUBLANES - b - 1, d), F32)], axis=0)
    mods = _ada_call(cc, ada_w, ada_b)

    def mod(layer, k, rows=slice(0, b)):
        return mods[layer, rows, None, k * d:(k + 1) * d]

    ctx_rows = lambda layer, k: jnp.broadcast_to(mods[layer, b:b + 1, None, k * d:(k + 1) * d], (b, 1, d))
    row = lambda v: v.reshape(1, -1)
    router_t = jnp.swapaxes(router, 1, 2).astype(BF16)

    wqkv, wrest, gpar = _layer0_mixer_inputs(ev_w_in[0], dn_a_log[0], dn_dt_bias[0])
    nmix0 = row(norm_mix[0])
    qc, kc, vc, _, _, gc = _proj0_call(ctx, ctx_rows(0, 0), ctx_rows(0, 1), nmix0, wqkv, wrest, dn_conv[0], gpar, 256)
    ql, kl, vl, zl, pl_in, gl = _proj0_call(x, mod(0, 0), mod(0, 1), nmix0, wqkv, wrest, dn_conv[0], gpar, 512)
    s0 = jnp.zeros((b, 2, HEADS, HEAD_DIM, HEAD_DIM), F32)
    _, _, s_ctx = _delta_scan_call(_delta_prep_call(qc, kc, vc, gc), s0)
    o_f, o_b, _ = _delta_scan_call(_delta_prep_call(ql, kl, vl, gl), s_ctx)
    x1, h2, lg = _out0_call(o_f, o_b, zl, pl_in, x, row(dn_norm[0]), pool_w[0].astype(BF16), row(pool_scale[0]),
                            ev_w_out[0].astype(BF16), mod(0, 2), mod(0, 3), mod(0, 4), row(norm_ffn[0]),
                            router_t[0], 512)
    moe0 = _moe(h2, lg, w_gate, w_up, w_down, 0)

    x3, h2, lg = _mix1_call(x1, moe0, mod(0, 5), mod(1, 0), mod(1, 1), row(norm_mix[1]), sc_w_in[0].astype(BF16),
                            sc_conv[0], sc_w_out[0].astype(BF16), mod(1, 2), mod(1, 3), mod(1, 4),
                            row(norm_ffn[1]), router_t[1], 512)
    moe1 = _moe(h2, lg, w_gate, w_up, w_down, 1)
    return _final_call(x3, moe1, mod(1, 5), row(norm_final), 1024)
```

```python
import functools
import math

import jax
import jax.numpy as jnp
from jax import lax
from jax.experimental import pallas as pl
from jax.experimental.pallas import tpu as pltpu

F32 = jnp.float32
BF16 = jnp.bfloat16
I32 = jnp.int32

EPS = 1e-6
GRID_W = 64
HEADS = 4
HEAD_DIM = 128
CHUNK = 64
POOL_WINDOWS = (2, 4, 8, 16)
POOL_GROUP = 128
LANES = 128
SUBLANES = 8
VMEM_LIMIT = 56 * 1024 * 1024
VALID_BIT = 1 << 30


def _silu(x):
    return x * jax.nn.sigmoid(x)


def _norm_mod(x, g, shift, scale):
    ms = jnp.mean(x * x, axis=-1, keepdims=True)
    return (x * lax.rsqrt(ms + EPS) * g) * (1.0 + scale) + shift


def _dot(a, b):
    return jnp.dot(a, b, preferred_element_type=F32)


def _dot_nt(a, b):
    return lax.dot_general(a, b, (((1,), (1,)), ((), ())), preferred_element_type=F32)


def _dot_tn(a, b):
    return lax.dot_general(a, b, (((0,), (0,)), ((), ())), preferred_element_type=F32)


def _split3(x):
    hi = x.astype(BF16)
    r = x - hi.astype(F32)
    mid = r.astype(BF16)
    lo = (r - mid.astype(F32)).astype(BF16)
    return hi, mid, lo


def _from_token_tiles(t, rows):
    p = t.shape[0] // rows
    chunks = jnp.swapaxes(t.reshape(rows, p, LANES), 0, 1)
    return jnp.concatenate([chunks[j] for j in range(p)], axis=1)


def _const_spec(shape):
    nd = len(shape)
    return pl.BlockSpec(shape, lambda *_: (0,) * nd, pipeline_mode=pl.Buffered(1))


def _params(*sem):
    return pltpu.CompilerParams(dimension_semantics=sem, vmem_limit_bytes=VMEM_LIMIT)


def _ada_kernel(c_ref, w_ref, b_ref, o_ref):
    s = _silu(c_ref[...])
    o_ref[0] = _dot(s.astype(BF16), w_ref[0].astype(BF16)) + b_ref[0]


def _ada_call(cc, ada_w, ada_b):
    depth, d, n6 = ada_w.shape
    tn = n6 // 4
    return pl.pallas_call(
        _ada_kernel,
        out_shape=jax.ShapeDtypeStruct((depth, SUBLANES, n6), F32),
        grid=(depth, n6 // tn),
        in_specs=[
            pl.BlockSpec((SUBLANES, d), lambda i, j: (0, 0)),
            pl.BlockSpec((1, d, tn), lambda i, j: (i, 0, j)),
            pl.BlockSpec((1, 1, tn), lambda i, j: (i, 0, j)),
        ],
        out_specs=pl.BlockSpec((1, SUBLANES, tn), lambda i, j: (i, 0, j)),
        compiler_params=_params("arbitrary", "arbitrary"),
        name="adaln",
    )(cc, ada_w, ada_b.reshape(depth, 1, n6))


def _proj0_kernel(xp_ref, x_ref, xn_ref, sh_ref, sc_ref, g_ref, wqkv_ref, wrest_ref, conv_ref, gpar_ref,
                  q_ref, k_ref, v_ref, z_ref, p_ref, gt_ref, *, tm):
    i = pl.program_id(1)
    last = pl.num_programs(1) - 1
    norm = lambda t: _norm_mod(t, g_ref[...], sh_ref[0], sc_ref[0])
    hb = jnp.concatenate([jnp.where(i == 0, 0.0, norm(xp_ref[0])), norm(x_ref[0]),
                          jnp.where(i == last, 0.0, norm(xn_ref[0]))], axis=0).astype(BF16)
    proj = _dot(hb, wqkv_ref[...])
    cw = conv_ref[...]
    lo = SUBLANES
    rows = tm + 2 * lo
    a = (pltpu.roll(proj, 1, axis=0)[lo:lo + tm] * cw[0:1] + proj[lo:lo + tm] * cw[1:2]
         + pltpu.roll(proj, rows - 1, axis=0)[lo:lo + tm] * cw[2:3])
    a = _silu(a)
    w = HEADS * HEAD_DIM
    for hd in range(HEADS):
        sl = slice(hd * HEAD_DIM, (hd + 1) * HEAD_DIM)
        qh = a[:, sl]
        kh = a[:, w + hd * HEAD_DIM: w + (hd + 1) * HEAD_DIM]
        qn = qh * lax.rsqrt(jnp.sum(qh * qh, axis=-1, keepdims=True) + EPS) * (HEAD_DIM ** -0.5)
        q_ref[0, :, sl] = qn.astype(BF16)
        k_ref[0, :, sl] = (kh * lax.rsqrt(jnp.sum(kh * kh, axis=-1, keepdims=True) + EPS)).astype(BF16)
    v_ref[0] = a[:, 2 * w:].astype(BF16)
    rest = _dot(hb[lo:lo + tm], wrest_ref[...])
    z_ref[0] = rest[:, :w].astype(BF16)
    p_ref[0] = rest[:, w:2 * w].astype(BF16)
    gates = rest[:, 2 * w:]
    col = lax.broadcasted_iota(I32, (1, LANES), 1)
    xb = gates + gpar_ref[1:2]
    softplus = jnp.maximum(xb, 0.0) + jnp.log1p(jnp.exp(-jnp.abs(xb)))
    log_decay = -jnp.exp(gpar_ref[0:1]) * softplus
    out = jnp.where(col < 2 * HEADS, jax.nn.sigmoid(gates), log_decay)
    gt_ref[0] = out[:, :4 * HEADS]


def _proj0_call(x, shift, scale, gain, wqkv, wrest, conv_w, gpar, tm):
    b, l, d = x.shape
    tm = min(tm, l)
    nt = l // tm
    nb8 = l // SUBLANES
    r = tm // SUBLANES
    w = HEADS * HEAD_DIM
    tok = lambda width: pl.BlockSpec((1, tm, width), lambda bi, i: (bi, i, 0))
    vec = pl.BlockSpec((1, 1, d), lambda bi, i: (bi, 0, 0))
    return pl.pallas_call(
        functools.partial(_proj0_kernel, tm=tm),
        out_shape=[jax.ShapeDtypeStruct((b, l, w), BF16)] * 5 + [jax.ShapeDtypeStruct((b, l, 4 * HEADS), F32)],
        grid=(b, nt),
        in_specs=[
            pl.BlockSpec((1, SUBLANES, d), lambda bi, i: (bi, jnp.maximum(i * r - 1, 0), 0)),
            tok(d),
            pl.BlockSpec((1, SUBLANES, d), lambda bi, i: (bi, jnp.minimum((i + 1) * r, nb8 - 1), 0)),
            vec, vec,
            _const_spec((1, d)),
            _const_spec(wqkv.shape),
            _const_spec(wrest.shape),
            _const_spec(conv_w.shape),
            _const_spec(gpar.shape),
        ],
        out_specs=[tok(w)] * 5 + [tok(4 * HEADS)],
        compiler_params=_params("arbitrary", "arbitrary"),
        name="proj0",
    )(x, x, x, shift, scale, gain, wqkv, wrest, conv_w, gpar)


PREP_CHUNKS = 8
SCAN_CHUNKS = 8


def _stack_masked(x, block_of_lane, nblocks):
    return jnp.concatenate([jnp.where(block_of_lane == h, x, jnp.zeros_like(x)) for h in range(nblocks)], axis=0)


def _delta_prep_kernel(q_ref, k_ref, v_ref, g_ref, u_ref, wk_ref, qd_ref, kdt_ref, qk_ref, cd_ref, *, cps):
    c = CHUNK
    wc = HEADS * c
    wd = HEADS * HEAD_DIM
    ri = lax.broadcasted_iota(I32, (c, c), 0)
    ci = lax.broadcasted_iota(I32, (c, c), 1)
    tri_l = jnp.where(ri >= ci, 1.0, 0.0).astype(BF16)
    tri_u = jnp.where(ri <= ci, 1.0, 0.0).astype(BF16)
    row = lax.broadcasted_iota(I32, (c, wc), 0)
    lane = lax.broadcasted_iota(I32, (c, wc), 1)
    pos = lane % c
    blk_c = lax.broadcasted_iota(I32, (1, wc), 1) // c
    blk_d = lax.broadcasted_iota(I32, (1, wd), 1) // HEAD_DIM
    eye = jnp.where(row == pos, 1.0, 0.0)
    incl = (row >= pos, row <= pos)
    strict = (row > pos, row < pos)
    tot = (c - 1, 0)
    chunks = range(cps)
    heads = range(HEADS)
    tok = [slice(j * c, (j + 1) * c) for j in chunks]

    def spread(cols, first, width):
        full = [jnp.broadcast_to(cols[:, first + hd:first + hd + 1], (c, LANES)) for hd in heads]
        if width == LANES:
            return jnp.concatenate(full, axis=1)
        half = lax.broadcasted_iota(I32, (c, LANES), 1) < width
        return jnp.concatenate([jnp.where(half, full[2 * i], full[2 * i + 1]) for i in range(HEADS // 2)], axis=1)

    k = [k_ref[0, tok[j], :] for j in chunks]
    q = [q_ref[0, tok[j], :] for j in chunks]
    g = [g_ref[0, tok[j], :] for j in chunks]
    kq = [_dot_nt(jnp.concatenate([k[j], q[j]], axis=0), _stack_masked(k[j], blk_d, HEADS)) for j in chunks]
    g3 = [_split3(g[j]) for j in chunks]
    cum = [(sum(_dot(tri_l, p) for p in g3[j]), sum(_dot(tri_u, p) for p in g3[j])) for j in chunks]

    chains = [(j, d) for j in chunks for d in range(2)]
    beta_d, gam_d, decay, a = {}, {}, {}, {}
    for j, d in chains:
        key = (j, d)
        gam_c = spread(cum[j][d], 2 * HEADS + d * HEADS, c)
        gam_d[key] = spread(cum[j][d], 2 * HEADS + d * HEADS, HEAD_DIM)
        beta_c = spread(g[j], d * HEADS, c)
        beta_d[key] = spread(g[j], d * HEADS, HEAD_DIM)
        gam_r = jnp.sum(jnp.where(row == pos, gam_c, 0.0), axis=0, keepdims=True)
        diff = gam_c - gam_r
        decay[key] = jnp.where(incl[d], jnp.exp(jnp.where(incl[d], diff, 0.0)), 0.0)
        a[key] = jnp.where(strict[d], kq[j][:c] * decay[key], 0.0) * beta_c
    p = dict(a)
    tinv = {key: eye - a[key] for key in chains}
    for _ in range(c.bit_length() - 2):
        pb = {key: p[key].astype(BF16) for key in chains}
        p = {key: _dot(pb[key], _stack_masked(pb[key], blk_c, HEADS)) for key in chains}
        pb = {key: p[key].astype(BF16) for key in chains}
        tinv = {key: tinv[key] + _dot(tinv[key].astype(BF16), _stack_masked(pb[key], blk_c, HEADS))
                for key in chains}
    eg, kf, u, wk = {}, {}, {}, {}
    for j, d in chains:
        key = (j, d)
        kf[key] = k[j].astype(F32)
        eg[key] = jnp.exp(gam_d[key])
        tb = tinv[key].astype(BF16)
        rhs_u = (v_ref[0, tok[j], :].astype(F32) * beta_d[key]).astype(BF16)
        rhs_w = (kf[key] * (beta_d[key] * eg[key])).astype(BF16)
        u[key] = _dot(tb, _stack_masked(rhs_u, blk_d, HEADS))
        wk[key] = _dot(tb, _stack_masked(rhs_w, blk_d, HEADS))
    for j, d in chains:
        key = (j, d)
        gtot = gam_d[key][tot[d]:tot[d] + 1, :]
        u_ref[0, d, tok[j], :] = u[key]
        wk_ref[0, d, tok[j], :] = wk[key].astype(BF16)
        qd_ref[0, d, tok[j], :] = (q[j].astype(F32) * eg[key]).astype(BF16)
        qk_ref[0, d, tok[j], :] = (kq[j][c:] * decay[key]).astype(BF16)
        kd = kf[key] * jnp.exp(gtot - gam_d[key])
        for hd in heads:
            kdt_ref[0, d, j, :, hd * c:(hd + 1) * c] = kd[:, hd * HEAD_DIM:(hd + 1) * HEAD_DIM].T.astype(BF16)
        cd = jnp.exp(gtot)
        cd_ref[0, d, j] = jnp.concatenate([cd[:, hd * HEAD_DIM:(hd + 1) * HEAD_DIM] for hd in heads]
                                          + [jnp.zeros((SUBLANES - HEADS, LANES), F32)], axis=0)


def _delta_prep_call(q, k, v, g):
    b, l, w = k.shape
    nc = l // CHUNK
    cps = math.gcd(PREP_CHUNKS, nc)
    t = cps * CHUNK
    tok = lambda width: pl.BlockSpec((1, t, width), lambda bi, n: (bi, n, 0))
    dtok = lambda width: pl.BlockSpec((1, 2, t, width), lambda bi, n: (bi, 0, n, 0))
    return pl.pallas_call(
        functools.partial(_delta_prep_kernel, cps=cps),
        out_shape=[
            jax.ShapeDtypeStruct((b, 2, l, w), F32),
            jax.ShapeDtypeStruct((b, 2, l, w), BF16),
            jax.ShapeDtypeStruct((b, 2, l, w), BF16),
            jax.ShapeDtypeStruct((b, 2, nc, HEAD_DIM, HEADS * CHUNK), BF16),
            jax.ShapeDtypeStruct((b, 2, l, HEADS * CHUNK), BF16),
            jax.ShapeDtypeStruct((b, 2, nc, SUBLANES, LANES), F32),
        ],
        grid=(b, nc // cps),
        in_specs=[tok(w), tok(w), tok(w), tok(4 * HEADS)],
        out_specs=[dtok(w), dtok(w), dtok(w),
                   pl.BlockSpec((1, 2, cps, HEAD_DIM, HEADS * CHUNK), lambda bi, n: (bi, 0, n, 0, 0)),
                   dtok(HEADS * CHUNK),
                   pl.BlockSpec((1, 2, cps, SUBLANES, LANES), lambda bi, n: (bi, 0, n, 0, 0))],
        compiler_params=_params("arbitrary", "arbitrary"),
        name="delta_prep",
    )(q, k, v, g)


def _delta_scan_kernel(*refs, cps):
    ins = (refs[0:6], refs[6:12])
    s0_ref, of_ref, ob_ref, sfin_ref, s_ref = refs[12:]
    outs = (of_ref, ob_ref)
    n = pl.program_id(1)
    c = CHUNK

    @pl.when(n == 0)
    def _():
        s_ref[...] = s0_ref[0]

    combos = [(d, hd) for d in range(2) for hd in range(HEADS)]
    sl = lambda hd: slice(hd * HEAD_DIM, (hd + 1) * HEAD_DIM)
    cs = lambda hd: slice(hd * c, (hd + 1) * c)
    s = {(d, hd): s_ref[d, hd] for d, hd in combos}
    for step in range(cps):
        pos = (step, cps - 1 - step)
        tok = [slice(pos[d] * c, (pos[d] + 1) * c) for d in range(2)]
        m1, w, m2 = {}, {}, {}
        for d, hd in combos:
            u_ref, wk_ref, qd_ref, kdt_ref, qk_ref, cd_ref = ins[d]
            lhs = jnp.concatenate([wk_ref[0, 0, tok[d], sl(hd)], qd_ref[0, 0, tok[d], sl(hd)]], axis=0)
            m1[d, hd] = _dot(lhs, s[d, hd].astype(BF16))
        for d, hd in combos:
            w[d, hd] = (ins[d][0][0, 0, tok[d], sl(hd)] - m1[d, hd][:c]).astype(BF16)
        for d, hd in combos:
            u_ref, wk_ref, qd_ref, kdt_ref, qk_ref, cd_ref = ins[d]
            lhs = jnp.concatenate([qk_ref[0, 0, tok[d], cs(hd)], kdt_ref[0, 0, pos[d], :, cs(hd)]], axis=0)
            m2[d, hd] = _dot(lhs, w[d, hd])
        for d, hd in combos:
            outs[d][0, tok[d], sl(hd)] = (m1[d, hd][c:] + m2[d, hd][:c]).astype(BF16)
            s[d, hd] = ins[d][5][0, 0, pos[d], hd:hd + 1, :] * s[d, hd] + m2[d, hd][c:]
    for d, hd in combos:
        s_ref[d, hd] = s[d, hd]

    @pl.when(n == pl.num_programs(1) - 1)
    def _():
        sfin_ref[0] = s_ref[...]


def _delta_scan_call(prep, s0):
    u, wk, qd, kdt, qk, cd = prep
    b, _, l, w = u.shape
    cps = math.gcd(SCAN_CHUNKS, l // CHUNK)
    t = cps * CHUNK
    ns = l // t

    def specs(d, blk):
        tok = lambda width: pl.BlockSpec((1, 1, t, width), lambda bi, n: (bi, d, blk(n), 0))
        return [tok(w), tok(w), tok(w),
                pl.BlockSpec((1, 1, cps, HEAD_DIM, HEADS * CHUNK), lambda bi, n: (bi, d, blk(n), 0, 0)),
                tok(HEADS * CHUNK),
                pl.BlockSpec((1, 1, cps, SUBLANES, LANES), lambda bi, n: (bi, d, blk(n), 0, 0))]

    st = pl.BlockSpec((1, 2, HEADS, HEAD_DIM, HEAD_DIM), lambda bi, n: (bi, 0, 0, 0, 0))
    return pl.pallas_call(
        functools.partial(_delta_scan_kernel, cps=cps),
        out_shape=[jax.ShapeDtypeStruct((b, l, w), BF16)] * 2
        + [jax.ShapeDtypeStruct((b, 2, HEADS, HEAD_DIM, HEAD_DIM), F32)],
        grid=(b, ns),
        in_specs=specs(0, lambda n: n) + specs(1, lambda n: ns - 1 - n) + [st],
        out_specs=[pl.BlockSpec((1, t, w), lambda bi, n: (bi, n, 0)),
                   pl.BlockSpec((1, t, w), lambda bi, n: (bi, ns - 1 - n, 0)), st],
        scratch_shapes=[pltpu.VMEM((2, HEADS, HEAD_DIM, HEAD_DIM), F32)],
        compiler_params=_params("arbitrary", "arbitrary"),
        name="delta_scan",
    )(*prep, *prep, s0)


def _ffn_prenorm(x, nffn_ref, sh2_ref, sc2_ref, rt_ref, h2_ref, lg_ref):
    h2 = _norm_mod(x, nffn_ref[...], sh2_ref[0], sc2_ref[0])
    tm, d = x.shape
    pieces = d // LANES
    for j in range(pieces):
        h2_ref[0, pl.ds(j, tm, stride=pieces), :] = h2[:, j * LANES:(j + 1) * LANES]
    lg_ref[0] = _dot_nt(rt_ref[...], h2.astype(BF16))


def _out0_kernel(of_ref, ob_ref, z_ref, p_ref, x_ref, band_ref, cnt_ref, onorm_ref, pw_ref, ps_ref, wout_ref,
                 g1_ref, sh2_ref, sc2_ref, nffn_ref, rt_ref, x1_ref, h2_ref, lg_ref):
    o = of_ref[0].astype(F32) + ob_ref[0].astype(F32)
    z = z_ref[0].astype(F32)
    pin = p_ref[0].astype(F32)
    parts = []
    for hd in range(HEADS):
        sl = slice(hd * HEAD_DIM, (hd + 1) * HEAD_DIM)
        oh = o[:, sl]
        ms = jnp.mean(oh * oh, axis=-1, keepdims=True)
        parts.append(oh * lax.rsqrt(ms + EPS) * onorm_ref[...] * _silu(z[:, sl]))
    pr = band_ref.shape[1]
    for gi in range(len(POOL_WINDOWS)):
        sl = slice(gi * POOL_GROUP, (gi + 1) * POOL_GROUP)
        band = band_ref[gi]
        grp = []
        for r0 in range(0, pin.shape[0], pr):
            u = pin[r0:r0 + pr, sl]
            uh = u.astype(BF16)
            ul = (u - uh.astype(F32)).astype(BF16)
            wsum = _dot(band, uh) + _dot(band, ul)
            grp.append(wsum / cnt_ref[gi] - u)
        parts.append(_dot(jnp.concatenate(grp, axis=0).astype(BF16), pw_ref[gi]) * ps_ref[:, sl])
    cat = jnp.concatenate(parts, axis=1).astype(BF16)
    x1 = x_ref[0] + g1_ref[0] * _dot(cat, wout_ref[...])
    x1_ref[0] = x1
    _ffn_prenorm(x1, nffn_ref, sh2_ref, sc2_ref, rt_ref, h2_ref, lg_ref)


POOL_ROWS = 256


def _pool_tables(tm):
    t = jnp.arange(tm)
    seg = t // GRID_W
    loc = t % GRID_W
    bands, cnts = [], []
    for w in POOL_WINDOWS:
        lo = jnp.clip(loc - w // 2, 0, GRID_W)
        hi = jnp.clip(loc + w - w // 2, 0, GRID_W)
        inside = (seg[:, None] == seg[None, :]) & (loc[None, :] >= lo[:, None]) & (loc[None, :] < hi[:, None])
        bands.append(inside.astype(BF16))
        cnts.append((hi - lo).astype(F32)[:, None])
    return jnp.stack(bands), jnp.stack(cnts)


def _out0_call(o_f, o_b, z, pin, x, onorm, pool_w, pool_scale, w_out, g1, sh2, sc2, nffn, router_t, tm):
    b, l, d = x.shape
    tm = min(tm, l)
    w = HEADS * HEAD_DIM
    e = router_t.shape[0]
    band, cnt = _pool_tables(math.gcd(tm, POOL_ROWS))
    tok = lambda width: pl.BlockSpec((1, tm, width), lambda bi, i: (bi, i, 0))
    vec = pl.BlockSpec((1, 1, d), lambda bi, i: (bi, 0, 0))
    return pl.pallas_call(
        _out0_kernel,
        out_shape=[jax.ShapeDtypeStruct((b, l, d), F32), jax.ShapeDtypeStruct((b, l * d // LANES, LANES), F32),
                   jax.ShapeDtypeStruct((b, e, l), F32)],
        grid=(b, l // tm),
        in_specs=[tok(w), tok(w), tok(w), tok(w), tok(d),
                  _const_spec(band.shape), _const_spec(cnt.shape), _const_spec(onorm.shape),
                  _const_spec(pool_w.shape), _const_spec(pool_scale.shape), _const_spec(w_out.shape),
                  vec, vec, vec, _const_spec(nffn.shape), _const_spec(router_t.shape)],
        out_specs=[tok(d), pl.BlockSpec((1, tm * d // LANES, LANES), lambda bi, i: (bi, i, 0)),
                   pl.BlockSpec((1, e, tm), lambda bi, i: (bi, 0, i))],
        compiler_params=_params("arbitrary", "arbitrary"),
        name="out0",
    )(o_f, o_b, z, pin, x, band, cnt, onorm, pool_w, pool_scale, w_out, g1, sh2, sc2, nffn, router_t)


def _excl_cumsum_lanes(m):
    rows, n = m.shape
    ri = lax.broadcasted_iota(I32, (LANES, LANES), 0)
    ci = lax.broadcasted_iota(I32, (LANES, LANES), 1)
    upper = jnp.where(ri <= ci, 1.0, 0.0).astype(BF16)
    carry = jnp.zeros((rows, 1), F32)
    outs = []
    for blk in range(n // LANES):
        x = m[:, blk * LANES:(blk + 1) * LANES]
        inc = _dot(x.astype(BF16), upper)
        outs.append(inc - x + carry)
        carry = carry + inc[:, LANES - 1:LANES]
    return jnp.concatenate(outs, axis=1)


def _route_kernel(lg_ref, idx_ref, gate_ref, *, cap):
    x = lg_ref[0]
    e, n = x.shape
    ex = jnp.exp(x - jnp.max(x, axis=0, keepdims=True))
    aff = ex / jnp.sum(ex, axis=0, keepdims=True)

    def count_ge(bits):
        return jnp.sum(jnp.where(aff >= pltpu.bitcast(bits, F32), 1.0, 0.0), axis=1, keepdims=True)

    def bisect(_, lohi):
        lo, hi = lohi
        mid = lo + ((hi - lo + 1) >> 1)
        ok = count_ge(mid) >= cap
        return jnp.where(ok, mid, lo), jnp.where(ok, hi, mid - 1)

    lo0 = jnp.zeros((e, 1), I32)
    hi0 = jnp.full((e, 1), 0x7F800000, I32)
    thr, _ = lax.fori_loop(0, 31, bisect, (lo0, hi0))
    above = jnp.where(aff >= pltpu.bitcast(thr + 1, F32), 1.0, 0.0)
    tied = jnp.where(aff >= pltpu.bitcast(thr, F32), 1.0, 0.0) - above
    need = cap - jnp.sum(above, axis=1, keepdims=True)
    sel = above + tied * jnp.where(_excl_cumsum_lanes(tied) < need, 1.0, 0.0)
    pos = _excl_cumsum_lanes(sel)
    tok = lax.broadcasted_iota(I32, (e, n), 1)
    key = jnp.where(sel > 0.0, (tok - pos.astype(I32)) | VALID_BIT, 0)
    shift = 1
    while shift < n:
        mk = pltpu.roll(key, n - shift, axis=1)
        mt = pltpu.roll(tok, n - shift, axis=1)
        ma = pltpu.roll(aff, n - shift, axis=1)
        take = (mk & shift) != 0
        key = jnp.where(take, mk, jnp.where((key & shift) == 0, key, 0))
        tok = jnp.where(take, mt, tok)
        aff = jnp.where(take, ma, aff)
        shift *= 2
    idx_ref[0] = tok[:, :cap]
    gate_ref[0] = aff[:, :cap]


def _route_call(logits_t, cap):
    b, e, n = logits_t.shape
    return pl.pallas_call(
        functools.partial(_route_kernel, cap=cap),
        out_shape=[jax.ShapeDtypeStruct((b, e, cap), I32), jax.ShapeDtypeStruct((b, e, cap), F32)],
        grid=(b,),
        in_specs=[pl.BlockSpec((1, e, n), lambda bi: (bi, 0, 0))],
        out_specs=[pl.BlockSpec((1, e, cap), lambda bi: (bi, 0, 0))] * 2,
        compiler_params=_params("arbitrary"),
        name="route",
    )(logits_t)


GATHER_UNROLL = 8


def _gather_row(idx_ref, src_ref, dst_ref, s):
    t = idx_ref[0, 0, s]
    dst_ref[pl.ds(pl.multiple_of(s * SUBLANES, SUBLANES), SUBLANES), :] = (
        src_ref[0, pl.ds(pl.multiple_of(t * SUBLANES, SUBLANES), SUBLANES), :])


def _moe_up_kernel(idx_ref, idx_next_ref, src_ref, wg_ref, wu_ref, hid_ref, rows0_ref, rows1_ref, *, cap):
    e = pl.program_id(1)

    @pl.when(e == 0)
    def _():
        def gather(cidx, carry):
            for u in range(GATHER_UNROLL):
                _gather_row(idx_ref, src_ref, rows0_ref, cidx * GATHER_UNROLL + u)
            return carry
        lax.fori_loop(0, cap // GATHER_UNROLL, gather, 0)

    def step(cur_ref, next_ref):
        for s in range(cap):
            _gather_row(idx_next_ref, src_ref, next_ref, s)
        x = _from_token_tiles(cur_ref[...], cap).astype(BF16)
        g = _dot(x, wg_ref[0, 0].astype(BF16))
        u = _dot(x, wu_ref[0, 0].astype(BF16))
        hid_ref[0, 0] = (_silu(g) * u).astype(BF16)

    @pl.when(e % 2 == 0)
    def _():
        step(rows0_ref, rows1_ref)

    @pl.when(e % 2 == 1)
    def _():
        step(rows1_ref, rows0_ref)


def _moe_up_call(idx, src, w_gate, w_up, layer, cap):
    b, nrows, _ = src.shape
    _, e, d, f = w_gate.shape
    assert e % 2 == 0
    idx = idx.reshape(b * e, 1, cap)
    rows = pltpu.VMEM((cap * d // LANES, LANES), F32)
    wspec = pl.BlockSpec((1, 1, d, f), lambda bi, ei: (layer, ei, 0, 0))
    return pl.pallas_call(
        functools.partial(_moe_up_kernel, cap=cap),
        out_shape=jax.ShapeDtypeStruct((b, e, cap, f), BF16),
        grid=(b, e),
        in_specs=[
            pl.BlockSpec((1, 1, cap), lambda bi, ei: (bi * e + ei, 0, 0), memory_space=pltpu.SMEM),
            pl.BlockSpec((1, 1, cap), lambda bi, ei: (bi * e + jnp.minimum(ei + 1, e - 1), 0, 0),
                         memory_space=pltpu.SMEM),
            pl.BlockSpec((1, nrows, LANES), lambda bi, ei: (bi, 0, 0), pipeline_mode=pl.Buffered(1)),
            wspec, wspec,
        ],
        out_specs=pl.BlockSpec((1, 1, cap, f), lambda bi, ei: (bi, ei, 0, 0)),
        scratch_shapes=[rows, rows],
        compiler_params=_params("arbitrary", "arbitrary"),
        name="moe_up",
    )(idx, idx, src, w_gate, w_up)


SCATTER_UNROLL = 8
DOWN_SPLIT = 1


def _moe_down_kernel(idx_ref, gate_ref, hid_ref, hid_next_ref, wd_ref, wd_next_ref, acc_ref, y0_ref, y1_ref,
                     *, cap, rows):
    e = pl.program_id(1)

    def project(h_ref, w_ref, y_ref):
        y = _dot(h_ref[0, 0], w_ref[0, 0].astype(BF16))
        for j in range(rows):
            y_ref[pl.ds(j, cap, stride=rows), :] = y[:, j * LANES:(j + 1) * LANES]

    @pl.when(e == 0)
    def _():
        acc_ref[...] = jnp.zeros_like(acc_ref)
        project(hid_ref, wd_ref, y0_ref)

    def step(cur_ref, next_ref):
        project(hid_next_ref, wd_next_ref, next_ref)
        for base in range(0, cap, SCATTER_UNROLL):
            dsts, vals = [], []
            for s in range(base, base + SCATTER_UNROLL):
                t = idx_ref[0, 0, s]
                dst = pl.ds(pl.multiple_of(t * rows, rows), rows)
                dsts.append(dst)
                vals.append(acc_ref[0, dst, :] + gate_ref[0, 0, s] * cur_ref[s * rows:(s + 1) * rows, :])
            for dst, val in zip(dsts, vals):
                acc_ref[0, dst, :] = val

    @pl.when(e % 2 == 0)
    def _():
        step(y0_ref, y1_ref)

    @pl.when(e % 2 == 1)
    def _():
        step(y1_ref, y0_ref)


def _moe_down_call(idx, gate, hid, wd, layer, n):
    b, e, cap, f = hid.shape
    d = wd.shape[3]
    assert e % 2 == 0
    dw = d // DOWN_SPLIT
    rows = dw // LANES
    nxt = lambda ei: jnp.minimum(ei + 1, e - 1)
    slot = lambda bh, ei: ((bh // DOWN_SPLIT) * e + ei, 0, 0)
    y = pltpu.VMEM((cap * rows, LANES), F32)
    out = pl.pallas_call(
        functools.partial(_moe_down_kernel, cap=cap, rows=rows),
        out_shape=jax.ShapeDtypeStruct((b * DOWN_SPLIT, n * rows, LANES), F32),
        grid=(b * DOWN_SPLIT, e),
        in_specs=[
            pl.BlockSpec((1, 1, cap), slot, memory_space=pltpu.SMEM),
            pl.BlockSpec((1, 1, cap), slot, memory_space=pltpu.SMEM),
            pl.BlockSpec((1, 1, cap, f), lambda bh, ei: (bh // DOWN_SPLIT, ei, 0, 0)),
            pl.BlockSpec((1, 1, cap, f), lambda bh, ei: (bh // DOWN_SPLIT, nxt(ei), 0, 0)),
            pl.BlockSpec((1, 1, f, dw), lambda bh, ei: (layer, ei, 0, bh % DOWN_SPLIT)),
            pl.BlockSpec((1, 1, f, dw), lambda bh, ei: (layer, nxt(ei), 0, bh % DOWN_SPLIT)),
        ],
        out_specs=pl.BlockSpec((1, n * rows, LANES), lambda bh, ei: (bh, 0, 0), pipeline_mode=pl.Buffered(1)),
        scratch_shapes=[y, y],
        compiler_params=_params("arbitrary", "arbitrary"),
        name="moe_down",
    )(idx.reshape(b * e, 1, cap), gate.reshape(b * e, 1, cap), hid, hid, wd, wd)
    return out.reshape(b, DOWN_SPLIT, n * rows, LANES)


def _moe(h2, logits_t, w_gate, w_up, wd, layer):
    b, e, n = logits_t.shape
    cap = 2 * n // e
    idx, gate = _route_call(logits_t, cap)
    hid = _moe_up_call(idx, h2, w_gate, w_up, layer, cap)
    return _moe_down_call(idx, gate, hid, wd, layer, n)


def _join(m_ref, tm):
    rows = m_ref.shape[2] // tm
    return jnp.concatenate([m_ref[0, h, pl.ds(j, tm, stride=rows), :]
                            for h in range(DOWN_SPLIT) for j in range(rows)], axis=1)


def _mix1_kernel(xp_ref, x_ref, xn_ref, mp_ref, m_ref, mn_ref, g2p_ref, sh1_ref, sc1_ref, nmix_ref, win_ref,
                 conv_ref, wout_ref, g1_ref, sh2_ref, sc2_ref, nffn_ref, rt_ref, x3_ref, h2_ref, lg_ref, *, tm):
    i = pl.program_id(1)
    last = pl.num_programs(1) - 1
    d = x_ref.shape[2]
    x2 = x_ref[0] + g2p_ref[0] * _join(m_ref, tm)
    x2p = xp_ref[0] + g2p_ref[0] * _join(mp_ref, SUBLANES)
    x2n = xn_ref[0] + g2p_ref[0] * _join(mn_ref, SUBLANES)
    norm = lambda t: _norm_mod(t, nmix_ref[...], sh1_ref[0], sc1_ref[0])
    hb = jnp.concatenate([jnp.where(i == 0, 0.0, norm(x2p)), norm(x2), jnp.where(i == last, 0.0, norm(x2n))],
                         axis=0).astype(BF16)
    proj = _dot(hb, win_ref[...])
    lo = SUBLANES
    u = proj[:, d:2 * d] * proj[:, 2 * d:]
    cw = conv_ref[...]
    rows = tm + 2 * lo
    cv = (pltpu.roll(u, 1, axis=0)[lo:lo + tm] * cw[0:1] + u[lo:lo + tm] * cw[1:2]
          + pltpu.roll(u, rows - 1, axis=0)[lo:lo + tm] * cw[2:3])
    y = _dot((proj[lo:lo + tm, :d] * cv).astype(BF16), wout_ref[...])
    x3 = x2 + g1_ref[0] * y
    x3_ref[0] = x3
    _ffn_prenorm(x3, nffn_ref, sh2_ref, sc2_ref, rt_ref, h2_ref, lg_ref)


def _mix1_call(x1, moe, g2p, sh1, sc1, nmix, w_in, conv_w, w_out, g1, sh2, sc2, nffn, router_t, tm):
    b, l, d = x1.shape
    tm = min(tm, l)
    nb8 = l // SUBLANES
    r = tm // SUBLANES
    e = router_t.shape[0]
    mr = d // DOWN_SPLIT // LANES
    prev = lambda bi, i: (bi, jnp.maximum(i * r - 1, 0), 0)
    nxt = lambda bi, i: (bi, jnp.minimum((i + 1) * r, nb8 - 1), 0)
    tok = pl.BlockSpec((1, tm, d), lambda bi, i: (bi, i, 0))
    vec = pl.BlockSpec((1, 1, d), lambda bi, i: (bi, 0, 0))
    return pl.pallas_call(
        functools.partial(_mix1_kernel, tm=tm),
        out_shape=[jax.ShapeDtypeStruct((b, l, d), F32), jax.ShapeDtypeStruct((b, l * d // LANES, LANES), F32),
                   jax.ShapeDtypeStruct((b, e, l), F32)],
        grid=(b, l // tm),
        in_specs=[
            pl.BlockSpec((1, SUBLANES, d), prev), tok, pl.BlockSpec((1, SUBLANES, d), nxt),
            pl.BlockSpec((1, DOWN_SPLIT, SUBLANES * mr, LANES), lambda bi, i: (bi, 0, jnp.maximum(i * r - 1, 0), 0)),
            pl.BlockSpec((1, DOWN_SPLIT, tm * mr, LANES), lambda bi, i: (bi, 0, i, 0)),
            pl.BlockSpec((1, DOWN_SPLIT, SUBLANES * mr, LANES),
                         lambda bi, i: (bi, 0, jnp.minimum((i + 1) * r, nb8 - 1), 0)),
            vec, vec, vec, _const_spec(nmix.shape), _const_spec(w_in.shape), _const_spec(conv_w.shape),
            _const_spec(w_out.shape), vec, vec, vec, _const_spec(nffn.shape), _const_spec(router_t.shape),
        ],
        out_specs=[tok, pl.BlockSpec((1, tm * d // LANES, LANES), lambda bi, i: (bi, i, 0)),
                   pl.BlockSpec((1, e, tm), lambda bi, i: (bi, 0, i))],
        compiler_params=_params("arbitrary", "arbitrary"),
        name="mix1",
    )(x1, x1, x1, moe, moe, moe, g2p, sh1, sc1, nmix, w_in, conv_w, w_out, g1, sh2, sc2, nffn, router_t)


def _final_kernel(x_ref, m_ref, g2_ref, nf_ref, o_ref):
    x = x_ref[0] + g2_ref[0] * _join(m_ref, x_ref.shape[1])
    ms = jnp.mean(x * x, axis=-1, keepdims=True)
    o_ref[0] = x * lax.rsqrt(ms + EPS) * nf_ref[...]


def _final_call(x3, moe, g2, nf, tm):
    b, l, d = x3.shape
    tm = min(tm, l)
    mr = d // DOWN_SPLIT // LANES
    tok = pl.BlockSpec((1, tm, d), lambda bi, i: (bi, i, 0))
    return pl.pallas_call(
        _final_kernel,
        out_shape=jax.ShapeDtypeStruct((b, l, d), F32),
        grid=(b, l // tm),
        in_specs=[tok, pl.BlockSpec((1, DOWN_SPLIT, tm * mr, LANES), lambda bi, i: (bi, 0, i, 0)),
                  pl.BlockSpec((1, 1, d), lambda bi, i: (bi, 0, 0)), _const_spec(nf.shape)],
        out_specs=tok,
        compiler_params=_params("arbitrary", "arbitrary"),
        name="final",
    )(x3, moe, g2, nf)


def _layer0_mixer_inputs(ev_w_in, dn_a_log, dn_dt_bias):
    w = HEADS * HEAD_DIM
    qkv_w = 3 * w
    wqkv = ev_w_in[:, :qkv_w].astype(BF16)
    wz = ev_w_in[:, qkv_w:qkv_w + w]
    wg = ev_w_in[:, qkv_w + w:qkv_w + w + 4 * HEADS]
    wp = ev_w_in[:, qkv_w + w + 4 * HEADS:]
    wrest = jnp.concatenate([wz, wp, jnp.pad(wg, ((0, 0), (0, LANES - 4 * HEADS)))], axis=1).astype(BF16)
    pad = jnp.zeros((2 * HEADS,), F32)
    tail = jnp.zeros((LANES - 4 * HEADS,), F32)
    gpar = jnp.stack([jnp.concatenate([pad, dn_a_log.reshape(-1), tail]),
                      jnp.concatenate([pad, dn_dt_bias.reshape(-1), tail])])
    return wqkv, wrest, gpar


def kernel(x, c, ctx, c_ctx, ada_w, ada_b, norm_mix, norm_ffn, norm_final, ev_w_in, dn_conv, dn_a_log, dn_dt_bias,
           dn_norm, pool_w, pool_scale, ev_w_out, sc_w_in, sc_conv, sc_w_out, router, w_gate, w_up, w_down):
    b, n, d = x.shape
    depth = ada_w.shape[0]
    assert depth == 2 and b + 1 <= SUBLANES and n % GRID_W == 0 and n % CHUNK == 0 and ctx.shape[1] % CHUNK == 0

    cc = jnp.concatenate([c, c_ctx[None], jnp.zeros((SUBLANES - b - 1, d), F32)], axis=0)
    mods = _ada_call(cc, ada_w, ada_b)

    def mod(layer, k, rows=slice(0, b)):
        return mods[layer, rows, None, k * d:(k + 1) * d]

    ctx_rows = lambda layer, k: jnp.broadcast_to(mods[layer, b:b + 1, None, k * d:(k + 1) * d], (b, 1, d))
    row = lambda v: v.reshape(1, -1)
    router_t = jnp.swapaxes(router, 1, 2).astype(BF16)

    wqkv, wrest, gpar = _layer0_mixer_inputs(ev_w_in[0], dn_a_log[0], dn_dt_bias[0])
    nmix0 = row(norm_mix[0])
    qc, kc, vc, _, _, gc = _proj0_call(ctx, ctx_rows(0, 0), ctx_rows(0, 1), nmix0, wqkv, wrest, dn_conv[0], gpar, 256)
    ql, kl, vl, zl, pl_in, gl = _proj0_call(x, mod(0, 0), mod(0, 1), nmix0, wqkv, wrest, dn_conv[0], gpar, 512)
    s0 = jnp.zeros((b, 2, HEADS, HEAD_DIM, HEAD_DIM), F32)
    _, _, s_ctx = _delta_scan_call(_delta_prep_call(qc, kc, vc, gc), s0)
    o_f, o_b, _ = _delta_scan_call(_delta_prep_call(ql, kl, vl, gl), s_ctx)
    x1, h2, lg = _out0_call(o_f, o_b, zl, pl_in, x, row(dn_norm[0]), pool_w[0].astype(BF16), row(pool_scale[0]),
                            ev_w_out[0].astype(BF16), mod(0, 2), mod(0, 3), mod(0, 4), row(norm_ffn[0]),
                            router_t[0], 512)
    moe0 = _moe(h2, lg, w_gate, w_up, w_down, 0)

    x3, h2, lg = _mix1_call(x1, moe0, mod(0, 5), mod(1, 0), mod(1, 1), row(norm_mix[1]), sc_w_in[0].astype(BF16),
                            sc_conv[0], sc_w_out[0].astype(BF16), mod(1, 2), mod(1, 3), mod(1, 4),
                            row(norm_ffn[1]), router_t[1], 512)
    moe1 = _moe(h2, lg, w_gate, w_up, w_down, 1)
    return _final_call(x3, moe1, mod(1, 5), row(norm_final), 1024)
```

```python
import functools
import math

import jax
import jax.numpy as jnp
from jax import lax
from jax.experimental import pallas as pl
from jax.experimental.pallas import tpu as pltpu

F32 = jnp.float32
BF16 = jnp.bfloat16
I32 = jnp.int32

EPS = 1e-6
GRID_W = 64
HEADS = 4
HEAD_DIM = 128
CHUNK = 64
POOL_WINDOWS = (2, 4, 8, 16)
POOL_GROUP = 128
LANES = 128
SUBLANES = 8
VMEM_LIMIT = 56 * 1024 * 1024
PROJ_TILE = 512
OUT0_TILE = 512
MIX1_TILE = 512
FINAL_TILE = 1024
VALID_BIT = 1 << 30


def _silu(x):
    return x * jax.nn.sigmoid(x)


def _norm_mod(x, g, shift, scale):
    ms = jnp.mean(x * x, axis=-1, keepdims=True)
    return (x * lax.rsqrt(ms + EPS) * g) * (1.0 + scale) + shift


def _dot(a, b):
    return jnp.dot(a, b, preferred_element_type=F32)


def _dot_nt(a, b):
    return lax.dot_general(a, b, (((1,), (1,)), ((), ())), preferred_element_type=F32)


def _dot_tn(a, b):
    return lax.dot_general(a, b, (((0,), (0,)), ((), ())), preferred_element_type=F32)


def _split3(x):
    hi = x.astype(BF16)
    r = x - hi.astype(F32)
    mid = r.astype(BF16)
    lo = (r - mid.astype(F32)).astype(BF16)
    return hi, mid, lo


def _from_token_tiles(t, rows):
    p = t.shape[0] // rows
    chunks = jnp.swapaxes(t.reshape(rows, p, LANES), 0, 1)
    return jnp.concatenate([chunks[j] for j in range(p)], axis=1)


def _const_spec(shape):
    nd = len(shape)
    return pl.BlockSpec(shape, lambda *_: (0,) * nd, pipeline_mode=pl.Buffered(1))


def _params(*sem):
    return pltpu.CompilerParams(dimension_semantics=sem, vmem_limit_bytes=VMEM_LIMIT)


def _ada_kernel(c_ref, w_ref, b_ref, o_ref):
    s = _silu(c_ref[...])
    o_ref[0] = _dot(s.astype(BF16), w_ref[0].astype(BF16)) + b_ref[0]


def _ada_call(cc, ada_w, ada_b):
    depth, d, n6 = ada_w.shape
    tn = n6 // 4
    return pl.pallas_call(
        _ada_kernel,
        out_shape=jax.ShapeDtypeStruct((depth, SUBLANES, n6), F32),
        grid=(depth, n6 // tn),
        in_specs=[
            pl.BlockSpec((SUBLANES, d), lambda i, j: (0, 0)),
            pl.BlockSpec((1, d, tn), lambda i, j: (i, 0, j)),
            pl.BlockSpec((1, 1, tn), lambda i, j: (i, 0, j)),
        ],
        out_specs=pl.BlockSpec((1, SUBLANES, tn), lambda i, j: (i, 0, j)),
        compiler_params=_params("arbitrary", "arbitrary"),
        name="adaln",
    )(cc, ada_w, ada_b.reshape(depth, 1, n6))


def _proj0_kernel(xp_ref, x_ref, xn_ref, sh_ref, sc_ref, g_ref, wqkv_ref, wrest_ref, conv_ref, gpar_ref,
                  q_ref, k_ref, v_ref, z_ref, p_ref, gt_ref, *, tm):
    i = pl.program_id(1)
    last = pl.num_programs(1) - 1
    norm = lambda t: _norm_mod(t, g_ref[...], sh_ref[0], sc_ref[0])
    hb = jnp.concatenate([jnp.where(i == 0, 0.0, norm(xp_ref[0])), norm(x_ref[0]),
                          jnp.where(i == last, 0.0, norm(xn_ref[0]))], axis=0).astype(BF16)
    proj = _dot(hb, wqkv_ref[...])
    cw = conv_ref[...]
    lo = SUBLANES
    rows = tm + 2 * lo
    a = (pltpu.roll(proj, 1, axis=0)[lo:lo + tm] * cw[0:1] + proj[lo:lo + tm] * cw[1:2]
         + pltpu.roll(proj, rows - 1, axis=0)[lo:lo + tm] * cw[2:3])
    a = _silu(a)
    w = HEADS * HEAD_DIM
    for hd in range(HEADS):
        sl = slice(hd * HEAD_DIM, (hd + 1) * HEAD_DIM)
        qh = a[:, sl]
        kh = a[:, w + hd * HEAD_DIM: w + (hd + 1) * HEAD_DIM]
        qn = qh * lax.rsqrt(jnp.sum(qh * qh, axis=-1, keepdims=True) + EPS) * (HEAD_DIM ** -0.5)
        q_ref[0, :, sl] = qn.astype(BF16)
        k_ref[0, :, sl] = (kh * lax.rsqrt(jnp.sum(kh * kh, axis=-1, keepdims=True) + EPS)).astype(BF16)
    v_ref[0] = a[:, 2 * w:].astype(BF16)
    rest = _dot(hb[lo:lo + tm], wrest_ref[...])
    z_ref[0] = rest[:, :w].astype(BF16)
    p_ref[0] = rest[:, w:2 * w].astype(BF16)
    gates = rest[:, 2 * w:]
    col = lax.broadcasted_iota(I32, (1, LANES), 1)
    xb = gates + gpar_ref[1:2]
    softplus = jnp.maximum(xb, 0.0) + jnp.log1p(jnp.exp(-jnp.abs(xb)))
    log_decay = -jnp.exp(gpar_ref[0:1]) * softplus
    out = jnp.where(col < 2 * HEADS, jax.nn.sigmoid(gates), log_decay)
    gt_ref[0] = out[:, :4 * HEADS]


def _proj0_call(x, shift, scale, gain, wqkv, wrest, conv_w, gpar, tm):
    b, l, d = x.shape
    tm = min(tm, l)
    nt = l // tm
    nb8 = l // SUBLANES
    r = tm // SUBLANES
    w = HEADS * HEAD_DIM
    tok = lambda width: pl.BlockSpec((1, tm, width), lambda bi, i: (bi, i, 0))
    vec = pl.BlockSpec((1, 1, d), lambda bi, i: (bi, 0, 0))
    return pl.pallas_call(
        functools.partial(_proj0_kernel, tm=tm),
        out_shape=[jax.ShapeDtypeStruct((b, l, w), BF16)] * 5 + [jax.ShapeDtypeStruct((b, l, 4 * HEADS), F32)],
        grid=(b, nt),
        in_specs=[
            pl.BlockSpec((1, SUBLANES, d), lambda bi, i: (bi, jnp.maximum(i * r - 1, 0), 0)),
            tok(d),
            pl.BlockSpec((1, SUBLANES, d), lambda bi, i: (bi, jnp.minimum((i + 1) * r, nb8 - 1), 0)),
            vec, vec,
            _const_spec((1, d)),
            _const_spec(wqkv.shape),
            _const_spec(wrest.shape),
            _const_spec(conv_w.shape),
            _const_spec(gpar.shape),
        ],
        out_specs=[tok(w)] * 5 + [tok(4 * HEADS)],
        compiler_params=_params("arbitrary", "arbitrary"),
        name="proj0",
    )(x, x, x, shift, scale, gain, wqkv, wrest, conv_w, gpar)


PREP_CHUNKS = 8
SCAN_CHUNKS = 8


def _stack_masked(x, block_of_lane, nblocks):
    return jnp.concatenate([jnp.where(block_of_lane == h, x, jnp.zeros_like(x)) for h in range(nblocks)], axis=0)


def _delta_prep_kernel(q_ref, k_ref, v_ref, g_ref, u_ref, wk_ref, qd_ref, kdt_ref, qk_ref, cd_ref, *, cps):
    c = CHUNK
    wc = HEADS * c
    wd = HEADS * HEAD_DIM
    ri = lax.broadcasted_iota(I32, (c, c), 0)
    ci = lax.broadcasted_iota(I32, (c, c), 1)
    tri_l = jnp.where(ri >= ci, 1.0, 0.0).astype(BF16)
    tri_u = jnp.where(ri <= ci, 1.0, 0.0).astype(BF16)
    row = lax.broadcasted_iota(I32, (c, wc), 0)
    lane = lax.broadcasted_iota(I32, (c, wc), 1)
    pos = lane % c
    blk_c = lax.broadcasted_iota(I32, (1, wc), 1) // c
    blk_d = lax.broadcasted_iota(I32, (1, wd), 1) // HEAD_DIM
    eye = jnp.where(row == pos, 1.0, 0.0)
    incl = (row >= pos, row <= pos)
    strict = (row > pos, row < pos)
    tot = (c - 1, 0)
    chunks = range(cps)
    heads = range(HEADS)
    tok = [slice(j * c, (j + 1) * c) for j in chunks]

    def spread(cols, first, width):
        full = [jnp.broadcast_to(cols[:, first + hd:first + hd + 1], (c, LANES)) for hd in heads]
        if width == LANES:
            return jnp.concatenate(full, axis=1)
        half = lax.broadcasted_iota(I32, (c, LANES), 1) < width
        return jnp.concatenate([jnp.where(half, full[2 * i], full[2 * i + 1]) for i in range(HEADS // 2)], axis=1)

    k = [k_ref[0, tok[j], :] for j in chunks]
    q = [q_ref[0, tok[j], :] for j in chunks]
    g = [g_ref[0, tok[j], :] for j in chunks]
    kq = [_dot_nt(jnp.concatenate([k[j], q[j]], axis=0), _stack_masked(k[j], blk_d, HEADS)) for j in chunks]
    g3 = [_split3(g[j]) for j in chunks]
    cum = [(sum(_dot(tri_l, p) for p in g3[j]), sum(_dot(tri_u, p) for p in g3[j])) for j in chunks]

    chains = [(j, d) for j in chunks for d in range(2)]
    beta_d, gam_d, decay, a = {}, {}, {}, {}
    for j, d in chains:
        key = (j, d)
        gam_c = spread(cum[j][d], 2 * HEADS + d * HEADS, c)
        gam_d[key] = spread(cum[j][d], 2 * HEADS + d * HEADS, HEAD_DIM)
        beta_c = spread(g[j], d * HEADS, c)
        beta_d[key] = spread(g[j], d * HEADS, HEAD_DIM)
        gam_r = jnp.sum(jnp.where(row == pos, gam_c, 0.0), axis=0, keepdims=True)
        diff = gam_c - gam_r
        decay[key] = jnp.where(incl[d], jnp.exp(jnp.where(incl[d], diff, 0.0)), 0.0)
        a[key] = jnp.where(strict[d], kq[j][:c] * decay[key], 0.0) * beta_c
    p = dict(a)
    tinv = {key: eye - a[key] for key in chains}
    for _ in range(c.bit_length() - 2):
        pb = {key: p[key].astype(BF16) for key in chains}
        p = {key: _dot(pb[key], _stack_masked(pb[key], blk_c, HEADS)) for key in chains}
        pb = {key: p[key].astype(BF16) for key in chains}
        tinv = {key: tinv[key] + _dot(tinv[key].astype(BF16), _stack_masked(pb[key], blk_c, HEADS))
                for key in chains}
    eg, kf, u, wk = {}, {}, {}, {}
    for j, d in chains:
        key = (j, d)
        kf[key] = k[j].astype(F32)
        eg[key] = jnp.exp(gam_d[key])
        tb = tinv[key].astype(BF16)
        rhs_u = (v_ref[0, tok[j], :].astype(F32) * beta_d[key]).astype(BF16)
        rhs_w = (kf[key] * (beta_d[key] * eg[key])).astype(BF16)
        u[key] = _dot(tb, _stack_masked(rhs_u, blk_d, HEADS))
        wk[key] = _dot(tb, _stack_masked(rhs_w, blk_d, HEADS))
    for j, d in chains:
        key = (j, d)
        gtot = gam_d[key][tot[d]:tot[d] + 1, :]
        u_ref[0, d, tok[j], :] = u[key]
        wk_ref[0, d, tok[j], :] = wk[key].astype(BF16)
        qd_ref[0, d, tok[j], :] = (q[j].astype(F32) * eg[key]).astype(BF16)
        qk_ref[0, d, tok[j], :] = (kq[j][c:] * decay[key]).astype(BF16)
        kd = kf[key] * jnp.exp(gtot - gam_d[key])
        for hd in heads:
            kdt_ref[0, d, j, :, hd * c:(hd + 1) * c] = kd[:, hd * HEAD_DIM:(hd + 1) * HEAD_DIM].T.astype(BF16)
        cd = jnp.exp(gtot)
        cd_ref[0, d, j] = jnp.concatenate([cd[:, hd * HEAD_DIM:(hd + 1) * HEAD_DIM] for hd in heads]
                                          + [jnp.zeros((SUBLANES - HEADS, LANES), F32)], axis=0)


def _delta_prep_call(q, k, v, g):
    b, l, w = k.shape
    nc = l // CHUNK
    cps = math.gcd(PREP_CHUNKS, nc)
    t = cps * CHUNK
    tok = lambda width: pl.BlockSpec((1, t, width), lambda bi, n: (bi, n, 0))
    dtok = lambda width: pl.BlockSpec((1, 2, t, width), lambda bi, n: (bi, 0, n, 0))
    return pl.pallas_call(
        functools.partial(_delta_prep_kernel, cps=cps),
        out_shape=[
            jax.ShapeDtypeStruct((b, 2, l, w), F32),
            jax.ShapeDtypeStruct((b, 2, l, w), BF16),
            jax.ShapeDtypeStruct((b, 2, l, w), BF16),
            jax.ShapeDtypeStruct((b, 2, nc, HEAD_DIM, HEADS * CHUNK), BF16),
            jax.ShapeDtypeStruct((b, 2, l, HEADS * CHUNK), BF16),
            jax.ShapeDtypeStruct((b, 2, nc, SUBLANES, LANES), F32),
        ],
        grid=(b, nc // cps),
        in_specs=[tok(w), tok(w), tok(w), tok(4 * HEADS)],
        out_specs=[dtok(w), dtok(w), dtok(w),
                   pl.BlockSpec((1, 2, cps, HEAD_DIM, HEADS * CHUNK), lambda bi, n: (bi, 0, n, 0, 0)),
                   dtok(HEADS * CHUNK),
                   pl.BlockSpec((1, 2, cps, SUBLANES, LANES), lambda bi, n: (bi, 0, n, 0, 0))],
        compiler_params=_params("arbitrary", "arbitrary"),
        name="delta_prep",
    )(q, k, v, g)


def _delta_scan_kernel(*refs, cps):
    ins = (refs[0:6], refs[6:12])
    s0_ref, of_ref, ob_ref, sfin_ref, s_ref = refs[12:]
    outs = (of_ref, ob_ref)
    n = pl.program_id(1)
    c = CHUNK

    @pl.when(n == 0)
    def _():
        s_ref[...] = s0_ref[0]

    combos = [(d, hd) for d in range(2) for hd in range(HEADS)]
    sl = lambda hd: slice(hd * HEAD_DIM, (hd + 1) * HEAD_DIM)
    cs = lambda hd: slice(hd * c, (hd + 1) * c)
    s = {(d, hd): s_ref[d, hd] for d, hd in combos}
    for step in range(cps):
        pos = (step, cps - 1 - step)
        tok = [slice(pos[d] * c, (pos[d] + 1) * c) for d in range(2)]
        m1, w, m2 = {}, {}, {}
        for d, hd in combos:
            u_ref, wk_ref, qd_ref, kdt_ref, qk_ref, cd_ref = ins[d]
            lhs = jnp.concatenate([wk_ref[0, 0, tok[d], sl(hd)], qd_ref[0, 0, tok[d], sl(hd)]], axis=0)
            m1[d, hd] = _dot(lhs, s[d, hd].astype(BF16))
        for d, hd in combos:
            w[d, hd] = (ins[d][0][0, 0, tok[d], sl(hd)] - m1[d, hd][:c]).astype(BF16)
        for d, hd in combos:
            u_ref, wk_ref, qd_ref, kdt_ref, qk_ref, cd_ref = ins[d]
            lhs = jnp.concatenate([qk_ref[0, 0, tok[d], cs(hd)], kdt_ref[0, 0, pos[d], :, cs(hd)]], axis=0)
            m2[d, hd] = _dot(lhs, w[d, hd])
        for d, hd in combos:
            outs[d][0, tok[d], sl(hd)] = (m1[d, hd][c:] + m2[d, hd][:c]).astype(BF16)
            s[d, hd] = ins[d][5][0, 0, pos[d], hd:hd + 1, :] * s[d, hd] + m2[d, hd][c:]
    for d, hd in combos:
        s_ref[d, hd] = s[d, hd]

    @pl.when(n == pl.num_programs(1) - 1)
    def _():
        sfin_ref[0] = s_ref[...]


def _delta_scan_call(prep, s0):
    u, wk, qd, kdt, qk, cd = prep
    b, _, l, w = u.shape
    cps = math.gcd(SCAN_CHUNKS, l // CHUNK)
    t = cps * CHUNK
    ns = l // t

    def specs(d, blk):
        tok = lambda width: pl.BlockSpec((1, 1, t, width), lambda bi, n: (bi, d, blk(n), 0))
        return [tok(w), tok(w), tok(w),
                pl.BlockSpec((1, 1, cps, HEAD_DIM, HEADS * CHUNK), lambda bi, n: (bi, d, blk(n), 0, 0)),
                tok(HEADS * CHUNK),
                pl.BlockSpec((1, 1, cps, SUBLANES, LANES), lambda bi, n: (bi, d, blk(n), 0, 0))]

    st = pl.BlockSpec((1, 2, HEADS, HEAD_DIM, HEAD_DIM), lambda bi, n: (bi, 0, 0, 0, 0))
    return pl.pallas_call(
        functools.partial(_delta_scan_kernel, cps=cps),
        out_shape=[jax.ShapeDtypeStruct((b, l, w), BF16)] * 2
        + [jax.ShapeDtypeStruct((b, 2, HEADS, HEAD_DIM, HEAD_DIM), F32)],
        grid=(b, ns),
        in_specs=specs(0, lambda n: n) + specs(1, lambda n: ns - 1 - n) + [st],
        out_specs=[pl.BlockSpec((1, t, w), lambda bi, n: (bi, n, 0)),
                   pl.BlockSpec((1, t, w), lambda bi, n: (bi, ns - 1 - n, 0)), st],
        scratch_shapes=[pltpu.VMEM((2, HEADS, HEAD_DIM, HEAD_DIM), F32)],
        compiler_params=_params("arbitrary", "arbitrary"),
        name="delta_scan",
    )(*prep, *prep, s0)


def _ffn_prenorm(x, nffn_ref, sh2_ref, sc2_ref, rt_ref, h2_ref, lg_ref):
    h2 = _norm_mod(x, nffn_ref[...], sh2_ref[0], sc2_ref[0])
    tm, d = x.shape
    pieces = d // LANES
    for j in range(pieces):
        h2_ref[0, pl.ds(j, tm, stride=pieces), :] = h2[:, j * LANES:(j + 1) * LANES]
    lg_ref[0] = _dot_nt(rt_ref[...], h2.astype(BF16))


def _out0_kernel(of_ref, ob_ref, z_ref, p_ref, x_ref, band_ref, cnt_ref, onorm_ref, pw_ref, ps_ref, wout_ref,
                 g1_ref, sh2_ref, sc2_ref, nffn_ref, rt_ref, x1_ref, h2_ref, lg_ref):
    o = of_ref[0].astype(F32) + ob_ref[0].astype(F32)
    z = z_ref[0].astype(F32)
    pin = p_ref[0].astype(F32)
    parts = []
    for hd in range(HEADS):
        sl = slice(hd * HEAD_DIM, (hd + 1) * HEAD_DIM)
        oh = o[:, sl]
        ms = jnp.mean(oh * oh, axis=-1, keepdims=True)
        parts.append(oh * lax.rsqrt(ms + EPS) * onorm_ref[...] * _silu(z[:, sl]))
    pr = band_ref.shape[1]
    for gi in range(len(POOL_WINDOWS)):
        sl = slice(gi * POOL_GROUP, (gi + 1) * POOL_GROUP)
        band = band_ref[gi]
        grp = []
        for r0 in range(0, pin.shape[0], pr):
            u = pin[r0:r0 + pr, sl]
            uh = u.astype(BF16)
            ul = (u - uh.astype(F32)).astype(BF16)
            wsum = _dot(band, uh) + _dot(band, ul)
            grp.append(wsum / cnt_ref[gi] - u)
        parts.append(_dot(jnp.concatenate(grp, axis=0).astype(BF16), pw_ref[gi]) * ps_ref[:, sl])
    cat = jnp.concatenate(parts, axis=1).astype(BF16)
    x1 = x_ref[0] + g1_ref[0] * _dot(cat, wout_ref[...])
    x1_ref[0] = x1
    _ffn_prenorm(x1, nffn_ref, sh2_ref, sc2_ref, rt_ref, h2_ref, lg_ref)


POOL_ROWS = 256


def _pool_tables(tm):
    t = jnp.arange(tm)
    seg = t // GRID_W
    loc = t % GRID_W
    bands, cnts = [], []
    for w in POOL_WINDOWS:
        lo = jnp.clip(loc - w // 2, 0, GRID_W)
        hi = jnp.clip(loc + w - w // 2, 0, GRID_W)
        inside = (seg[:, None] == seg[None, :]) & (loc[None, :] >= lo[:, None]) & (loc[None, :] < hi[:, None])
        bands.append(inside.astype(BF16))
        cnts.append((hi - lo).astype(F32)[:, None])
    return jnp.stack(bands), jnp.stack(cnts)


def _out0_call(o_f, o_b, z, pin, x, onorm, pool_w, pool_scale, w_out, g1, sh2, sc2, nffn, router_t, tm):
    b, l, d = x.shape
    tm = min(tm, l)
    w = HEADS * HEAD_DIM
    e = router_t.shape[0]
    band, cnt = _pool_tables(math.gcd(tm, POOL_ROWS))
    tok = lambda width: pl.BlockSpec((1, tm, width), lambda bi, i: (bi, i, 0))
    vec = pl.BlockSpec((1, 1, d), lambda bi, i: (bi, 0, 0))
    return pl.pallas_call(
        _out0_kernel,
        out_shape=[jax.ShapeDtypeStruct((b, l, d), F32), jax.ShapeDtypeStruct((b, l * d // LANES, LANES), F32),
                   jax.ShapeDtypeStruct((b, e, l), F32)],
        grid=(b, l // tm),
        in_specs=[tok(w), tok(w), tok(w), tok(w), tok(d),
                  _const_spec(band.shape), _const_spec(cnt.shape), _const_spec(onorm.shape),
                  _const_spec(pool_w.shape), _const_spec(pool_scale.shape), _const_spec(w_out.shape),
                  vec, vec, vec, _const_spec(nffn.shape), _const_spec(router_t.shape)],
        out_specs=[tok(d), pl.BlockSpec((1, tm * d // LANES, LANES), lambda bi, i: (bi, i, 0)),
                   pl.BlockSpec((1, e, tm), lambda bi, i: (bi, 0, i))],
        compiler_params=_params("arbitrary", "arbitrary"),
        name="out0",
    )(o_f, o_b, z, pin, x, band, cnt, onorm, pool_w, pool_scale, w_out, g1, sh2, sc2, nffn, router_t)


def _excl_cumsum_lanes(m):
    rows, n = m.shape
    ri = lax.broadcasted_iota(I32, (LANES, LANES), 0)
    ci = lax.broadcasted_iota(I32, (LANES, LANES), 1)
    upper = jnp.where(ri <= ci, 1.0, 0.0).astype(BF16)
    carry = jnp.zeros((rows, 1), F32)
    outs = []
    for blk in range(n // LANES):
        x = m[:, blk * LANES:(blk + 1) * LANES]
        inc = _dot(x.astype(BF16), upper)
        outs.append(inc - x + carry)
        carry = carry + inc[:, LANES - 1:LANES]
    return jnp.concatenate(outs, axis=1)


def _route_kernel(lg_ref, idx_ref, gate_ref, *, cap):
    x = lg_ref[0]
    e, n = x.shape
    ex = jnp.exp(x - jnp.max(x, axis=0, keepdims=True))
    aff = ex / jnp.sum(ex, axis=0, keepdims=True)

    def count_ge(bits):
        return jnp.sum(jnp.where(aff >= pltpu.bitcast(bits, F32), 1.0, 0.0), axis=1, keepdims=True)

    def bisect(_, lohi):
        lo, hi = lohi
        mid = lo + ((hi - lo + 1) >> 1)
        ok = count_ge(mid) >= cap
        return jnp.where(ok, mid, lo), jnp.where(ok, hi, mid - 1)

    lo0 = jnp.zeros((e, 1), I32)
    hi0 = jnp.full((e, 1), 0x7F800000, I32)
    thr, _ = lax.fori_loop(0, 31, bisect, (lo0, hi0))
    above = jnp.where(aff >= pltpu.bitcast(thr + 1, F32), 1.0, 0.0)
    tied = jnp.where(aff >= pltpu.bitcast(thr, F32), 1.0, 0.0) - above
    need = cap - jnp.sum(above, axis=1, keepdims=True)
    sel = above + tied * jnp.where(_excl_cumsum_lanes(tied) < need, 1.0, 0.0)
    pos = _excl_cumsum_lanes(sel)
    tok = lax.broadcasted_iota(I32, (e, n), 1)
    key = jnp.where(sel > 0.0, (tok - pos.astype(I32)) | VALID_BIT, 0)
    shift = 1
    while shift < n:
        mk = pltpu.roll(key, n - shift, axis=1)
        mt = pltpu.roll(tok, n - shift, axis=1)
        ma = pltpu.roll(aff, n - shift, axis=1)
        take = (mk & shift) != 0
        key = jnp.where(take, mk, jnp.where((key & shift) == 0, key, 0))
        tok = jnp.where(take, mt, tok)
        aff = jnp.where(take, ma, aff)
        shift *= 2
    idx_ref[0] = tok[:, :cap]
    gate_ref[0] = aff[:, :cap]


def _route_call(logits_t, cap):
    b, e, n = logits_t.shape
    return pl.pallas_call(
        functools.partial(_route_kernel, cap=cap),
        out_shape=[jax.ShapeDtypeStruct((b, e, cap), I32), jax.ShapeDtypeStruct((b, e, cap), F32)],
        grid=(b,),
        in_specs=[pl.BlockSpec((1, e, n), lambda bi: (bi, 0, 0))],
        out_specs=[pl.BlockSpec((1, e, cap), lambda bi: (bi, 0, 0))] * 2,
        compiler_params=_params("arbitrary"),
        name="route",
    )(logits_t)


GATHER_UNROLL = 8


def _gather_row(idx_ref, src_ref, dst_ref, s):
    t = idx_ref[0, 0, s]
    dst_ref[pl.ds(pl.multiple_of(s * SUBLANES, SUBLANES), SUBLANES), :] = (
        src_ref[0, pl.ds(pl.multiple_of(t * SUBLANES, SUBLANES), SUBLANES), :])


def _moe_up_kernel(idx_ref, idx_next_ref, src_ref, wg_ref, wu_ref, hid_ref, rows0_ref, rows1_ref, *, cap):
    e = pl.program_id(1)

    @pl.when(e == 0)
    def _():
        def gather(cidx, carry):
            for u in range(GATHER_UNROLL):
                _gather_row(idx_ref, src_ref, rows0_ref, cidx * GATHER_UNROLL + u)
            return carry
        lax.fori_loop(0, cap // GATHER_UNROLL, gather, 0)

    def step(cur_ref, next_ref):
        for s in range(cap):
            _gather_row(idx_next_ref, src_ref, next_ref, s)
        x = _from_token_tiles(cur_ref[...], cap).astype(BF16)
        g = _dot(x, wg_ref[0, 0].astype(BF16))
        u = _dot(x, wu_ref[0, 0].astype(BF16))
        hid_ref[0, 0] = (_silu(g) * u).astype(BF16)

    @pl.when(e % 2 == 0)
    def _():
        step(rows0_ref, rows1_ref)

    @pl.when(e % 2 == 1)
    def _():
        step(rows1_ref, rows0_ref)


def _moe_up_call(idx, src, w_gate, w_up, layer, cap):
    b, nrows, _ = src.shape
    _, e, d, f = w_gate.shape
    assert e % 2 == 0
    idx = idx.reshape(b * e, 1, cap)
    rows = pltpu.VMEM((cap * d // LANES, LANES), F32)
    wspec = pl.BlockSpec((1, 1, d, f), lambda bi, ei: (layer, ei, 0, 0))
    return pl.pallas_call(
        functools.partial(_moe_up_kernel, cap=cap),
        out_shape=jax.ShapeDtypeStruct((b, e, cap, f), BF16),
        grid=(b, e),
        in_specs=[
            pl.BlockSpec((1, 1, cap), lambda bi, ei: (bi * e + ei, 0, 0), memory_space=pltpu.SMEM),
            pl.BlockSpec((1, 1, cap), lambda bi, ei: (bi * e + jnp.minimum(ei + 1, e - 1), 0, 0),
                         memory_space=pltpu.SMEM),
            pl.BlockSpec((1, nrows, LANES), lambda bi, ei: (bi, 0, 0), pipeline_mode=pl.Buffered(1)),
            wspec, wspec,
        ],
        out_specs=pl.BlockSpec((1, 1, cap, f), lambda bi, ei: (bi, ei, 0, 0)),
        scratch_shapes=[rows, rows],
        compiler_params=_params("arbitrary", "arbitrary"),
        name="moe_up",
    )(idx, idx, src, w_gate, w_up)


SCATTER_UNROLL = 8
DOWN_SPLIT = 1


def _moe_down_kernel(idx_ref, gate_ref, hid_ref, hid_next_ref, wd_ref, wd_next_ref, acc_ref, y0_ref, y1_ref,
                     *, cap, rows):
    e = pl.program_id(1)

    def project(h_ref, w_ref, y_ref):
        y = _dot(h_ref[0, 0], w_ref[0, 0].astype(BF16))
        for j in range(rows):
            y_ref[pl.ds(j, cap, stride=rows), :] = y[:, j * LANES:(j + 1) * LANES]

    @pl.when(e == 0)
    def _():
        acc_ref[...] = jnp.zeros_like(acc_ref)
        project(hid_ref, wd_ref, y0_ref)

    def step(cur_ref, next_ref):
        project(hid_next_ref, wd_next_ref, next_ref)
        for base in range(0, cap, SCATTER_UNROLL):
            dsts, vals = [], []
            for s in range(base, base + SCATTER_UNROLL):
                t = idx_ref[0, 0, s]
                dst = pl.ds(pl.multiple_of(t * rows, rows), rows)
                dsts.append(dst)
                vals.append(acc_ref[0, dst, :] + gate_ref[0, 0, s] * cur_ref[s * rows:(s + 1) * rows, :])
            for dst, val in zip(dsts, vals):
                acc_ref[0, dst, :] = val

    @pl.when(e % 2 == 0)
    def _():
        step(y0_ref, y1_ref)

    @pl.when(e % 2 == 1)
    def _():
        step(y1_ref, y0_ref)


def _moe_down_call(idx, gate, hid, wd, layer, n):
    b, e, cap, f = hid.shape
    d = wd.shape[3]
    assert e % 2 == 0
    dw = d // DOWN_SPLIT
    rows = dw // LANES
    nxt = lambda ei: jnp.minimum(ei + 1, e - 1)
    slot = lambda bh, ei: ((bh // DOWN_SPLIT) * e + ei, 0, 0)
    y = pltpu.VMEM((cap * rows, LANES), F32)
    out = pl.pallas_call(
        functools.partial(_moe_down_kernel, cap=cap, rows=rows),
        out_shape=jax.ShapeDtypeStruct((b * DOWN_SPLIT, n * rows, LANES), F32),
        grid=(b * DOWN_SPLIT, e),
        in_specs=[
            pl.BlockSpec((1, 1, cap), slot, memory_space=pltpu.SMEM),
            pl.BlockSpec((1, 1, cap), slot, memory_space=pltpu.SMEM),
            pl.BlockSpec((1, 1, cap, f), lambda bh, ei: (bh // DOWN_SPLIT, 0, 0, 0)),
            pl.BlockSpec((1, 1, cap, f), lambda bh, ei: (bh // DOWN_SPLIT, nxt(ei), 0, 0)),
            pl.BlockSpec((1, 1, f, dw), lambda bh, ei: (layer, 0, 0, bh % DOWN_SPLIT)),
            pl.BlockSpec((1, 1, f, dw), lambda bh, ei: (layer, nxt(ei), 0, bh % DOWN_SPLIT)),
        ],
        out_specs=pl.BlockSpec((1, n * rows, LANES), lambda bh, ei: (bh, 0, 0), pipeline_mode=pl.Buffered(1)),
        scratch_shapes=[y, y],
        compiler_params=_params("arbitrary", "arbitrary"),
        name="moe_down",
    )(idx.reshape(b * e, 1, cap), gate.reshape(b * e, 1, cap), hid, hid, wd, wd)
    return out.reshape(b, DOWN_SPLIT, n * rows, LANES)


def _moe(h2, logits_t, w_gate, w_up, wd, layer):
    b, e, n = logits_t.shape
    cap = 2 * n // e
    idx, gate = _route_call(logits_t, cap)
    hid = _moe_up_call(idx, h2, w_gate, w_up, layer, cap)
    return _moe_down_call(idx, gate, hid, wd, layer, n)


def _join(m_ref, tm):
    rows = m_ref.shape[2] // tm
    return jnp.concatenate([m_ref[0, h, pl.ds(j, tm, stride=rows), :]
                            for h in range(DOWN_SPLIT) for j in range(rows)], axis=1)


def _mix1_kernel(xp_ref, x_ref, xn_ref, mp_ref, m_ref, mn_ref, g2p_ref, sh1_ref, sc1_ref, nmix_ref, win_ref,
                 conv_ref, wout_ref, g1_ref, sh2_ref, sc2_ref, nffn_ref, rt_ref, x3_ref, h2_ref, lg_ref, *, tm):
    i = pl.program_id(1)
    last = pl.num_programs(1) - 1
    d = x_ref.shape[2]
    x2 = x_ref[0] + g2p_ref[0] * _join(m_ref, tm)
    x2p = xp_ref[0] + g2p_ref[0] * _join(mp_ref, SUBLANES)
    x2n = xn_ref[0] + g2p_ref[0] * _join(mn_ref, SUBLANES)
    norm = lambda t: _norm_mod(t, nmix_ref[...], sh1_ref[0], sc1_ref[0])
    hb = jnp.concatenate([jnp.where(i == 0, 0.0, norm(x2p)), norm(x2), jnp.where(i == last, 0.0, norm(x2n))],
                         axis=0).astype(BF16)
    proj = _dot(hb, win_ref[...])
    lo = SUBLANES
    u = proj[:, d:2 * d] * proj[:, 2 * d:]
    cw = conv_ref[...]
    rows = tm + 2 * lo
    cv = (pltpu.roll(u, 1, axis=0)[lo:lo + tm] * cw[0:1] + u[lo:lo + tm] * cw[1:2]
          + pltpu.roll(u, rows - 1, axis=0)[lo:lo + tm] * cw[2:3])
    y = _dot((proj[lo:lo + tm, :d] * cv).astype(BF16), wout_ref[...])
    x3 = x2 + g1_ref[0] * y
    x3_ref[0] = x3
    _ffn_prenorm(x3, nffn_ref, sh2_ref, sc2_ref, rt_ref, h2_ref, lg_ref)


def _mix1_call(x1, moe, g2p, sh1, sc1, nmix, w_in, conv_w, w_out, g1, sh2, sc2, nffn, router_t, tm):
    b, l, d = x1.shape
    tm = min(tm, l)
    nb8 = l // SUBLANES
    r = tm // SUBLANES
    e = router_t.shape[0]
    mr = d // DOWN_SPLIT // LANES
    prev = lambda bi, i: (bi, jnp.maximum(i * r - 1, 0), 0)
    nxt = lambda bi, i: (bi, jnp.minimum((i + 1) * r, nb8 - 1), 0)
    tok = pl.BlockSpec((1, tm, d), lambda bi, i: (bi, i, 0))
    vec = pl.BlockSpec((1, 1, d), lambda bi, i: (bi, 0, 0))
    return pl.pallas_call(
        functools.partial(_mix1_kernel, tm=tm),
        out_shape=[jax.ShapeDtypeStruct((b, l, d), F32), jax.ShapeDtypeStruct((b, l * d // LANES, LANES), F32),
                   jax.ShapeDtypeStruct((b, e, l), F32)],
        grid=(b, l // tm),
        in_specs=[
            pl.BlockSpec((1, SUBLANES, d), prev), tok, pl.BlockSpec((1, SUBLANES, d), nxt),
            pl.BlockSpec((1, DOWN_SPLIT, SUBLANES * mr, LANES), lambda bi, i: (bi, 0, jnp.maximum(i * r - 1, 0), 0)),
            pl.BlockSpec((1, DOWN_SPLIT, tm * mr, LANES), lambda bi, i: (bi, 0, i, 0)),
            pl.BlockSpec((1, DOWN_SPLIT, SUBLANES * mr, LANES),
                         lambda bi, i: (bi, 0, jnp.minimum((i + 1) * r, nb8 - 1), 0)),
            vec, vec, vec, _const_spec(nmix.shape), _const_spec(w_in.shape), _const_spec(conv_w.shape),
            _const_spec(w_out.shape), vec, vec, vec, _const_spec(nffn.shape), _const_spec(router_t.shape),
        ],
        out_specs=[tok, pl.BlockSpec((1, tm * d // LANES, LANES), lambda bi, i: (bi, i, 0)),
                   pl.BlockSpec((1, e, tm), lambda bi, i: (bi, 0, i))],
        compiler_params=_params("arbitrary", "arbitrary"),
        name="mix1",
    )(x1, x1, x1, moe, moe, moe, g2p, sh1, sc1, nmix, w_in, conv_w, w_out, g1, sh2, sc2, nffn, router_t)


def _final_kernel(x_ref, m_ref, g2_ref, nf_ref, o_ref):
    x = x_ref[0] + g2_ref[0] * _join(m_ref, x_ref.shape[1])
    ms = jnp.mean(x * x, axis=-1, keepdims=True)
    o_ref[0] = x * lax.rsqrt(ms + EPS) * nf_ref[...]


def _final_call(x3, moe, g2, nf, tm):
    b, l, d = x3.shape
    tm = min(tm, l)
    mr = d // DOWN_SPLIT // LANES
    tok = pl.BlockSpec((1, tm, d), lambda bi, i: (bi, i, 0))
    return pl.pallas_call(
        _final_kernel,
        out_shape=jax.ShapeDtypeStruct((b, l, d), F32),
        grid=(b, l // tm),
        in_specs=[tok, pl.BlockSpec((1, DOWN_SPLIT, tm * mr, LANES), lambda bi, i: (bi, 0, i, 0)),
                  pl.BlockSpec((1, 1, d), lambda bi, i: (bi, 0, 0)), _const_spec(nf.shape)],
        out_specs=tok,
        compiler_params=_params("arbitrary", "arbitrary"),
        name="final",
    )(x3, moe, g2, nf)


def _layer0_mixer_inputs(ev_w_in, dn_a_log, dn_dt_bias):
    w = HEADS * HEAD_DIM
    qkv_w = 3 * w
    wqkv = ev_w_in[:, :qkv_w].astype(BF16)
    wz = ev_w_in[:, qkv_w:qkv_w + w]
    wg = ev_w_in[:, qkv_w + w:qkv_w + w + 4 * HEADS]
    wp = ev_w_in[:, qkv_w + w + 4 * HEADS:]
    wrest = jnp.concatenate([wz, wp, jnp.pad(wg, ((0, 0), (0, LANES - 4 * HEADS)))], axis=1).astype(BF16)
    pad = jnp.zeros((2 * HEADS,), F32)
    tail = jnp.zeros((LANES - 4 * HEADS,), F32)
    gpar = jnp.stack([jnp.concatenate([pad, dn_a_log.reshape(-1), tail]),
                      jnp.concatenate([pad, dn_dt_bias.reshape(-1), tail])])
    return wqkv, wrest, gpar


def kernel(x, c, ctx, c_ctx, ada_w, ada_b, norm_mix, norm_ffn, norm_final, ev_w_in, dn_conv, dn_a_log, dn_dt_bias,
           dn_norm, pool_w, pool_scale, ev_w_out, sc_w_in, sc_conv, sc_w_out, router, w_gate, w_up, w_down):
    b, n, d = x.shape
    depth = ada_w.shape[0]
    assert depth == 2 and b + 1 <= SUBLANES and n % GRID_W == 0 and n % CHUNK == 0 and ctx.shape[1] % CHUNK == 0

    cc = jnp.concatenate([c, c_ctx[None], jnp.zeros((SUBLANES - b - 1, d), F32)], axis=0)
    mods = _ada_call(cc, ada_w, ada_b)

    def mod(layer, k, rows=slice(0, b)):
        return mods[layer, rows, None, k * d:(k + 1) * d]

    ctx_rows = lambda layer, k: jnp.broadcast_to(mods[layer, b:b + 1, None, k * d:(k + 1) * d], (b, 1, d))
    row = lambda v: v.reshape(1, -1)
    router_t = jnp.swapaxes(router, 1, 2).astype(BF16)

    wqkv, wrest, gpar = _layer0_mixer_inputs(ev_w_in[0], dn_a_log[0], dn_dt_bias[0])
    nmix0 = row(norm_mix[0])
    qc, kc, vc, _, _, gc = _proj0_call(ctx, ctx_rows(0, 0), ctx_rows(0, 1), nmix0, wqkv, wrest, dn_conv[0], gpar,
                                      PROJ_TILE)
    ql, kl, vl, zl, pl_in, gl = _proj0_call(x, mod(0, 0), mod(0, 1), nmix0, wqkv, wrest, dn_conv[0], gpar, PROJ_TILE)
    s0 = jnp.zeros((b, 2, HEADS, HEAD_DIM, HEAD_DIM), F32)
    _, _, s_ctx = _delta_scan_call(_delta_prep_call(qc, kc, vc, gc), s0)
    o_f, o_b, _ = _delta_scan_call(_delta_prep_call(ql, kl, vl, gl), s_ctx)
    x1, h2, lg = _out0_call(o_f, o_b, zl, pl_in, x, row(dn_norm[0]), pool_w[0].astype(BF16), row(pool_scale[0]),
                            ev_w_out[0].astype(BF16), mod(0, 2), mod(0, 3), mod(0, 4), row(norm_ffn[0]),
                            router_t[0], OUT0_TILE)
    moe0 = _moe(h2, lg, w_gate, w_up, w_down, 0)

    x3, h2, lg = _mix1_call(x1, moe0, mod(0, 5), mod(1, 0), mod(1, 1), row(norm_mix[1]), sc_w_in[0].astype(BF16),
                            sc_conv[0], sc_w_out[0].astype(BF16), mod(1, 2), mod(1, 3), mod(1, 4),
                            row(norm_ffn[1]), router_t[1], MIX1_TILE)
    moe1 = _moe(h2, lg, w_gate, w_up, w_down, 1)
    return _final_call(x3, moe1, mod(1, 5), row(norm_final), FINAL_TILE)
```

```python
import functools
import math

import jax
import jax.numpy as jnp
from jax import lax
from jax.experimental import pallas as pl
from jax.experimental.pallas import tpu as pltpu

F32 = jnp.float32
BF16 = jnp.bfloat16
I32 = jnp.int32

EPS = 1e-6
GRID_W = 64
HEADS = 4
HEAD_DIM = 128
CHUNK = 64
POOL_WINDOWS = (2, 4, 8, 16)
POOL_GROUP = 128
LANES = 128
SUBLANES = 8
VMEM_LIMIT = 56 * 1024 * 1024
PROJ_TILE = 512
OUT0_TILE = 512
MIX1_TILE = 512
FINAL_TILE = 1024
VALID_BIT = 1 << 30


def _silu(x):
    return x * jax.nn.sigmoid(x)


def _norm_mod(x, g, shift, scale):
    ms = jnp.mean(x * x, axis=-1, keepdims=True)
    return (x * lax.rsqrt(ms + EPS) * g) * (1.0 + scale) + shift


def _dot(a, b):
    return jnp.dot(a, b, preferred_element_type=F32)


def _dot_nt(a, b):
    return lax.dot_general(a, b, (((1,), (1,)), ((), ())), preferred_element_type=F32)


def _dot_tn(a, b):
    return lax.dot_general(a, b, (((0,), (0,)), ((), ())), preferred_element_type=F32)


def _split3(x):
    hi = x.astype(BF16)
    r = x - hi.astype(F32)
    mid = r.astype(BF16)
    lo = (r - mid.astype(F32)).astype(BF16)
    return hi, mid, lo


def _from_token_tiles(t, rows):
    p = t.shape[0] // rows
    chunks = jnp.swapaxes(t.reshape(rows, p, LANES), 0, 1)
    return jnp.concatenate([chunks[j] for j in range(p)], axis=1)


def _const_spec(shape):
    nd = len(shape)
    return pl.BlockSpec(shape, lambda *_: (0,) * nd, pipeline_mode=pl.Buffered(1))


def _params(*sem):
    return pltpu.CompilerParams(dimension_semantics=sem, vmem_limit_bytes=VMEM_LIMIT)


def _ada_kernel(c_ref, w_ref, b_ref, o_ref):
    s = _silu(c_ref[...])
    o_ref[0] = _dot(s.astype(BF16), w_ref[0].astype(BF16)) + b_ref[0]


def _ada_call(cc, ada_w, ada_b):
    depth, d, n6 = ada_w.shape
    tn = n6 // 4
    return pl.pallas_call(
        _ada_kernel,
        out_shape=jax.ShapeDtypeStruct((depth, SUBLANES, n6), F32),
        grid=(depth, n6 // tn),
        in_specs=[
            pl.BlockSpec((SUBLANES, d), lambda i, j: (0, 0)),
            pl.BlockSpec((1, d, tn), lambda i, j: (i, 0, j)),
            pl.BlockSpec((1, 1, tn), lambda i, j: (i, 0, j)),
        ],
        out_specs=pl.BlockSpec((1, SUBLANES, tn), lambda i, j: (i, 0, j)),
        compiler_params=_params("arbitrary", "arbitrary"),
        name="adaln",
    )(cc, ada_w, ada_b.reshape(depth, 1, n6))


def _proj0_kernel(xp_ref, x_ref, xn_ref, sh_ref, sc_ref, g_ref, wqkv_ref, wrest_ref, conv_ref, gpar_ref,
                  q_ref, k_ref, v_ref, z_ref, p_ref, gt_ref, *, tm):
    i = pl.program_id(1)
    last = pl.num_programs(1) - 1
    norm = lambda t: _norm_mod(t, g_ref[...], sh_ref[0], sc_ref[0])
    hb = jnp.concatenate([jnp.where(i == 0, 0.0, norm(xp_ref[0])), norm(x_ref[0]),
                          jnp.where(i == last, 0.0, norm(xn_ref[0]))], axis=0).astype(BF16)
    proj = _dot(hb, wqkv_ref[...])
    cw = conv_ref[...]
    lo = SUBLANES
    rows = tm + 2 * lo
    a = (pltpu.roll(proj, 1, axis=0)[lo:lo + tm] * cw[0:1] + proj[lo:lo + tm] * cw[1:2]
         + pltpu.roll(proj, rows - 1, axis=0)[lo:lo + tm] * cw[2:3])
    a = _silu(a)
    w = HEADS * HEAD_DIM
    for hd in range(HEADS):
        sl = slice(hd * HEAD_DIM, (hd + 1) * HEAD_DIM)
        qh = a[:, sl]
        kh = a[:, w + hd * HEAD_DIM: w + (hd + 1) * HEAD_DIM]
        qn = qh * lax.rsqrt(jnp.sum(qh * qh, axis=-1, keepdims=True) + EPS) * (HEAD_DIM ** -0.5)
        q_ref[0, :, sl] = qn.astype(BF16)
        k_ref[0, :, sl] = (kh * lax.rsqrt(jnp.sum(kh * kh, axis=-1, keepdims=True) + EPS)).astype(BF16)
    v_ref[0] = a[:, 2 * w:].astype(BF16)
    rest = _dot(hb[lo:lo + tm], wrest_ref[...])
    z_ref[0] = rest[:, :w].astype(BF16)
    p_ref[0] = rest[:, w:2 * w].astype(BF16)
    gates = rest[:, 2 * w:]
    col = lax.broadcasted_iota(I32, (1, LANES), 1)
    xb = gates + gpar_ref[1:2]
    softplus = jnp.maximum(xb, 0.0) + jnp.log1p(jnp.exp(-jnp.abs(xb)))
    log_decay = -jnp.exp(gpar_ref[0:1]) * softplus
    out = jnp.where(col < 2 * HEADS, jax.nn.sigmoid(gates), log_decay)
    gt_ref[0] = out[:, :4 * HEADS]


def _proj0_call(x, shift, scale, gain, wqkv, wrest, conv_w, gpar, tm):
    b, l, d = x.shape
    tm = min(tm, l)
    nt = l // tm
    nb8 = l // SUBLANES
    r = tm // SUBLANES
    w = HEADS * HEAD_DIM
    tok = lambda width: pl.BlockSpec((1, tm, width), lambda bi, i: (bi, i, 0))
    vec = pl.BlockSpec((1, 1, d), lambda bi, i: (bi, 0, 0))
    return pl.pallas_call(
        functools.partial(_proj0_kernel, tm=tm),
        out_shape=[jax.ShapeDtypeStruct((b, l, w), BF16)] * 5 + [jax.ShapeDtypeStruct((b, l, 4 * HEADS), F32)],
        grid=(b, nt),
        in_specs=[
            pl.BlockSpec((1, SUBLANES, d), lambda bi, i: (bi, jnp.maximum(i * r - 1, 0), 0)),
            tok(d),
            pl.BlockSpec((1, SUBLANES, d), lambda bi, i: (bi, jnp.minimum((i + 1) * r, nb8 - 1), 0)),
            vec, vec,
            _const_spec((1, d)),
            _const_spec(wqkv.shape),
            _const_spec(wrest.shape),
            _const_spec(conv_w.shape),
            _const_spec(gpar.shape),
        ],
        out_specs=[tok(w)] * 5 + [tok(4 * HEADS)],
        compiler_params=_params("arbitrary", "arbitrary"),
        name="proj0",
    )(x, x, x, shift, scale, gain, wqkv, wrest, conv_w, gpar)


PREP_CHUNKS = 8
SCAN_CHUNKS = 16


def _stack_masked(x, block_of_lane, nblocks):
    return jnp.concatenate([jnp.where(block_of_lane == h, x, jnp.zeros_like(x)) for h in range(nblocks)], axis=0)


def _delta_prep_kernel(q_ref, k_ref, v_ref, g_ref, u_ref, wk_ref, qd_ref, kdt_ref, qk_ref, cd_ref, *, cps):
    c = CHUNK
    wc = HEADS * c
    wd = HEADS * HEAD_DIM
    ri = lax.broadcasted_iota(I32, (c, c), 0)
    ci = lax.broadcasted_iota(I32, (c, c), 1)
    tri_l = jnp.where(ri >= ci, 1.0, 0.0).astype(BF16)
    tri_u = jnp.where(ri <= ci, 1.0, 0.0).astype(BF16)
    row = lax.broadcasted_iota(I32, (c, wc), 0)
    lane = lax.broadcasted_iota(I32, (c, wc), 1)
    pos = lane % c
    blk_c = lax.broadcasted_iota(I32, (1, wc), 1) // c
    blk_d = lax.broadcasted_iota(I32, (1, wd), 1) // HEAD_DIM
    eye = jnp.where(row == pos, 1.0, 0.0)
    incl = (row >= pos, row <= pos)
    strict = (row > pos, row < pos)
    tot = (c - 1, 0)
    chunks = range(cps)
    heads = range(HEADS)
    tok = [slice(j * c, (j + 1) * c) for j in chunks]

    def spread(cols, first, width):
        full = [jnp.broadcast_to(cols[:, first + hd:first + hd + 1], (c, LANES)) for hd in heads]
        if width == LANES:
            return jnp.concatenate(full, axis=1)
        half = lax.broadcasted_iota(I32, (c, LANES), 1) < width
        return jnp.concatenate([jnp.where(half, full[2 * i], full[2 * i + 1]) for i in range(HEADS // 2)], axis=1)

    k = [k_ref[0, tok[j], :] for j in chunks]
    q = [q_ref[0, tok[j], :] for j in chunks]
    g = [g_ref[0, tok[j], :] for j in chunks]
    kq = [_dot_nt(jnp.concatenate([k[j], q[j]], axis=0), _stack_masked(k[j], blk_d, HEADS)) for j in chunks]
    g3 = [_split3(g[j]) for j in chunks]
    cum = [(sum(_dot(tri_l, p) for p in g3[j]), sum(_dot(tri_u, p) for p in g3[j])) for j in chunks]

    chains = [(j, d) for j in chunks for d in range(2)]
    beta_d, gam_d, decay, a = {}, {}, {}, {}
    for j, d in chains:
        key = (j, d)
        gam_c = spread(cum[j][d], 2 * HEADS + d * HEADS, c)
        gam_d[key] = spread(cum[j][d], 2 * HEADS + d * HEADS, HEAD_DIM)
        beta_c = spread(g[j], d * HEADS, c)
        beta_d[key] = spread(g[j], d * HEADS, HEAD_DIM)
        gam_r = jnp.sum(jnp.where(row == pos, gam_c, 0.0), axis=0, keepdims=True)
        diff = gam_c - gam_r
        decay[key] = jnp.where(incl[d], jnp.exp(jnp.where(incl[d], diff, 0.0)), 0.0)
        a[key] = jnp.where(strict[d], kq[j][:c] * decay[key], 0.0) * beta_c
    p = dict(a)
    tinv = {key: eye - a[key] for key in chains}
    for _ in range(c.bit_length() - 2):
        pb = {key: p[key].astype(BF16) for key in chains}
        p = {key: _dot(pb[key], _stack_masked(pb[key], blk_c, HEADS)) for key in chains}
        pb = {key: p[key].astype(BF16) for key in chains}
        tinv = {key: tinv[key] + _dot(tinv[key].astype(BF16), _stack_masked(pb[key], blk_c, HEADS))
                for key in chains}
    eg, kf, u, wk = {}, {}, {}, {}
    for j, d in chains:
        key = (j, d)
        kf[key] = k[j].astype(F32)
        eg[key] = jnp.exp(gam_d[key])
        tb = tinv[key].astype(BF16)
        rhs_u = (v_ref[0, tok[j], :].astype(F32) * beta_d[key]).astype(BF16)
        rhs_w = (kf[key] * (beta_d[key] * eg[key])).astype(BF16)
        u[key] = _dot(tb, _stack_masked(rhs_u, blk_d, HEADS))
        wk[key] = _dot(tb, _stack_masked(rhs_w, blk_d, HEADS))
    for j, d in chains:
        key = (j, d)
        gtot = gam_d[key][tot[d]:tot[d] + 1, :]
        u_ref[0, d, tok[j], :] = u[key]
        wk_ref[0, d, tok[j], :] = wk[key].astype(BF16)
        qd_ref[0, d, tok[j], :] = (q[j].astype(F32) * eg[key]).astype(BF16)
        qk_ref[0, d, tok[j], :] = (kq[j][c:] * decay[key]).astype(BF16)
        kd = kf[key] * jnp.exp(gtot - gam_d[key])
        for hd in heads:
            kdt_ref[0, d, j, :, hd * c:(hd + 1) * c] = kd[:, hd * HEAD_DIM:(hd + 1) * HEAD_DIM].T.astype(BF16)
        cd = jnp.exp(gtot)
        cd_ref[0, d, j] = jnp.concatenate([cd[:, hd * HEAD_DIM:(hd + 1) * HEAD_DIM] for hd in heads]
                                          + [jnp.zeros((SUBLANES - HEADS, LANES), F32)], axis=0)


def _delta_prep_call(q, k, v, g):
    b, l, w = k.shape
    nc = l // CHUNK
    cps = math.gcd(PREP_CHUNKS, nc)
    t = cps * CHUNK
    tok = lambda width: pl.BlockSpec((1, t, width), lambda bi, n: (bi, n, 0))
    dtok = lambda width: pl.BlockSpec((1, 2, t, width), lambda bi, n: (bi, 0, n, 0))
    return pl.pallas_call(
        functools.partial(_delta_prep_kernel, cps=cps),
        out_shape=[
            jax.ShapeDtypeStruct((b, 2, l, w), F32),
            jax.ShapeDtypeStruct((b, 2, l, w), BF16),
            jax.ShapeDtypeStruct((b, 2, l, w), BF16),
            jax.ShapeDtypeStruct((b, 2, nc, HEAD_DIM, HEADS * CHUNK), BF16),
            jax.ShapeDtypeStruct((b, 2, l, HEADS * CHUNK), BF16),
            jax.ShapeDtypeStruct((b, 2, nc, SUBLANES, LANES), F32),
        ],
        grid=(b, nc // cps),
        in_specs=[tok(w), tok(w), tok(w), tok(4 * HEADS)],
        out_specs=[dtok(w), dtok(w), dtok(w),
                   pl.BlockSpec((1, 2, cps, HEAD_DIM, HEADS * CHUNK), lambda bi, n: (bi, 0, n, 0, 0)),
                   dtok(HEADS * CHUNK),
                   pl.BlockSpec((1, 2, cps, SUBLANES, LANES), lambda bi, n: (bi, 0, n, 0, 0))],
        compiler_params=_params("arbitrary", "arbitrary"),
        name="delta_prep",
    )(q, k, v, g)


def _delta_scan_kernel(*refs, cps):
    ins = (refs[0:6], refs[6:12])
    s0_ref, of_ref, ob_ref, sfin_ref, s_ref = refs[12:]
    outs = (of_ref, ob_ref)
    n = pl.program_id(1)
    c = CHUNK

    @pl.when(n == 0)
    def _():
        s_ref[...] = s0_ref[0]

    combos = [(d, hd) for d in range(2) for hd in range(HEADS)]
    sl = lambda hd: slice(hd * HEAD_DIM, (hd + 1) * HEAD_DIM)
    cs = lambda hd: slice(hd * c, (hd + 1) * c)
    s = {(d, hd): s_ref[d, hd] for d, hd in combos}
    for step in range(cps):
        pos = (step, cps - 1 - step)
        tok = [slice(pos[d] * c, (pos[d] + 1) * c) for d in range(2)]
        m1, w, m2 = {}, {}, {}
        for d, hd in combos:
            u_ref, wk_ref, qd_ref, kdt_ref, qk_ref, cd_ref = ins[d]
            lhs = jnp.concatenate([wk_ref[0, 0, tok[d], sl(hd)], qd_ref[0, 0, tok[d], sl(hd)]], axis=0)
            m1[d, hd] = _dot(lhs, s[d, hd].astype(BF16))
        for d, hd in combos:
            w[d, hd] = (ins[d][0][0, 0, tok[d], sl(hd)] - m1[d, hd][:c]).astype(BF16)
        for d, hd in combos:
            u_ref, wk_ref, qd_ref, kdt_ref, qk_ref, cd_ref = ins[d]
            lhs = jnp.concatenate([qk_ref[0, 0, tok[d], cs(hd)], kdt_ref[0, 0, pos[d], :, cs(hd)]], axis=0)
            m2[d, hd] = _dot(lhs, w[d, hd])
        for d, hd in combos:
            outs[d][0, tok[d], sl(hd)] = (m1[d, hd][c:] + m2[d, hd][:c]).astype(BF16)
            s[d, hd] = ins[d][5][0, 0, pos[d], hd:hd + 1, :] * s[d, hd] + m2[d, hd][c:]
    for d, hd in combos:
        s_ref[d, hd] = s[d, hd]

    @pl.when(n == pl.num_programs(1) - 1)
    def _():
        sfin_ref[0] = s_ref[...]


def _delta_scan_call(prep, s0):
    u, wk, qd, kdt, qk, cd = prep
    b, _, l, w = u.shape
    cps = math.gcd(SCAN_CHUNKS, l // CHUNK)
    t = cps * CHUNK
    ns = l // t

    def specs(d, blk):
        tok = lambda width: pl.BlockSpec((1, 1, t, width), lambda bi, n: (bi, d, blk(n), 0))
        return [tok(w), tok(w), tok(w),
                pl.BlockSpec((1, 1, cps, HEAD_DIM, HEADS * CHUNK), lambda bi, n: (bi, d, blk(n), 0, 0)),
                tok(HEADS * CHUNK),
                pl.BlockSpec((1, 1, cps, SUBLANES, LANES), lambda bi, n: (bi, d, blk(n), 0, 0))]

    st = pl.BlockSpec((1, 2, HEADS, HEAD_DIM, HEAD_DIM), lambda bi, n: (bi, 0, 0, 0, 0))
    return pl.pallas_call(
        functools.partial(_delta_scan_kernel, cps=cps),
        out_shape=[jax.ShapeDtypeStruct((b, l, w), BF16)] * 2
        + [jax.ShapeDtypeStruct((b, 2, HEADS, HEAD_DIM, HEAD_DIM), F32)],
        grid=(b, ns),
        in_specs=specs(0, lambda n: n) + specs(1, lambda n: ns - 1 - n) + [st],
        out_specs=[pl.BlockSpec((1, t, w), lambda bi, n: (bi, n, 0)),
                   pl.BlockSpec((1, t, w), lambda bi, n: (bi, ns - 1 - n, 0)), st],
        scratch_shapes=[pltpu.VMEM((2, HEADS, HEAD_DIM, HEAD_DIM), F32)],
        compiler_params=_params("arbitrary", "arbitrary"),
        name="delta_scan",
    )(*prep, *prep, s0)


def _ffn_prenorm(x, nffn_ref, sh2_ref, sc2_ref, rt_ref, h2_ref, lg_ref):
    h2 = _norm_mod(x, nffn_ref[...], sh2_ref[0], sc2_ref[0])
    tm, d = x.shape
    pieces = d // LANES
    for j in range(pieces):
        h2_ref[0, pl.ds(j, tm, stride=pieces), :] = h2[:, j * LANES:(j + 1) * LANES]
    lg_ref[0] = _dot_nt(rt_ref[...], h2.astype(BF16))


def _out0_kernel(of_ref, ob_ref, z_ref, p_ref, x_ref, band_ref, cnt_ref, onorm_ref, pw_ref, ps_ref, wout_ref,
                 g1_ref, sh2_ref, sc2_ref, nffn_ref, rt_ref, x1_ref, h2_ref, lg_ref):
    o = of_ref[0].astype(F32) + ob_ref[0].astype(F32)
    z = z_ref[0].astype(F32)
    pin = p_ref[0].astype(F32)
    parts = []
    for hd in range(HEADS):
        sl = slice(hd * HEAD_DIM, (hd + 1) * HEAD_DIM)
        oh = o[:, sl]
        ms = jnp.mean(oh * oh, axis=-1, keepdims=True)
        parts.append(oh * lax.rsqrt(ms + EPS) * onorm_ref[...] * _silu(z[:, sl]))
    pr = band_ref.shape[1]
    for gi in range(len(POOL_WINDOWS)):
        sl = slice(gi * POOL_GROUP, (gi + 1) * POOL_GROUP)
        band = band_ref[gi]
        grp = []
        for r0 in range(0, pin.shape[0], pr):
            u = pin[r0:r0 + pr, sl]
            uh = u.astype(BF16)
            ul = (u - uh.astype(F32)).astype(BF16)
            wsum = _dot(band, uh) + _dot(band, ul)
            grp.append(wsum / cnt_ref[gi] - u)
        parts.append(_dot(jnp.concatenate(grp, axis=0).astype(BF16), pw_ref[gi]) * ps_ref[:, sl])
    cat = jnp.concatenate(parts, axis=1).astype(BF16)
    x1 = x_ref[0] + g1_ref[0] * _dot(cat, wout_ref[...])
    x1_ref[0] = x1
    _ffn_prenorm(x1, nffn_ref, sh2_ref, sc2_ref, rt_ref, h2_ref, lg_ref)


POOL_ROWS = 256


def _pool_tables(tm):
    t = jnp.arange(tm)
    seg = t // GRID_W
    loc = t % GRID_W
    bands, cnts = [], []
    for w in POOL_WINDOWS:
        lo = jnp.clip(loc - w // 2, 0, GRID_W)
        hi = jnp.clip(loc + w - w // 2, 0, GRID_W)
        inside = (seg[:, None] == seg[None, :]) & (loc[None, :] >= lo[:, None]) & (loc[None, :] < hi[:, None])
        bands.append(inside.astype(BF16))
        cnts.append((hi - lo).astype(F32)[:, None])
    return jnp.stack(bands), jnp.stack(cnts)


def _out0_call(o_f, o_b, z, pin, x, onorm, pool_w, pool_scale, w_out, g1, sh2, sc2, nffn, router_t, tm):
    b, l, d = x.shape
    tm = min(tm, l)
    w = HEADS * HEAD_DIM
    e = router_t.shape[0]
    band, cnt = _pool_tables(math.gcd(tm, POOL_ROWS))
    tok = lambda width: pl.BlockSpec((1, tm, width), lambda bi, i: (bi, i, 0))
    vec = pl.BlockSpec((1, 1, d), lambda bi, i: (bi, 0, 0))
    return pl.pallas_call(
        _out0_kernel,
        out_shape=[jax.ShapeDtypeStruct((b, l, d), F32), jax.ShapeDtypeStruct((b, l * d // LANES, LANES), F32),
                   jax.ShapeDtypeStruct((b, e, l), F32)],
        grid=(b, l // tm),
        in_specs=[tok(w), tok(w), tok(w), tok(w), tok(d),
                  _const_spec(band.shape), _const_spec(cnt.shape), _const_spec(onorm.shape),
                  _const_spec(pool_w.shape), _const_spec(pool_scale.shape), _const_spec(w_out.shape),
                  vec, vec, vec, _const_spec(nffn.shape), _const_spec(router_t.shape)],
        out_specs=[tok(d), pl.BlockSpec((1, tm * d // LANES, LANES), lambda bi, i: (bi, i, 0)),
                   pl.BlockSpec((1, e, tm), lambda bi, i: (bi, 0, i))],
        compiler_params=_params("arbitrary", "arbitrary"),
        name="out0",
    )(o_f, o_b, z, pin, x, band, cnt, onorm, pool_w, pool_scale, w_out, g1, sh2, sc2, nffn, router_t)


def _excl_cumsum_lanes(m):
    rows, n = m.shape
    ri = lax.broadcasted_iota(I32, (LANES, LANES), 0)
    ci = lax.broadcasted_iota(I32, (LANES, LANES), 1)
    upper = jnp.where(ri <= ci, 1.0, 0.0).astype(BF16)
    carry = jnp.zeros((rows, 1), F32)
    outs = []
    for blk in range(n // LANES):
        x = m[:, blk * LANES:(blk + 1) * LANES]
        inc = _dot(x.astype(BF16), upper)
        outs.append(inc - x + carry)
        carry = carry + inc[:, LANES - 1:LANES]
    return jnp.concatenate(outs, axis=1)


def _route_kernel(lg_ref, idx_ref, gate_ref, *, cap):
    x = lg_ref[0]
    e, n = x.shape
    ex = jnp.exp(x - jnp.max(x, axis=0, keepdims=True))
    aff = ex / jnp.sum(ex, axis=0, keepdims=True)

    def count_ge(bits):
        return jnp.sum(jnp.where(aff >= pltpu.bitcast(bits, F32), 1.0, 0.0), axis=1, keepdims=True)

    def bisect(_, lohi):
        lo, hi = lohi
        mid = lo + ((hi - lo + 1) >> 1)
        ok = count_ge(mid) >= cap
        return jnp.where(ok, mid, lo), jnp.where(ok, hi, mid - 1)

    lo0 = jnp.zeros((e, 1), I32)
    hi0 = jnp.full((e, 1), 0x7F800000, I32)
    thr, _ = lax.fori_loop(0, 31, bisect, (lo0, hi0))
    above = jnp.where(aff >= pltpu.bitcast(thr + 1, F32), 1.0, 0.0)
    tied = jnp.where(aff >= pltpu.bitcast(thr, F32), 1.0, 0.0) - above
    need = cap - jnp.sum(above, axis=1, keepdims=True)
    sel = above + tied * jnp.where(_excl_cumsum_lanes(tied) < need, 1.0, 0.0)
    pos = _excl_cumsum_lanes(sel)
    tok = lax.broadcasted_iota(I32, (e, n), 1)
    key = jnp.where(sel > 0.0, (tok - pos.astype(I32)) | VALID_BIT, 0)
    shift = 1
    while shift < n:
        mk = pltpu.roll(key, n - shift, axis=1)
        mt = pltpu.roll(tok, n - shift, axis=1)
        ma = pltpu.roll(aff, n - shift, axis=1)
        take = (mk & shift) != 0
        key = jnp.where(take, mk, jnp.where((key & shift) == 0, key, 0))
        tok = jnp.where(take, mt, tok)
        aff = jnp.where(take, ma, aff)
        shift *= 2
    idx_ref[0] = tok[:, :cap]
    gate_ref[0] = aff[:, :cap]


def _route_call(logits_t, cap):
    b, e, n = logits_t.shape
    return pl.pallas_call(
        functools.partial(_route_kernel, cap=cap),
        out_shape=[jax.ShapeDtypeStruct((b, e, cap), I32), jax.ShapeDtypeStruct((b, e, cap), F32)],
        grid=(b,),
        in_specs=[pl.BlockSpec((1, e, n), lambda bi: (bi, 0, 0))],
        out_specs=[pl.BlockSpec((1, e, cap), lambda bi: (bi, 0, 0))] * 2,
        compiler_params=_params("arbitrary"),
        name="route",
    )(logits_t)


GATHER_UNROLL = 8


def _gather_row(idx_ref, src_ref, dst_ref, s):
    t = idx_ref[0, 0, s]
    dst_ref[pl.ds(pl.multiple_of(s * SUBLANES, SUBLANES), SUBLANES), :] = (
        src_ref[0, pl.ds(pl.multiple_of(t * SUBLANES, SUBLANES), SUBLANES), :])


def _moe_up_kernel(idx_ref, idx_next_ref, src_ref, wg_ref, wu_ref, hid_ref, rows0_ref, rows1_ref, *, cap):
    e = pl.program_id(1)

    @pl.when(e == 0)
    def _():
        def gather(cidx, carry):
            for u in range(GATHER_UNROLL):
                _gather_row(idx_ref, src_ref, rows0_ref, cidx * GATHER_UNROLL + u)
            return carry
        lax.fori_loop(0, cap // GATHER_UNROLL, gather, 0)

    def step(cur_ref, next_ref):
        for s in range(cap):
            _gather_row(idx_next_ref, src_ref, next_ref, s)
        x = _from_token_tiles(cur_ref[...], cap).astype(BF16)
        g = _dot(x, wg_ref[0, 0].astype(BF16))
        u = _dot(x, wu_ref[0, 0].astype(BF16))
        hid_ref[0, 0] = (_silu(g) * u).astype(BF16)

    @pl.when(e % 2 == 0)
    def _():
        step(rows0_ref, rows1_ref)

    @pl.when(e % 2 == 1)
    def _():
        step(rows1_ref, rows0_ref)


def _moe_up_call(idx, src, w_gate, w_up, layer, cap):
    b, nrows, _ = src.shape
    _, e, d, f = w_gate.shape
    assert e % 2 == 0
    idx = idx.reshape(b * e, 1, cap)
    rows = pltpu.VMEM((cap * d // LANES, LANES), F32)
    wspec = pl.BlockSpec((1, 1, d, f), lambda bi, ei: (layer, ei, 0, 0))
    return pl.pallas_call(
        functools.partial(_moe_up_kernel, cap=cap),
        out_shape=jax.ShapeDtypeStruct((b, e, cap, f), BF16),
        grid=(b, e),
        in_specs=[
            pl.BlockSpec((1, 1, cap), lambda bi, ei: (bi * e + ei, 0, 0), memory_space=pltpu.SMEM),
            pl.BlockSpec((1, 1, cap), lambda bi, ei: (bi * e + jnp.minimum(ei + 1, e - 1), 0, 0),
                         memory_space=pltpu.SMEM),
            pl.BlockSpec((1, nrows, LANES), lambda bi, ei: (bi, 0, 0), pipeline_mode=pl.Buffered(1)),
            wspec, wspec,
        ],
        out_specs=pl.BlockSpec((1, 1, cap, f), lambda bi, ei: (bi, ei, 0, 0)),
        scratch_shapes=[rows, rows],
        compiler_params=_params("arbitrary", "arbitrary"),
        name="moe_up",
    )(idx, idx, src, w_gate, w_up)


SCATTER_UNROLL = 8
DOWN_SPLIT = 1


def _moe_down_kernel(idx_ref, gate_ref, hid_ref, hid_next_ref, wd_ref, wd_next_ref, acc_ref, y0_ref, y1_ref,
                     *, cap, rows):
    e = pl.program_id(1)

    def project(h_ref, w_ref, y_ref):
        y = _dot(h_ref[0, 0], w_ref[0, 0].astype(BF16))
        for j in range(rows):
            y_ref[pl.ds(j, cap, stride=rows), :] = y[:, j * LANES:(j + 1) * LANES]

    @pl.when(e == 0)
    def _():
        acc_ref[...] = jnp.zeros_like(acc_ref)
        project(hid_ref, wd_ref, y0_ref)

    def step(cur_ref, next_ref):
        project(hid_next_ref, wd_next_ref, next_ref)
        for base in range(0, cap, SCATTER_UNROLL):
            dsts, vals = [], []
            for s in range(base, base + SCATTER_UNROLL):
                t = idx_ref[0, 0, s]
                dst = pl.ds(pl.multiple_of(t * rows, rows), rows)
                dsts.append(dst)
                vals.append(acc_ref[0, dst, :] + gate_ref[0, 0, s] * cur_ref[s * rows:(s + 1) * rows, :])
            for dst, val in zip(dsts, vals):
                acc_ref[0, dst, :] = val

    @pl.when(e % 2 == 0)
    def _():
        step(y0_ref, y1_ref)

    @pl.when(e % 2 == 1)
    def _():
        step(y1_ref, y0_ref)


def _moe_down_call(idx, gate, hid, wd, layer, n):
    b, e, cap, f = hid.shape
    d = wd.shape[3]
    assert e % 2 == 0
    dw = d // DOWN_SPLIT
    rows = dw // LANES
    nxt = lambda ei: jnp.minimum(ei + 1, e - 1)
    slot = lambda bh, ei: ((bh // DOWN_SPLIT) * e + ei, 0, 0)
    y = pltpu.VMEM((cap * rows, LANES), F32)
    out = pl.pallas_call(
        functools.partial(_moe_down_kernel, cap=cap, rows=rows),
        out_shape=jax.ShapeDtypeStruct((b * DOWN_SPLIT, n * rows, LANES), F32),
        grid=(b * DOWN_SPLIT, e),
        in_specs=[
            pl.BlockSpec((1, 1, cap), slot, memory_space=pltpu.SMEM),
            pl.BlockSpec((1, 1, cap), slot, memory_space=pltpu.SMEM),
            pl.BlockSpec((1, 1, cap, f), lambda bh, ei: (bh // DOWN_SPLIT, 0, 0, 0)),
            pl.BlockSpec((1, 1, cap, f), lambda bh, ei: (bh // DOWN_SPLIT, nxt(ei), 0, 0)),
            pl.BlockSpec((1, 1, f, dw), lambda bh, ei: (layer, 0, 0, bh % DOWN_SPLIT)),
            pl.BlockSpec((1, 1, f, dw), lambda bh, ei: (layer, nxt(ei), 0, bh % DOWN_SPLIT)),
        ],
        out_specs=pl.BlockSpec((1, n * rows, LANES), lambda bh, ei: (bh, 0, 0), pipeline_mode=pl.Buffered(1)),
        scratch_shapes=[y, y],
        compiler_params=_params("arbitrary", "arbitrary"),
        name="moe_down",
    )(idx.reshape(b * e, 1, cap), gate.reshape(b * e, 1, cap), hid, hid, wd, wd)
    return out.reshape(b, DOWN_SPLIT, n * rows, LANES)


def _moe(h2, logits_t, w_gate, w_up, wd, layer):
    b, e, n = logits_t.shape
    cap = 2 * n // e
    idx, gate = _route_call(logits_t, cap)
    hid = _moe_up_call(idx, h2, w_gate, w_up, layer, cap)
    return _moe_down_call(idx, gate, hid, wd, layer, n)


def _join(m_ref, tm):
    rows = m_ref.shape[2] // tm
    return jnp.concatenate([m_ref[0, h, pl.ds(j, tm, stride=rows), :]
                            for h in range(DOWN_SPLIT) for j in range(rows)], axis=1)


def _mix1_kernel(xp_ref, x_ref, xn_ref, mp_ref, m_ref, mn_ref, g2p_ref, sh1_ref, sc1_ref, nmix_ref, win_ref,
                 conv_ref, wout_ref, g1_ref, sh2_ref, sc2_ref, nffn_ref, rt_ref, x3_ref, h2_ref, lg_ref, *, tm):
    i = pl.program_id(1)
    last = pl.num_programs(1) - 1
    d = x_ref.shape[2]
    x2 = x_ref[0] + g2p_ref[0] * _join(m_ref, tm)
    x2p = xp_ref[0] + g2p_ref[0] * _join(mp_ref, SUBLANES)
    x2n = xn_ref[0] + g2p_ref[0] * _join(mn_ref, SUBLANES)
    norm = lambda t: _norm_mod(t, nmix_ref[...], sh1_ref[0], sc1_ref[0])
    hb = jnp.concatenate([jnp.where(i == 0, 0.0, norm(x2p)), norm(x2), jnp.where(i == last, 0.0, norm(x2n))],
                         axis=0).astype(BF16)
    proj = _dot(hb, win_ref[...])
    lo = SUBLANES
    u = proj[:, d:2 * d] * proj[:, 2 * d:]
    cw = conv_ref[...]
    rows = tm + 2 * lo
    cv = (pltpu.roll(u, 1, axis=0)[lo:lo + tm] * cw[0:1] + u[lo:lo + tm] * cw[1:2]
          + pltpu.roll(u, rows - 1, axis=0)[lo:lo + tm] * cw[2:3])
    y = _dot((proj[lo:lo + tm, :d] * cv).astype(BF16), wout_ref[...])
    x3 = x2 + g1_ref[0] * y
    x3_ref[0] = x3
    _ffn_prenorm(x3, nffn_ref, sh2_ref, sc2_ref, rt_ref, h2_ref, lg_ref)


def _mix1_call(x1, moe, g2p, sh1, sc1, nmix, w_in, conv_w, w_out, g1, sh2, sc2, nffn, router_t, tm):
    b, l, d = x1.shape
    tm = min(tm, l)
    nb8 = l // SUBLANES
    r = tm // SUBLANES
    e = router_t.shape[0]
    mr = d // DOWN_SPLIT // LANES
    prev = lambda bi, i: (bi, jnp.maximum(i * r - 1, 0), 0)
    nxt = lambda bi, i: (bi, jnp.minimum((i + 1) * r, nb8 - 1), 0)
    tok = pl.BlockSpec((1, tm, d), lambda bi, i: (bi, i, 0))
    vec = pl.BlockSpec((1, 1, d), lambda bi, i: (bi, 0, 0))
    return pl.pallas_call(
        functools.partial(_mix1_kernel, tm=tm),
        out_shape=[jax.ShapeDtypeStruct((b, l, d), F32), jax.ShapeDtypeStruct((b, l * d // LANES, LANES), F32),
                   jax.ShapeDtypeStruct((b, e, l), F32)],
        grid=(b, l // tm),
        in_specs=[
            pl.BlockSpec((1, SUBLANES, d), prev), tok, pl.BlockSpec((1, SUBLANES, d), nxt),
            pl.BlockSpec((1, DOWN_SPLIT, SUBLANES * mr, LANES), lambda bi, i: (bi, 0, jnp.maximum(i * r - 1, 0), 0)),
            pl.BlockSpec((1, DOWN_SPLIT, tm * mr, LANES), lambda bi, i: (bi, 0, i, 0)),
            pl.BlockSpec((1, DOWN_SPLIT, SUBLANES * mr, LANES),
                         lambda bi, i: (bi, 0, jnp.minimum((i + 1) * r, nb8 - 1), 0)),
            vec, vec, vec, _const_spec(nmix.shape), _const_spec(w_in.shape), _const_spec(conv_w.shape),
            _const_spec(w_out.shape), vec, vec, vec, _const_spec(nffn.shape), _const_spec(router_t.shape),
        ],
        out_specs=[tok, pl.BlockSpec((1, tm * d // LANES, LANES), lambda bi, i: (bi, i, 0)),
                   pl.BlockSpec((1, e, tm), lambda bi, i: (bi, 0, i))],
        compiler_params=_params("arbitrary", "arbitrary"),
        name="mix1",
    )(x1, x1, x1, moe, moe, moe, g2p, sh1, sc1, nmix, w_in, conv_w, w_out, g1, sh2, sc2, nffn, router_t)


def _final_kernel(x_ref, m_ref, g2_ref, nf_ref, o_ref):
    x = x_ref[0] + g2_ref[0] * _join(m_ref, x_ref.shape[1])
    ms = jnp.mean(x * x, axis=-1, keepdims=True)
    o_ref[0] = x * lax.rsqrt(ms + EPS) * nf_ref[...]


def _final_call(x3, moe, g2, nf, tm):
    b, l, d = x3.shape
    tm = min(tm, l)
    mr = d // DOWN_SPLIT // LANES
    tok = pl.BlockSpec((1, tm, d), lambda bi, i: (bi, i, 0))
    return pl.pallas_call(
        _final_kernel,
        out_shape=jax.ShapeDtypeStruct((b, l, d), F32),
        grid=(b, l // tm),
        in_specs=[tok, pl.BlockSpec((1, DOWN_SPLIT, tm * mr, LANES), lambda bi, i: (bi, 0, i, 0)),
                  pl.BlockSpec((1, 1, d), lambda bi, i: (bi, 0, 0)), _const_spec(nf.shape)],
        out_specs=tok,
        compiler_params=_params("arbitrary", "arbitrary"),
        name="final",
    )(x3, moe, g2, nf)


def _layer0_mixer_inputs(ev_w_in, dn_a_log, dn_dt_bias):
    w = HEADS * HEAD_DIM
    qkv_w = 3 * w
    wqkv = ev_w_in[:, :qkv_w].astype(BF16)
    wz = ev_w_in[:, qkv_w:qkv_w + w]
    wg = ev_w_in[:, qkv_w + w:qkv_w + w + 4 * HEADS]
    wp = ev_w_in[:, qkv_w + w + 4 * HEADS:]
    wrest = jnp.concatenate([wz, wp, jnp.pad(wg, ((0, 0), (0, LANES - 4 * HEADS)))], axis=1).astype(BF16)
    pad = jnp.zeros((2 * HEADS,), F32)
    tail = jnp.zeros((LANES - 4 * HEADS,), F32)
    gpar = jnp.stack([jnp.concatenate([pad, dn_a_log.reshape(-1), tail]),
                      jnp.concatenate([pad, dn_dt_bias.reshape(-1), tail])])
    return wqkv, wrest, gpar


def kernel(x, c, ctx, c_ctx, ada_w, ada_b, norm_mix, norm_ffn, norm_final, ev_w_in, dn_conv, dn_a_log, dn_dt_bias,
           dn_norm, pool_w, pool_scale, ev_w_out, sc_w_in, sc_conv, sc_w_out, router, w_gate, w_up, w_down):
    b, n, d = x.shape
    depth = ada_w.shape[0]
    assert depth == 2 and b + 1 <= SUBLANES and n % GRID_W == 0 and n % CHUNK == 0 and ctx.shape[1] % CHUNK == 0

    cc = jnp.concatenate([c, c_ctx[None], jnp.zeros((SUBLANES - b - 1, d), F32)], axis=0)
    mods = _ada_call(cc, ada_w, ada_b)

    def mod(layer, k, rows=slice(0, b)):
        return mods[layer, rows, None, k * d:(k + 1) * d]

    ctx_rows = lambda layer, k: jnp.broadcast_to(mods[layer, b:b + 1, None, k * d:(k + 1) * d], (b, 1, d))
    row = lambda v: v.reshape(1, -1)
    router_t = jnp.swapaxes(router, 1, 2).astype(BF16)

    wqkv, wrest, gpar = _layer0_mixer_inputs(ev_w_in[0], dn_a_log[0], dn_dt_bias[0])
    nmix0 = row(norm_mix[0])
    qc, kc, vc, _, _, gc = _proj0_call(ctx, ctx_rows(0, 0), ctx_rows(0, 1), nmix0, wqkv, wrest, dn_conv[0], gpar,
                                      PROJ_TILE)
    ql, kl, vl, zl, pl_in, gl = _proj0_call(x, mod(0, 0), mod(0, 1), nmix0, wqkv, wrest, dn_conv[0], gpar, PROJ_TILE)
    s0 = jnp.zeros((b, 2, HEADS, HEAD_DIM, HEAD_DIM), F32)
    _, _, s_ctx = _delta_scan_call(_delta_prep_call(qc, kc, vc, gc), s0)
    o_f, o_b, _ = _delta_scan_call(_delta_prep_call(ql, kl, vl, gl), s_ctx)
    x1, h2, lg = _out0_call(o_f, o_b, zl, pl_in, x, row(dn_norm[0]), pool_w[0].astype(BF16), row(pool_scale[0]),
                            ev_w_out[0].astype(BF16), mod(0, 2), mod(0, 3), mod(0, 4), row(norm_ffn[0]),
                            router_t[0], OUT0_TILE)
    moe0 = _moe(h2, lg, w_gate, w_up, w_down, 0)

    x3, h2, lg = _mix1_call(x1, moe0, mod(0, 5), mod(1, 0), mod(1, 1), row(norm_mix[1]), sc_w_in[0].astype(BF16),
                            sc_conv[0], sc_w_out[0].astype(BF16), mod(1, 2), mod(1, 3), mod(1, 4),
                            row(norm_ffn[1]), router_t[1], MIX1_TILE)
    moe1 = _moe(h2, lg, w_gate, w_up, w_down, 1)
    return _final_call(x3, moe1, mod(1, 5), row(norm_final), FINAL_TILE)
```
